```python
import math
import jax, jax.numpy as jnp
from jax import lax
import numpy as np

D_MODEL = 1024
BATCH = 4
SEQ = 8192
DEPTH = 1

MIX_WIDTH = D_MODEL
SSD_WIDTH = MIX_WIDTH // 2
SSD_HEAD_DIM = 64
SSD_HEADS = SSD_WIDTH // SSD_HEAD_DIM
SSD_GROUPS = 2
SSD_HEADS_PER_GROUP = SSD_HEADS // SSD_GROUPS
D_STATE = 128
CONV_K = 5
CHUNK = 128
ATTN_WIDTH = MIX_WIDTH - SSD_WIDTH
HEAD_DIM = 64
ATTN_HEADS = ATTN_WIDTH // HEAD_DIM
ATTN_KV_HEADS = 2
Q_PER_KV = ATTN_HEADS // ATTN_KV_HEADS
WINDOW = 128
BLOCK = 128
ROPE_DIMS = HEAD_DIM // 4
ROPE_THETA = 500000.0
D_FF = 2816
N_MOD = 9
EPS = 1e-6
DT_MIN = 0.001
DT_MAX = 0.1
A_MIN = 1.0
A_MAX = 16.0

CONV_DIM = SSD_WIDTH + 2 * SSD_GROUPS * D_STATE
N_DT = 2 * SSD_HEADS
KV_WIDTH = ATTN_KV_HEADS * HEAD_DIM
S_Z = SSD_WIDTH
S_XBC = S_Z + CONV_DIM
S_DT = S_XBC + N_DT
S_Q = S_DT + ATTN_WIDTH
S_K = S_Q + KV_WIDTH
IN_WIDTH = S_K + KV_WIDTH

kernel_name = "hybrid_ssd_swa_macaron_adaln_block"


def _rms(t):
    tf = t.astype(jnp.float32)
    return tf * lax.rsqrt(jnp.mean(tf * tf, axis=-1, keepdims=True) + EPS)


def _ada_norm(h, gain, shift, scale):
    return _rms(h) * gain * (1.0 + scale) + shift


def _swiglu(u, wg, wu, wd):
    return (jax.nn.silu(u @ wg) * (u @ wu)) @ wd


def _ssd_chunked(xs, dt, a, bm, cm, strict):
    b_, s_, g, r, p = xs.shape
    n = bm.shape[-1]
    nc = s_ // CHUNK
    da = (dt * a).reshape(b_, nc, CHUNK, g, r)
    xdt = (xs * dt[..., None]).reshape(b_, nc, CHUNK, g, r, p)
    bc = bm.reshape(b_, nc, CHUNK, g, n)
    cc = cm.reshape(b_, nc, CHUNK, g, n)
    cs = jnp.cumsum(jnp.moveaxis(da, (1, 2), (3, 4)), axis=-1)
    seg = cs[..., :, None] - cs[..., None, :]
    mask = jnp.tril(jnp.ones((CHUNK, CHUNK), dtype=bool), k=-1 if strict else 0)
    lmat = jnp.exp(jnp.where(mask, seg, -jnp.inf))
    cb = jnp.einsum('bclgn,bcsgn->bgcls', cc, bc)
    y_diag = jnp.einsum('bgcls,bgrcls,bcsgrp->bclgrp', cb, lmat, xdt)
    decay_states = jnp.exp(cs[..., -1:] - cs)
    states = jnp.einsum('bclgn,bgrcl,bclgrp->bcgrpn', bc, decay_states, xdt)
    chunk_decay = jnp.exp(cs[..., -1])

    def step(hstate, inp):
        st, dec = inp
        return hstate * dec[..., None, None] + st, hstate

    h0 = jnp.zeros((b_, g, r, p, n), jnp.float32)
    _, h_in = lax.scan(step, h0, (jnp.moveaxis(states, 1, 0), jnp.moveaxis(chunk_decay, -1, 0)))
    h_in = jnp.moveaxis(h_in, 0, 1)
    y_off = jnp.einsum('bclgn,bcgrpn,bgrcl->bclgrp', cc, h_in, jnp.exp(cs))
    return (y_diag + y_off).reshape(b_, s_, g, r, p)


def _partial_rope(t, positions):
    half = ROPE_DIMS // 2
    inv = ROPE_THETA ** (-jnp.arange(half, dtype=jnp.float32) * 2.0 / ROPE_DIMS)
    ang = positions.astype(jnp.float32)[:, :, None, None] * inv
    cos, sin = jnp.cos(ang), jnp.sin(ang)
    t1, t2, rest = t[..., :half], t[..., half:ROPE_DIMS], t[..., ROPE_DIMS:]
    return jnp.concatenate([t1 * cos - t2 * sin, t2 * cos + t1 * sin, rest], axis=-1)


def _window_attention(q, k, v, sink_logit):
    b_, s_ = q.shape[:2]
    nb = s_ // BLOCK
    qb = q.reshape(b_, nb, BLOCK, ATTN_KV_HEADS, Q_PER_KV, HEAD_DIM)

    def band(t):
        tp = jnp.pad(t, ((0, 0), (BLOCK, BLOCK), (0, 0), (0, 0)))
        tp = tp.reshape(b_, nb + 2, BLOCK, ATTN_KV_HEADS, HEAD_DIM)
        return jnp.concatenate([tp[:, :-2], tp[:, 1:-1], tp[:, 2:]], axis=2)

    kb, vb = band(k), band(v)
    scores = jnp.einsum('bnqkgd,bnjkd->bnkgqj', qb, kb) * (HEAD_DIM ** -0.5)
    qi = jnp.arange(nb)[:, None] * BLOCK + jnp.arange(BLOCK)[None, :]
    kj = jnp.arange(nb)[:, None] * BLOCK - BLOCK + jnp.arange(3 * BLOCK)[None, :]
    valid = (jnp.abs(qi[:, :, None] - kj[:, None, :]) <= WINDOW) & ((kj >= 0) & (kj < s_))[:, None, :]
    scores = jnp.where(valid[None, :, None, None], scores.astype(jnp.float32), -jnp.inf)
    sink = sink_logit.astype(jnp.float32).reshape(ATTN_KV_HEADS, Q_PER_KV)[:, :, None, None]
    m = jnp.maximum(jnp.max(scores, axis=-1, keepdims=True), sink)
    pr = jnp.exp(scores - m)
    denom = jnp.sum(pr, axis=-1, keepdims=True) + jnp.exp(sink - m)
    out = jnp.einsum('bnkgqj,bnjkd->bnqkgd', pr / denom, vb)
    return out.reshape(b_, s_, ATTN_WIDTH)


def _token_mix(u, positions, w_in, conv_w, conv_b, dt_bias, a_log, d_skip, ssd_norm_w,
               q_norm_w, k_norm_w, sink_logit, w_out):
    b_, s_, _ = u.shape
    proj = u @ w_in
    z, xbc, dt_raw, q, k, v = jnp.split(proj, [S_Z, S_XBC, S_DT, S_Q, S_K], axis=-1)
    pad = CONV_K // 2
    xbc = lax.conv_general_dilated(xbc, conv_w.astype(xbc.dtype)[:, None, :], (1,), [(pad, pad)],
                                   dimension_numbers=('NWC', 'WIO', 'NWC'),
                                   feature_group_count=CONV_DIM) + conv_b
    xbc = jax.nn.silu(xbc)
    xs, bm, cm = jnp.split(xbc, [SSD_WIDTH, SSD_WIDTH + SSD_GROUPS * D_STATE], axis=-1)
    xs = xs.reshape(b_, s_, SSD_GROUPS, SSD_HEADS_PER_GROUP, SSD_HEAD_DIM)
    bm = bm.reshape(b_, s_, SSD_GROUPS, D_STATE)
    cm = cm.reshape(b_, s_, SSD_GROUPS, D_STATE)
    dt = jax.nn.softplus(dt_raw.reshape(b_, s_, 2, SSD_GROUPS, SSD_HEADS_PER_GROUP)
                         + dt_bias.reshape(2, SSD_GROUPS, SSD_HEADS_PER_GROUP))
    a = -jnp.exp(a_log.astype(jnp.float32)).reshape(2, SSD_GROUPS, SSD_HEADS_PER_GROUP)
    y_fwd = _ssd_chunked(xs, dt[:, :, 0], a[0], bm, cm, strict=False)
    flip = lambda t: jnp.flip(t, axis=1)
    y_bwd = flip(_ssd_chunked(flip(xs), flip(dt[:, :, 1]), a[1], flip(bm), flip(cm), strict=True))
    y = y_fwd + y_bwd + xs * d_skip.reshape(SSD_GROUPS, SSD_HEADS_PER_GROUP)[:, :, None]
    y = y.reshape(b_, s_, SSD_WIDTH) * jax.nn.silu(z)
    y_ssd = _rms(y) * ssd_norm_w
    q = q.reshape(b_, s_, ATTN_HEADS, HEAD_DIM)
    k = k.reshape(b_, s_, ATTN_KV_HEADS, HEAD_DIM)
    v = v.reshape(b_, s_, ATTN_KV_HEADS, HEAD_DIM)
    q = _partial_rope(_rms(q) * q_norm_w, positions)
    k = _partial_rope(_rms(k) * k_norm_w, positions)
    q = q.reshape(b_, s_, ATTN_KV_HEADS, Q_PER_KV, HEAD_DIM)
    y_attn = _window_attention(q, k, v, sink_logit)
    return jnp.concatenate([y_ssd, y_attn], axis=-1) @ w_out


def setup_inputs(seed: int = 0) -> dict:
    key = jax.random.key(seed)
    ks = jax.random.split(key, 24)
    f32 = jnp.float32
    L = DEPTH

    def nrm(k, shape, s):
        return jax.random.normal(k, shape, f32) * s

    x = nrm(ks[0], (BATCH, SEQ, D_MODEL), 1.0)
    c = nrm(ks[1], (BATCH, D_MODEL), 1.0)
    positions = jnp.tile(jnp.arange(SEQ, dtype=jnp.int32)[None, :], (BATCH, 1))
    w_ada = nrm(ks[2], (L, D_MODEL, N_MOD * D_MODEL), 0.5 * D_MODEL ** -0.5)
    b_ada = nrm(ks[3], (L, N_MOD * D_MODEL), 0.01)
    norm_ffn1 = 1.0 + nrm(ks[4], (L, D_MODEL), 0.01)
    ffn1_wg = nrm(ks[5], (L, D_MODEL, D_FF), D_MODEL ** -0.5)
    ffn1_wu = nrm(ks[6], (L, D_MODEL, D_FF), D_MODEL ** -0.5)
    ffn1_wd = nrm(ks[7], (L, D_FF, D_MODEL), D_FF ** -0.5)
    norm_mix = 1.0 + nrm(ks[8], (L, D_MODEL), 0.01)
    w_in = nrm(ks[9], (L, D_MODEL, IN_WIDTH), D_MODEL ** -0.5)
    conv_w = nrm(ks[10], (L, CONV_K, CONV_DIM), CONV_K ** -0.5)
    conv_b = nrm(ks[11], (L, CONV_DIM), 0.01)
    dt0 = jnp.exp(jax.random.uniform(ks[12], (L, 2, SSD_HEADS), f32, math.log(DT_MIN), math.log(DT_MAX)))
    dt_bias = dt0 + jnp.log(-jnp.expm1(-dt0))
    a_log = jnp.log(jax.random.uniform(ks[13], (L, 2, SSD_HEADS), f32, A_MIN, A_MAX))
    d_skip = 1.0 + nrm(ks[14], (L, SSD_HEADS), 0.01)
    ssd_norm_w = 1.0 + nrm(ks[15], (L, SSD_WIDTH), 0.01)
    q_norm_w = 1.0 + nrm(ks[16], (L, HEAD_DIM), 0.01)
    k_norm_w = 1.0 + nrm(ks[17], (L, HEAD_DIM), 0.01)
    sink_logit = nrm(ks[18], (L, ATTN_HEADS), 0.5)
    w_out = nrm(ks[19], (L, MIX_WIDTH, D_MODEL), MIX_WIDTH ** -0.5)
    norm_ffn2 = 1.0 + nrm(ks[20], (L, D_MODEL), 0.01)
    ffn2_wg = nrm(ks[21], (L, D_MODEL, D_FF), D_MODEL ** -0.5)
    ffn2_wu = nrm(ks[22], (L, D_MODEL, D_FF), D_MODEL ** -0.5)
    ffn2_wd = nrm(ks[23], (L, D_FF, D_MODEL), D_FF ** -0.5)
    return {"x": x, "c": c, "positions": positions, "w_ada": w_ada, "b_ada": b_ada,
            "norm_ffn1": norm_ffn1, "ffn1_wg": ffn1_wg, "ffn1_wu": ffn1_wu, "ffn1_wd": ffn1_wd,
            "norm_mix": norm_mix, "w_in": w_in, "conv_w": conv_w, "conv_b": conv_b,
            "dt_bias": dt_bias, "a_log": a_log, "d_skip": d_skip, "ssd_norm_w": ssd_norm_w,
            "q_norm_w": q_norm_w, "k_norm_w": k_norm_w, "sink_logit": sink_logit, "w_out": w_out,
            "norm_ffn2": norm_ffn2, "ffn2_wg": ffn2_wg, "ffn2_wu": ffn2_wu, "ffn2_wd": ffn2_wd}


def reference(x, c, positions, w_ada, b_ada, norm_ffn1, ffn1_wg, ffn1_wu, ffn1_wd, norm_mix,
              w_in, conv_w, conv_b, dt_bias, a_log, d_skip, ssd_norm_w, q_norm_w, k_norm_w,
              sink_logit, w_out, norm_ffn2, ffn2_wg, ffn2_wu, ffn2_wd):
    h = x.astype(jnp.float32)
    cs = jax.nn.silu(c.astype(jnp.float32))
    b_ = c.shape[0]
    for l in range(DEPTH):
        mod = (cs @ w_ada[l] + b_ada[l]).reshape(b_, N_MOD, 1, D_MODEL)
        sh1, sc1, g1, sh2, sc2, g2, sh3, sc3, g3 = [mod[:, i] for i in range(N_MOD)]
        h = h + 0.5 * (1.0 + g1) * _swiglu(_ada_norm(h, norm_ffn1[l], sh1, sc1),
                                           ffn1_wg[l], ffn1_wu[l], ffn1_wd[l])
        u = _ada_norm(h, norm_mix[l], sh2, sc2)
        h = h + (1.0 + g2) * _token_mix(u, positions, w_in[l], conv_w[l], conv_b[l], dt_bias[l],
                                        a_log[l], d_skip[l], ssd_norm_w[l], q_norm_w[l],
                                        k_norm_w[l], sink_logit[l], w_out[l])
        h = h + 0.5 * (1.0 + g3) * _swiglu(_ada_norm(h, norm_ffn2[l], sh3, sc3),
                                           ffn2_wg[l], ffn2_wu[l], ffn2_wd[l])
    return h.astype(x.dtype)
```

```python
import functools

import jax
import jax.numpy as jnp
import numpy as np
from jax import lax
from jax.experimental import pallas as pl
from jax.experimental.pallas import tpu as pltpu

F32 = jnp.float32
BF16 = jnp.bfloat16

SSD_HEAD_DIM = 64
SSD_HEADS = 8
SSD_GROUPS = 2
D_STATE = 128
CONV_K = 5
CHUNK = 128
HEAD_DIM = 64
ATTN_HEADS = 8
ATTN_KV_HEADS = 2
WINDOW = 128
BLOCK = 128
ROPE_DIMS = 16
ROPE_THETA = 500000.0
N_MOD = 9
EPS = 1e-6

LANES = 128
SUBLANES = 8
VMEM_LIMIT_BYTES = 56 * 1024 * 1024

NEG_BIG = -1e30


def _dot(a, b):
    return jnp.dot(a, b, preferred_element_type=F32)


def _dot_nt(a, b):
    return lax.dot_general(a, b, (((1,), (1,)), ((), ())), preferred_element_type=F32)


def _split_dot(x, m_bf16, terms=3, lhs=True):
    acc = None
    r = x
    for t in range(terms):
        h = r.astype(BF16)
        d = _dot(h, m_bf16) if lhs else _dot(m_bf16, h)
        acc = d if acc is None else acc + d
        if t + 1 < terms:
            r = r - h.astype(F32)
    return acc


def _silu(x):
    return x * jax.nn.sigmoid(x)


def _iota2(shape, dim):
    return lax.broadcasted_iota(jnp.int32, shape, dim)


def _const_spec(shape):
    nd = len(shape)
    return pl.BlockSpec(shape, lambda *_: (0,) * nd, pipeline_mode=pl.Buffered(1))


def _mod_kernel(c_ref, w_ref, b_ref, o_ref):
    cs = _silu(c_ref[...])
    o_ref[...] = _dot(cs, w_ref[...]) + b_ref[...]


def _adaln_mod(c_pad, w_ada, b_ada):
    rows, d = c_pad.shape
    n = w_ada.shape[1]
    bn = d
    return pl.pallas_call(
        _mod_kernel,
        grid=(n // bn,),
        in_specs=[pl.BlockSpec((rows, d), lambda j: (0, 0)),
                  pl.BlockSpec((d, bn), lambda j: (0, j)),
                  pl.BlockSpec((1, bn), lambda j: (0, j))],
        out_specs=pl.BlockSpec((rows, bn), lambda j: (0, j)),
        out_shape=jax.ShapeDtypeStruct((rows, n), F32),
        name="adaln_mod",
    )(c_pad, w_ada, b_ada.reshape(1, n))


def _rope_kernel(inv_freq, pos_ref, cos_ref, sin_ref):
    p = pos_ref[...].astype(F32)
    for j, inv in enumerate(inv_freq):
        ang = p * inv
        cos_ref[j] = jnp.cos(ang)
        sin_ref[j] = jnp.sin(ang)


def _rope_table(positions):
    half = ROPE_DIMS // 2
    inv_freq = tuple(float(v) for v in
                     np.float32(ROPE_THETA) ** (-np.arange(half, dtype=np.float32) * np.float32(2.0)
                                                / np.float32(ROPE_DIMS)))
    b, s = positions.shape
    rows = b * s // LANES
    pos2 = positions.reshape(rows, LANES)
    cos_t, sin_t = pl.pallas_call(
        functools.partial(_rope_kernel, inv_freq),
        out_shape=[jax.ShapeDtypeStruct((half, rows, LANES), F32)] * 2,
        name="rope_table",
    )(pos2)
    cos_t = cos_t.reshape(half, b, s).transpose(1, 2, 0)
    sin_t = sin_t.reshape(half, b, s).transpose(1, 2, 0)
    ones = jnp.ones((b, s, HEAD_DIM - ROPE_DIMS), F32)
    zeros = jnp.zeros((b, s, HEAD_DIM - ROPE_DIMS), F32)
    cos_p = jnp.concatenate([cos_t, cos_t, ones] * 2, axis=-1)
    sin_p = jnp.concatenate([sin_t, sin_t, zeros] * 2, axis=-1)
    return cos_p, sin_p


def _ada_norm(x, gain_scale, shift):
    ms = jnp.mean(x * x, axis=-1, keepdims=True)
    return x * lax.rsqrt(ms + EPS) * gain_scale + shift


def _swiglu(ub, wg_ref, wu_ref, wd_ref, ff_chunk):
    d_ff = wg_ref.shape[1]
    acc = None
    for c0 in range(0, d_ff, ff_chunk):
        c1 = min(c0 + ff_chunk, d_ff)
        g = _dot(ub, wg_ref[:, c0:c1])
        up = _dot(ub, wu_ref[:, c0:c1])
        a = (_silu(g) * up).astype(BF16)
        d = _dot(a, wd_ref[c0:c1, :])
        acc = d if acc is None else acc + d
    return acc


def _inproj_layout(d_model):
    ssd_w = SSD_HEADS * SSD_HEAD_DIM
    conv_dim = ssd_w + 2 * SSD_GROUPS * D_STATE
    attn_w = ATTN_HEADS * HEAD_DIM
    kv_w = ATTN_KV_HEADS * HEAD_DIM
    return (("z", ssd_w), ("xbc", conv_dim), ("q", attn_w), ("k", kv_w), ("v", kv_w), ("dt", LANES))


def _ffn1_kernel(ff_chunk, widths, x_ref, mod_ref, n1_ref, n2_ref, wg_ref, wu_ref, wd_ref, win_ref,
                 h_ref, *proj_refs):
    x = x_ref[0]
    mod = mod_ref[0]
    u = _ada_norm(x, n1_ref[...] * (1.0 + mod[1:2]), mod[0:1]).astype(BF16)
    ff = _swiglu(u, wg_ref, wu_ref, wd_ref, ff_chunk)
    h = x + (0.5 * (1.0 + mod[2:3])) * ff
    h_ref[0] = h
    u2 = _ada_norm(h, n2_ref[...] * (1.0 + mod[4:5]), mod[3:4]).astype(BF16)
    c0 = 0
    for ref, w in zip(proj_refs, widths):
        ref[0] = _dot(u2, win_ref[:, c0:c0 + w]).astype(ref.dtype)
        c0 += w


def _ffn1_inproj(x, mod, norm1, norm2, wg, wu, wd, w_in_p, *, tm, ff_chunk):
    b, s, d = x.shape
    layout = _inproj_layout(d)
    widths = tuple(w for _, w in layout)
    d_ff = wg.shape[1]
    tok = lambda w: pl.BlockSpec((1, tm, w), lambda bi, i: (bi, i, 0))
    out_shapes = [jax.ShapeDtypeStruct((b, s, d), F32)] + [jax.ShapeDtypeStruct((b, s, w), F32) for w in widths]
    out_specs = [tok(d)] + [tok(w) for w in widths]
    return pl.pallas_call(
        functools.partial(_ffn1_kernel, ff_chunk, widths),
        grid=(b, s // tm),
        in_specs=[tok(d),
                  pl.BlockSpec((1, N_MOD, d), lambda bi, i: (bi, 0, 0)),
                  _const_spec((1, d)), _const_spec((1, d)),
                  _const_spec((d, d_ff)), _const_spec((d, d_ff)), _const_spec((d_ff, d)),
                  _const_spec(w_in_p.shape)],
        out_specs=out_specs,
        out_shape=out_shapes,
        compiler_params=pltpu.CompilerParams(dimension_semantics=("arbitrary", "arbitrary"),
                                             vmem_limit_bytes=VMEM_LIMIT_BYTES),
        name="ffn1_inproj",
    )(x, mod, norm1, norm2, wg, wu, wd, w_in_p)


def _ffn2_kernel(ff_chunk, h_ref, ys_ref, ya_ref, mod_ref, n3_ref, wo_ref, wg_ref, wu_ref, wd_ref, o_ref):
    h1 = h_ref[0]
    mod = mod_ref[0]
    half = ys_ref.shape[-1]
    mix = _dot(ys_ref[0].astype(BF16), wo_ref[0:half, :]) + _dot(ya_ref[0].astype(BF16), wo_ref[half:, :])
    h2 = h1 + (1.0 + mod[5:6]) * mix
    u = _ada_norm(h2, n3_ref[...] * (1.0 + mod[7:8]), mod[6:7]).astype(BF16)
    ff = _swiglu(u, wg_ref, wu_ref, wd_ref, ff_chunk)
    o_ref[0] = h2 + (0.5 * (1.0 + mod[8:9])) * ff


def _outproj_ffn2(h1, y_ssd, y_attn, mod, norm3, w_out, wg, wu, wd, *, tm, ff_chunk):
    b, s, d = h1.shape
    d_ff = wg.shape[1]
    hw = y_ssd.shape[-1]
    tok = lambda w: pl.BlockSpec((1, tm, w), lambda bi, i: (bi, i, 0))
    return pl.pallas_call(
        functools.partial(_ffn2_kernel, ff_chunk),
        grid=(b, s // tm),
        in_specs=[tok(d), tok(hw), tok(hw),
                  pl.BlockSpec((1, N_MOD, d), lambda bi, i: (bi, 0, 0)),
                  _const_spec((1, d)),
                  _const_spec(w_out.shape),
                  _const_spec((d, d_ff)), _const_spec((d, d_ff)), _const_spec((d_ff, d))],
        out_specs=tok(d),
        out_shape=jax.ShapeDtypeStruct((b, s, d), F32),
        compiler_params=pltpu.CompilerParams(dimension_semantics=("arbitrary", "arbitrary"),
                                             vmem_limit_bytes=VMEM_LIMIT_BYTES),
        name="outproj_ffn2",
    )(h1, y_ssd, y_attn, mod, norm3, w_out, wg, wu, wd)


def _tri(lower_inclusive):
    r = _iota2((CHUNK, CHUNK), 0)
    c = _iota2((CHUNK, CHUNK), 1)
    m = (c <= r) if lower_inclusive else (c >= r)
    return jnp.where(m, 1.0, 0.0).astype(BF16)


def _expand_matrix(lane_offset, width):
    r = _iota2((LANES, width), 0)
    c = _iota2((LANES, width), 1)
    return jnp.where((c // SSD_HEAD_DIM) + lane_offset == r, 1.0, 0.0).astype(BF16)


def _dt_and_da(dt_raw, dtb_ref, alog_ref):
    lane = _iota2(dt_raw.shape, 1)
    dt = jnp.where(lane < 2 * SSD_HEADS, jax.nn.softplus(dt_raw + dtb_ref[...]), 0.0)
    a = -jnp.exp(alog_ref[...])
    return dt, dt * a


def _ssd_chunk(direction, cs, csx, cm, bm, xdt, h_ref):
    ssd_w = SSD_HEADS * SSD_HEAD_DIM
    gw = ssd_w // SSD_GROUPS
    lane0 = 0 if direction == "fwd" else SSD_HEADS
    tot_row = CHUNK - 1 if direction == "fwd" else 0
    tot = csx[tot_row:tot_row + 1, :]
    e_in = jnp.exp(csx)
    xdec = (xdt * jnp.exp(tot - csx)).astype(BF16)
    cdec = jnp.exp(tot)
    cst = cs.T
    row = _iota2((CHUNK, CHUNK), 0)
    col = _iota2((CHUNK, CHUNK), 1)
    keep = (col <= row) if direction == "fwd" else (col > row)
    half = _iota2((CHUNK, LANES), 1) // SSD_HEAD_DIM
    xdt_b = xdt.astype(BF16)
    ys = []
    for g in range(SSD_GROUPS):
        cg = cm[:, g * D_STATE:(g + 1) * D_STATE].astype(BF16)
        bg = bm[:, g * D_STATE:(g + 1) * D_STATE]
        cb = _dot_nt(cg, bg.astype(BF16))
        hg = h_ref[:, g * gw:(g + 1) * gw]
        y_off = _dot(cg, hg.astype(BF16)) * e_in[:, g * gw:(g + 1) * gw]
        pairs = []
        for pp in range(gw // LANES):
            p = g * (gw // LANES) + pp
            xp = xdt_b[:, p * LANES:(p + 1) * LANES]
            yp = None
            for e in range(2):
                hcol = lane0 + 2 * p + e
                seg = cs[:, hcol:hcol + 1] - cst[hcol:hcol + 1, :]
                m = (cb * jnp.exp(jnp.where(keep, seg, NEG_BIG))).astype(BF16)
                d = _dot(m, jnp.where(half == e, xp, jnp.zeros_like(xp)))
                yp = d if yp is None else yp + d
            pairs.append(yp)
        ys.append(jnp.concatenate(pairs, axis=1) + y_off)
        bgt = bg.astype(F32).T.astype(BF16)
        st = _dot(bgt, xdec[:, g * gw:(g + 1) * gw])
        h_ref[:, g * gw:(g + 1) * gw] = hg * cdec[:, g * gw:(g + 1) * gw] + st
    return jnp.concatenate(ys, axis=1)


def _ssd_fwd_kernel(xm_ref, xp_ref, xn_ref, dt_ref, cw_ref, cbias_ref, dtb_ref, alog_ref, dskip_ref,
                    bc_ref, xdtb_ref, yp_ref, h_ref):
    c = pl.program_id(1)
    nc = pl.num_programs(1)
    ssd_w = SSD_HEADS * SSD_HEAD_DIM

    @pl.when(c == 0)
    def _():
        h_ref[...] = jnp.zeros_like(h_ref)

    prev = jnp.where(c > 0, xp_ref[0], 0.0)
    nxt = jnp.where(c < nc - 1, xn_ref[0], 0.0)
    xpad = jnp.concatenate([prev, xm_ref[0], nxt], axis=0)
    pad = CONV_K // 2
    acc = cbias_ref[...]
    for k in range(CONV_K):
        o = SUBLANES - pad + k
        acc = acc + xpad[o:o + CHUNK, :] * cw_ref[k:k + 1, :]
    act = _silu(acc)
    xs = act[:, :ssd_w]
    bm = act[:, ssd_w:ssd_w + SSD_GROUPS * D_STATE]
    cm = act[:, ssd_w + SSD_GROUPS * D_STATE:]

    dt, da = _dt_and_da(dt_ref[0], dtb_ref, alog_ref)
    cs = _split_dot(da, _tri(True), lhs=False)
    e_f = _expand_matrix(0, ssd_w)
    e_b = _expand_matrix(SSD_HEADS, ssd_w)
    csx = _split_dot(cs, e_f)
    xdt = xs * _split_dot(dt, e_f)
    xdtb_ref[0] = (xs * _split_dot(dt, e_b)).astype(BF16)
    bc_ref[0] = jnp.concatenate([bm, cm], axis=1).astype(BF16)

    y = _ssd_chunk("fwd", cs, csx, cm, bm, xdt, h_ref)
    yp_ref[0] = y + xs * dskip_ref[...]


def _ssd_bwd_kernel(bc_ref, xdtb_ref, yp_ref, dt_ref, z_ref, dtb_ref, alog_ref, nw_ref, o_ref, h_ref):
    c = pl.program_id(1)
    ssd_w = SSD_HEADS * SSD_HEAD_DIM

    @pl.when(c == 0)
    def _():
        h_ref[...] = jnp.zeros_like(h_ref)

    bc = bc_ref[0]
    bm = bc[:, :SSD_GROUPS * D_STATE]
    cm = bc[:, SSD_GROUPS * D_STATE:]
    _, da = _dt_and_da(dt_ref[0], dtb_ref, alog_ref)
    cs = _split_dot(da, _tri(False), lhs=False)
    csx = _split_dot(cs, _expand_matrix(SSD_HEADS, ssd_w))
    y = _ssd_chunk("bwd", cs, csx, cm, bm, xdtb_ref[0].astype(F32), h_ref)
    y = (y + yp_ref[0]) * _silu(z_ref[0])
    ms = jnp.mean(y * y, axis=-1, keepdims=True)
    o_ref[0] = (y * lax.rsqrt(ms + EPS) * nw_ref[...]).astype(o_ref.dtype)


def _ssd(xbc, dt, z, conv_w8, conv_b, dtb, alog, dskip, norm_w):
    b, s, conv_dim = xbc.shape
    nc = s // CHUNK
    ssd_w = SSD_HEADS * SSD_HEAD_DIM
    per8 = CHUNK // SUBLANES
    n8 = s // SUBLANES
    fwd = lambda w: pl.BlockSpec((1, CHUNK, w), lambda bi, c: (bi, c, 0))
    rev = lambda w: pl.BlockSpec((1, CHUNK, w), lambda bi, c: (bi, nc - 1 - c, 0))
    params = pltpu.CompilerParams(dimension_semantics=("arbitrary", "arbitrary"))
    bc, xdtb, ypart = pl.pallas_call(
        _ssd_fwd_kernel,
        grid=(b, nc),
        in_specs=[fwd(conv_dim),
                  pl.BlockSpec((1, SUBLANES, conv_dim), lambda bi, c: (bi, jnp.maximum(c * per8 - 1, 0), 0)),
                  pl.BlockSpec((1, SUBLANES, conv_dim), lambda bi, c: (bi, jnp.minimum((c + 1) * per8, n8 - 1), 0)),
                  fwd(LANES),
                  _const_spec(conv_w8.shape), _const_spec((1, conv_dim)),
                  _const_spec((1, LANES)), _const_spec((1, LANES)), _const_spec((1, ssd_w))],
        out_specs=[fwd(2 * SSD_GROUPS * D_STATE), fwd(ssd_w), fwd(ssd_w)],
        out_shape=[jax.ShapeDtypeStruct((b, s, 2 * SSD_GROUPS * D_STATE), BF16),
                   jax.ShapeDtypeStruct((b, s, ssd_w), BF16),
                   jax.ShapeDtypeStruct((b, s, ssd_w), F32)],
        scratch_shapes=[pltpu.VMEM((D_STATE, ssd_w), F32)],
        compiler_params=params,
        name="ssd_fwd",
    )(xbc, xbc, xbc, dt, conv_w8, conv_b, dtb, alog, dskip)
    return pl.pallas_call(
        _ssd_bwd_kernel,
        grid=(b, nc),
        in_specs=[rev(2 * SSD_GROUPS * D_STATE), rev(ssd_w), rev(ssd_w), rev(LANES), rev(ssd_w),
                  _const_spec((1, LANES)), _const_spec((1, LANES)), _const_spec((1, ssd_w))],
        out_specs=rev(ssd_w),
        out_shape=jax.ShapeDtypeStruct((b, s, ssd_w), F32),
        scratch_shapes=[pltpu.VMEM((D_STATE, ssd_w), F32)],
        compiler_params=params,
        name="ssd_bwd",
    )(bc, xdtb, ypart, dt, z, dtb, alog, norm_w)


def _headnorm_rope(t, w, cos, sin):
    r = _iota2((LANES, LANES), 0) // HEAD_DIM
    c = _iota2((LANES, LANES), 1) // HEAD_DIM
    blockdiag = jnp.where(r == c, 1.0, 0.0).astype(BF16)
    ms = _split_dot(t * t, blockdiag) * (1.0 / HEAD_DIM)
    tn = t * lax.rsqrt(ms + EPS) * w
    half = ROPE_DIMS // 2
    left = pltpu.roll(tn, LANES - half, axis=1)
    right = pltpu.roll(tn, half, axis=1)
    lane = _iota2(tn.shape, 1) % HEAD_DIM
    rot = jnp.where(lane < half, -left, right)
    return tn * cos + rot * sin


def _attn_kernel(q_ref, kp_ref, kc_ref, kn_ref, vp_ref, vc_ref, vn_ref,
                 cp_ref, cc_ref, cn_ref, sp_ref, sc_ref, sn_ref, qw_ref, kw_ref, sink_ref, o_ref):
    i = pl.program_id(1)
    nb = pl.num_programs(1)
    kb = jnp.concatenate([kp_ref[0], kc_ref[0], kn_ref[0]], axis=0)
    cosb = jnp.concatenate([cp_ref[0], cc_ref[0], cn_ref[0]], axis=0)
    sinb = jnp.concatenate([sp_ref[0], sc_ref[0], sn_ref[0]], axis=0)
    kb = _headnorm_rope(kb, kw_ref[...], cosb, sinb)
    vb = jnp.concatenate([vp_ref[0], vc_ref[0], vn_ref[0]], axis=0)

    lo = _iota2(kb.shape, 1) < HEAD_DIM
    kb_sw = pltpu.roll(kb, HEAD_DIM, axis=1)
    vb_sw = pltpu.roll(vb, HEAD_DIM, axis=1)

    def place(t, t_sw, g, e):
        src = t if g == e else t_sw
        return (jnp.where(lo, src, 0.0) if e == 0 else jnp.where(lo, 0.0, src)).astype(BF16)

    row = _iota2((BLOCK, 3 * BLOCK), 0)
    col = _iota2((BLOCK, 3 * BLOCK), 1)
    rel = col - BLOCK - row
    valid = (jnp.abs(rel) <= WINDOW) & ((col >= BLOCK) | (i > 0)) & ((col < 2 * BLOCK) | (i < nb - 1))
    bias = jnp.where(valid, 0.0, NEG_BIG)

    scale = HEAD_DIM ** -0.5
    q_per_kv = ATTN_HEADS // ATTN_KV_HEADS
    for p in range(ATTN_HEADS // 2):
        qp = _headnorm_rope(q_ref[0, :, p * LANES:(p + 1) * LANES], qw_ref[...], cc_ref[0], sc_ref[0])
        qp = (qp * scale).astype(BF16)
        out = None
        for e in range(2):
            h = 2 * p + e
            g = h // q_per_kv
            s = _dot_nt(qp, place(kb, kb_sw, g, e)) + bias
            sink = sink_ref[h]
            m = jnp.maximum(jnp.max(s, axis=-1, keepdims=True), sink)
            pr = jnp.exp(s - m)
            denom = jnp.sum(pr, axis=-1, keepdims=True) + jnp.exp(sink - m)
            o = _dot(pr.astype(BF16), place(vb, vb_sw, g, e)) * (1.0 / denom)
            out = o if out is None else out + o
        o_ref[0, :, p * LANES:(p + 1) * LANES] = out.astype(o_ref.dtype)


def _window_attn(q, k, v, cos_p, sin_p, qw2, kw2, sink):
    b, s, aw = q.shape
    nb = s // BLOCK
    kvw = k.shape[-1]
    cur = lambda w: pl.BlockSpec((1, BLOCK, w), lambda bi, i: (bi, i, 0))
    prv = lambda w: pl.BlockSpec((1, BLOCK, w), lambda bi, i: (bi, jnp.maximum(i - 1, 0), 0))
    nxt = lambda w: pl.BlockSpec((1, BLOCK, w), lambda bi, i: (bi, jnp.minimum(i + 1, nb - 1), 0))
    return pl.pallas_call(
        _attn_kernel,
        grid=(b, nb),
        in_specs=[cur(aw), prv(kvw), cur(kvw), nxt(kvw), prv(kvw), cur(kvw), nxt(kvw),
                  prv(LANES), cur(LANES), nxt(LANES), prv(LANES), cur(LANES), nxt(LANES),
                  _const_spec((1, LANES)), _const_spec((1, LANES)),
                  pl.BlockSpec(memory_space=pltpu.SMEM)],
        out_specs=cur(aw),
        out_shape=jax.ShapeDtypeStruct((b, s, aw), F32),
        compiler_params=pltpu.CompilerParams(dimension_semantics=("arbitrary", "arbitrary")),
        name="window_attn",
    )(q, k, k, k, v, v, v, cos_p, cos_p, cos_p, sin_p, sin_p, sin_p, qw2, kw2, sink)


def _pad_inproj(w_in):
    d = w_in.shape[0]
    ssd_w = SSD_HEADS * SSD_HEAD_DIM
    conv_dim = ssd_w + 2 * SSD_GROUPS * D_STATE
    n_dt = 2 * SSD_HEADS
    s_xbc = ssd_w + conv_dim
    s_dt = s_xbc + n_dt
    dt_cols = jnp.pad(w_in[:, s_xbc:s_dt], ((0, 0), (0, LANES - n_dt)))
    return jnp.concatenate([w_in[:, :s_xbc], w_in[:, s_dt:], dt_cols], axis=1).astype(BF16)


def _pad_lanes(v, width=LANES):
    v = v.reshape(1, -1)
    return jnp.pad(v, ((0, 0), (0, width - v.shape[1])))


def kernel(x, c, positions, w_ada, b_ada, norm_ffn1, ffn1_wg, ffn1_wu, ffn1_wd, norm_mix, w_in, conv_w,
           conv_b, dt_bias, a_log, d_skip, ssd_norm_w, q_norm_w, k_norm_w, sink_logit, w_out, norm_ffn2,
           ffn2_wg, ffn2_wu, ffn2_wd, *, tm=512, ff_chunk=256):
    depth = w_ada.shape[0]
    b, s, d = x.shape
    h = x.astype(F32)
    c_pad = jnp.pad(c.astype(F32), ((0, SUBLANES - b % SUBLANES if b % SUBLANES else 0), (0, 0)))
    cos_p, sin_p = _rope_table(positions)
    for l in range(depth):
        mod = _adaln_mod(c_pad, w_ada[l], b_ada[l])[:b].reshape(b, N_MOD, d)
        h1, z, xbc, q, k, v, dt = _ffn1_inproj(
            h, mod, norm_ffn1[l].reshape(1, d), norm_mix[l].reshape(1, d),
            ffn1_wg[l].astype(BF16), ffn1_wu[l].astype(BF16), ffn1_wd[l].astype(BF16),
            _pad_inproj(w_in[l]), tm=tm, ff_chunk=ff_chunk)
        conv_w8 = jnp.pad(conv_w[l], ((0, SUBLANES - CONV_K), (0, 0)))
        y_ssd = _ssd(xbc, dt, z, conv_w8, conv_b[l].reshape(1, -1),
                     _pad_lanes(dt_bias[l]), _pad_lanes(a_log[l]),
                     jnp.repeat(d_skip[l], SSD_HEAD_DIM).reshape(1, -1), ssd_norm_w[l].reshape(1, -1))
        y_attn = _window_attn(q, k, v, cos_p, sin_p,
                              jnp.tile(q_norm_w[l], 2).reshape(1, LANES),
                              jnp.tile(k_norm_w[l], 2).reshape(1, LANES), sink_logit[l])
        h = _outproj_ffn2(h1, y_ssd, y_attn, mod, norm_ffn2[l].reshape(1, d),
                          w_out[l].astype(BF16), ffn2_wg[l].astype(BF16), ffn2_wu[l].astype(BF16),
                          ffn2_wd[l].astype(BF16), tm=tm, ff_chunk=ff_chunk)
    return h.astype(x.dtype)
```

```python
import functools

import jax
import jax.numpy as jnp
import numpy as np
from jax import lax
from jax.experimental import pallas as pl
from jax.experimental.pallas import tpu as pltpu

F32 = jnp.float32
BF16 = jnp.bfloat16

SSD_HEAD_DIM = 64
SSD_HEADS = 8
SSD_GROUPS = 2
D_STATE = 128
CONV_K = 5
CHUNK = 128
HEAD_DIM = 64
ATTN_HEADS = 8
ATTN_KV_HEADS = 2
WINDOW = 128
BLOCK = 128
ROPE_DIMS = 16
ROPE_THETA = 500000.0
N_MOD = 9
EPS = 1e-6

SSD_WIDTH = SSD_HEADS * SSD_HEAD_DIM
BC_WIDTH = 2 * SSD_GROUPS * D_STATE
CONV_DIM = SSD_WIDTH + BC_WIDTH

LANES = 128
SUBLANES = 8
VMEM_LIMIT_BYTES = 56 * 1024 * 1024

FFN_ROWS = 512
FFN_CHUNK = 256
SEQ_ROWS = 512
CONV_HALO = 64

NEG_BIG = -1e30


def _dot(a, b):
    return jnp.dot(a, b, preferred_element_type=F32)


def _dot_nt(a, b):
    return lax.dot_general(a, b, (((1,), (1,)), ((), ())), preferred_element_type=F32)


def _split_dot(x, m_bf16, terms=3, lhs=True):
    acc = None
    r = x
    for t in range(terms):
        h = r.astype(BF16)
        d = _dot(h, m_bf16) if lhs else _dot(m_bf16, h)
        acc = d if acc is None else acc + d
        if t + 1 < terms:
            r = r - h.astype(F32)
    return acc


def _silu(x):
    return x * jax.nn.sigmoid(x)


def _iota2(shape, dim):
    return lax.broadcasted_iota(jnp.int32, shape, dim)


def _const_spec(shape):
    nd = len(shape)
    return pl.BlockSpec(shape, lambda *_: (0,) * nd, pipeline_mode=pl.Buffered(1))


def _mod_kernel(c_ref, w_ref, b_ref, o_ref):
    cs = _silu(c_ref[...])
    o_ref[...] = _dot(cs, w_ref[...]) + b_ref[...]


def _adaln_mod(c_pad, w_ada, b_ada):
    rows, d = c_pad.shape
    n = w_ada.shape[1]
    bn = d
    return pl.pallas_call(
        _mod_kernel,
        grid=(n // bn,),
        in_specs=[pl.BlockSpec((rows, d), lambda j: (0, 0)),
                  pl.BlockSpec((d, bn), lambda j: (0, j)),
                  pl.BlockSpec((1, bn), lambda j: (0, j))],
        out_specs=pl.BlockSpec((rows, bn), lambda j: (0, j)),
        out_shape=jax.ShapeDtypeStruct((rows, n), F32),
        name="adaln_mod",
    )(c_pad, w_ada, b_ada.reshape(1, n))


def _rope_kernel(pos_ref, inv_ref, o_ref):
    half = ROPE_DIMS // 2
    s = pos_ref.shape[-1]
    p = pos_ref[0].astype(F32)
    ang = jnp.tile(inv_ref[...], (1, s // LANES)) * p
    o_ref[0, 0:half, :] = jnp.cos(ang)
    o_ref[0, half:, :] = jnp.sin(ang)


def _rope_table(positions):
    half = ROPE_DIMS // 2
    inv = ROPE_THETA ** (-jnp.arange(half, dtype=F32) * 2.0 / ROPE_DIMS)
    b, s = positions.shape
    return pl.pallas_call(
        _rope_kernel,
        grid=(b,),
        in_specs=[pl.BlockSpec((1, 1, s), lambda bi: (bi, 0, 0)), _const_spec((half, LANES))],
        out_specs=pl.BlockSpec((1, 2 * half, s), lambda bi: (bi, 0, 0)),
        out_shape=jax.ShapeDtypeStruct((b, 2 * half, s), F32),
        name="rope_table",
    )(positions.reshape(b, 1, s), jnp.broadcast_to(inv[:, None], (half, LANES)))


def _ada_norm(x, gain_scale, shift):
    ms = jnp.mean(x * x, axis=-1, keepdims=True)
    return x * lax.rsqrt(ms + EPS) * gain_scale + shift


def _swiglu(ub, wg_ref, wu_ref, wd_ref):
    d_ff = wg_ref.shape[1]
    acc = None
    for c0 in range(0, d_ff, FFN_CHUNK):
        c1 = min(c0 + FFN_CHUNK, d_ff)
        g = _dot(ub, wg_ref[:, c0:c1])
        up = _dot(ub, wu_ref[:, c0:c1])
        a = (_silu(g) * up).astype(BF16)
        d = _dot(a, wd_ref[c0:c1, :])
        acc = d if acc is None else acc + d
    return acc


_INPROJ_LAYOUT = (("z", SSD_WIDTH, F32), ("xbc", CONV_DIM, BF16), ("q", ATTN_HEADS * HEAD_DIM, F32),
                  ("kv", 2 * ATTN_KV_HEADS * HEAD_DIM, F32), ("dt", LANES, F32))


def _ffn1_kernel(x_ref, mod_ref, n1_ref, n2_ref, wg_ref, wu_ref, wd_ref, win_ref, h_ref, *proj_refs):
    x = x_ref[0]
    mod = mod_ref[0]
    u = _ada_norm(x, n1_ref[...] * (1.0 + mod[1:2]), mod[0:1]).astype(BF16)
    ff = _swiglu(u, wg_ref, wu_ref, wd_ref)
    h = x + (0.5 * (1.0 + mod[2:3])) * ff
    h_ref[0] = h
    u2 = _ada_norm(h, n2_ref[...] * (1.0 + mod[4:5]), mod[3:4]).astype(BF16)
    c0 = 0
    for ref, (_, w, _) in zip(proj_refs, _INPROJ_LAYOUT):
        ref[0] = _dot(u2, win_ref[:, c0:c0 + w]).astype(ref.dtype)
        c0 += w


def _ffn1_inproj(x, mod, norm1, norm2, wg, wu, wd, w_in_p):
    b, s, d = x.shape
    d_ff = wg.shape[1]
    tok = lambda w: pl.BlockSpec((1, FFN_ROWS, w), lambda bi, i: (bi, i, 0))
    out_shapes = [jax.ShapeDtypeStruct((b, s, d), F32)]
    out_shapes += [jax.ShapeDtypeStruct((b, s, w), dt) for _, w, dt in _INPROJ_LAYOUT]
    out_specs = [tok(d)] + [tok(w) for _, w, _ in _INPROJ_LAYOUT]
    return pl.pallas_call(
        _ffn1_kernel,
        grid=(b, s // FFN_ROWS),
        in_specs=[tok(d),
                  pl.BlockSpec((1, N_MOD, d), lambda bi, i: (bi, 0, 0)),
                  _const_spec((1, d)), _const_spec((1, d)),
                  _const_spec((d, d_ff)), _const_spec((d, d_ff)), _const_spec((d_ff, d)),
                  _const_spec(w_in_p.shape)],
        out_specs=out_specs,
        out_shape=out_shapes,
        compiler_params=pltpu.CompilerParams(dimension_semantics=("arbitrary", "arbitrary"),
                                             vmem_limit_bytes=VMEM_LIMIT_BYTES),
        name="ffn1_inproj",
    )(x, mod, norm1, norm2, wg, wu, wd, w_in_p)


def _ffn2_kernel(h_ref, ys_ref, ya_ref, mod_ref, n3_ref, wo_ref, wg_ref, wu_ref, wd_ref, o_ref):
    h1 = h_ref[0]
    mod = mod_ref[0]
    half = ys_ref.shape[-1]
    mix = _dot(ys_ref[0], wo_ref[0:half, :]) + _dot(ya_ref[0], wo_ref[half:, :])
    h2 = h1 + (1.0 + mod[5:6]) * mix
    u = _ada_norm(h2, n3_ref[...] * (1.0 + mod[7:8]), mod[6:7]).astype(BF16)
    ff = _swiglu(u, wg_ref, wu_ref, wd_ref)
    o_ref[0] = h2 + (0.5 * (1.0 + mod[8:9])) * ff


def _outproj_ffn2(h1, y_ssd, y_attn, mod, norm3, w_out, wg, wu, wd):
    b, s, d = h1.shape
    d_ff = wg.shape[1]
    hw = y_ssd.shape[-1]
    tok = lambda w: pl.BlockSpec((1, FFN_ROWS, w), lambda bi, i: (bi, i, 0))
    return pl.pallas_call(
        _ffn2_kernel,
        grid=(b, s // FFN_ROWS),
        in_specs=[tok(d), tok(hw), tok(hw),
                  pl.BlockSpec((1, N_MOD, d), lambda bi, i: (bi, 0, 0)),
                  _const_spec((1, d)),
                  _const_spec(w_out.shape),
                  _const_spec((d, d_ff)), _const_spec((d, d_ff)), _const_spec((d_ff, d))],
        out_specs=tok(d),
        out_shape=jax.ShapeDtypeStruct((b, s, d), F32),
        compiler_params=pltpu.CompilerParams(dimension_semantics=("arbitrary", "arbitrary"),
                                             vmem_limit_bytes=VMEM_LIMIT_BYTES),
        name="outproj_ffn2",
    )(h1, y_ssd, y_attn, mod, norm3, w_out, wg, wu, wd)


def _ssd_constants():
    r = np.arange(CHUNK)[:, None]
    c = np.arange(CHUNK)[None, :]
    tri_lo = (c <= r)
    tri_up = (c >= r)
    lane = np.arange(LANES)[:, None]
    col = np.arange(SSD_WIDTH)[None, :] // SSD_HEAD_DIM
    exp_f = (col == lane)
    exp_b = (col + SSD_HEADS == lane)
    taps = [k - CONV_K // 2 for k in range(CONV_K) if k != CONV_K // 2]
    rows = np.arange(len(taps) * CHUNK)[:, None]
    off = np.asarray(taps)[rows // CHUNK]
    wcol = np.arange(2 * CHUNK)[None, :]
    shift = (wcol == CONV_HALO + rows % CHUNK + off)
    return [jnp.asarray(m, dtype=BF16) for m in (tri_lo, tri_up, exp_f, exp_b, shift)]


def _dt_and_da(dt_raw, dtb_ref, alog_ref):
    lane = _iota2(dt_raw.shape, 1)
    dt = jnp.where(lane < 2 * SSD_HEADS, jax.nn.softplus(dt_raw + dtb_ref[...]), 0.0)
    a = -jnp.exp(alog_ref[...])
    return dt, dt * a


def _ssd_chunk(direction, cs, csx, cm, bm, xdt, h_ref):
    gw = SSD_WIDTH // SSD_GROUPS
    lane0 = 0 if direction == "fwd" else SSD_HEADS
    tot_row = CHUNK - 1 if direction == "fwd" else 0
    tot = csx[tot_row:tot_row + 1, :]
    e_in = jnp.exp(csx)
    xdec = (xdt * jnp.exp(tot - csx)).astype(BF16)
    cdec = jnp.exp(tot)
    cst = cs.T
    row = _iota2((CHUNK, CHUNK), 0)
    col = _iota2((CHUNK, CHUNK), 1)
    keep = (col <= row) if direction == "fwd" else (col > row)
    half = _iota2((CHUNK, LANES), 1) // SSD_HEAD_DIM
    xdt_b = xdt.astype(BF16)
    ys = []
    for g in range(SSD_GROUPS):
        cg = cm[:, g * D_STATE:(g + 1) * D_STATE]
        bg = bm[:, g * D_STATE:(g + 1) * D_STATE]
        cb = _dot_nt(cg, bg)
        hg = h_ref[:, g * gw:(g + 1) * gw]
        y_off = _dot(cg, hg.astype(BF16)) * e_in[:, g * gw:(g + 1) * gw]
        pairs = []
        for pp in range(gw // LANES):
            p = g * (gw // LANES) + pp
            xp = xdt_b[:, p * LANES:(p + 1) * LANES]
            yp = None
            for e in range(2):
                hcol = lane0 + 2 * p + e
                seg = cs[:, hcol:hcol + 1] - cst[hcol:hcol + 1, :]
                m = (cb * jnp.exp(jnp.where(keep, seg, NEG_BIG))).astype(BF16)
                d = _dot(m, jnp.where(half == e, xp, jnp.zeros_like(xp)))
                yp = d if yp is None else yp + d
            pairs.append(yp)
        ys.append(jnp.concatenate(pairs, axis=1) + y_off)
        bgt = bg.astype(F32).T.astype(BF16)
        st = _dot(bgt, xdec[:, g * gw:(g + 1) * gw])
        h_ref[:, g * gw:(g + 1) * gw] = hg * cdec[:, g * gw:(g + 1) * gw] + st
    return jnp.concatenate(ys, axis=1)


def _ssd_fwd_kernel(xm_ref, xp_ref, xn_ref, dt_ref, cw_ref, cbias_ref, dtb_ref, alog_ref, dskip_ref,
                    tri_ref, ef_ref, eb_ref, shift_ref, bc_ref, xdtb_ref, yp_ref, h_ref):
    i = pl.program_id(1)
    n_steps = pl.num_programs(1)

    @pl.when(i == 0)
    def _():
        h_ref[...] = jnp.zeros_like(h_ref)

    prev = xp_ref[0]
    nxt = xn_ref[0]
    prev = jnp.where(i > 0, prev, jnp.zeros_like(prev))
    nxt = jnp.where(i < n_steps - 1, nxt, jnp.zeros_like(nxt))
    xpad = jnp.concatenate([prev, xm_ref[0], nxt], axis=0)
    mid = CONV_K // 2
    for j in range(xm_ref.shape[1] // CHUNK):
        win = xpad[j * CHUNK:(j + 2) * CHUNK]
        sh = _dot(shift_ref[...], win)
        acc = cbias_ref[...] + win[CONV_HALO:CONV_HALO + CHUNK].astype(F32) * cw_ref[mid:mid + 1, :]
        for a, k in enumerate([k for k in range(CONV_K) if k != mid]):
            acc = acc + sh[a * CHUNK:(a + 1) * CHUNK] * cw_ref[k:k + 1, :]
        act = _silu(acc)
        xs = act[:, :SSD_WIDTH]
        bc = act[:, SSD_WIDTH:].astype(BF16)
        bm = bc[:, :SSD_GROUPS * D_STATE]
        cm = bc[:, SSD_GROUPS * D_STATE:]

        rows = pl.ds(j * CHUNK, CHUNK)
        dt, da = _dt_and_da(dt_ref[0, rows, :], dtb_ref, alog_ref)
        cs = _split_dot(da, tri_ref[...], lhs=False)
        csx = _split_dot(cs, ef_ref[...])
        xdt = xs * _split_dot(dt, ef_ref[...])
        xdtb_ref[0, rows, :] = (xs * _split_dot(dt, eb_ref[...])).astype(BF16)
        bc_ref[0, rows, :] = bc
        y = _ssd_chunk("fwd", cs, csx, cm, bm, xdt, h_ref)
        yp_ref[0, rows, :] = y + xs * dskip_ref[...]


def _ssd_bwd_kernel(bc_ref, xdtb_ref, yp_ref, dt_ref, z_ref, dtb_ref, alog_ref, nw_ref, tri_ref, eb_ref,
                    o_ref, h_ref):
    @pl.when(pl.program_id(1) == 0)
    def _():
        h_ref[...] = jnp.zeros_like(h_ref)

    for j in reversed(range(bc_ref.shape[1] // CHUNK)):
        rows = pl.ds(j * CHUNK, CHUNK)
        bc = bc_ref[0, rows, :]
        bm = bc[:, :SSD_GROUPS * D_STATE]
        cm = bc[:, SSD_GROUPS * D_STATE:]
        _, da = _dt_and_da(dt_ref[0, rows, :], dtb_ref, alog_ref)
        cs = _split_dot(da, tri_ref[...], lhs=False)
        csx = _split_dot(cs, eb_ref[...])
        y = _ssd_chunk("bwd", cs, csx, cm, bm, xdtb_ref[0, rows, :].astype(F32), h_ref)
        y = (y + yp_ref[0, rows, :]) * _silu(z_ref[0, rows, :])
        ms = jnp.mean(y * y, axis=-1, keepdims=True)
        o_ref[0, rows, :] = (y * lax.rsqrt(ms + EPS) * nw_ref[...]).astype(o_ref.dtype)


def _ssd(xbc, dt, z, conv_w8, conv_b, dtb, alog, dskip, norm_w):
    b, s, conv_dim = xbc.shape
    n_steps = s // SEQ_ROWS
    per_halo = SEQ_ROWS // CONV_HALO
    n_halo = s // CONV_HALO
    tri_lo, tri_up, exp_f, exp_b, shift = _ssd_constants()
    fwd = lambda w: pl.BlockSpec((1, SEQ_ROWS, w), lambda bi, i: (bi, i, 0))
    rev = lambda w: pl.BlockSpec((1, SEQ_ROWS, w), lambda bi, i: (bi, n_steps - 1 - i, 0))
    params = pltpu.CompilerParams(dimension_semantics=("arbitrary", "arbitrary"))
    bc, xdtb, ypart = pl.pallas_call(
        _ssd_fwd_kernel,
        grid=(b, n_steps),
        in_specs=[fwd(conv_dim),
                  pl.BlockSpec((1, CONV_HALO, conv_dim), lambda bi, i: (bi, jnp.maximum(i * per_halo - 1, 0), 0)),
                  pl.BlockSpec((1, CONV_HALO, conv_dim),
                               lambda bi, i: (bi, jnp.minimum((i + 1) * per_halo, n_halo - 1), 0)),
                  fwd(LANES),
                  _const_spec(conv_w8.shape), _const_spec((1, conv_dim)),
                  _const_spec((1, LANES)), _const_spec((1, LANES)), _const_spec((1, SSD_WIDTH)),
                  _const_spec(tri_lo.shape), _const_spec(exp_f.shape), _const_spec(exp_b.shape),
                  _const_spec(shift.shape)],
        out_specs=[fwd(BC_WIDTH), fwd(SSD_WIDTH), fwd(SSD_WIDTH)],
        out_shape=[jax.ShapeDtypeStruct((b, s, BC_WIDTH), BF16),
                   jax.ShapeDtypeStruct((b, s, SSD_WIDTH), BF16),
                   jax.ShapeDtypeStruct((b, s, SSD_WIDTH), F32)],
        scratch_shapes=[pltpu.VMEM((D_STATE, SSD_WIDTH), F32)],
        compiler_params=params,
        name="ssd_fwd",
    )(xbc, xbc, xbc, dt, conv_w8, conv_b, dtb, alog, dskip, tri_lo, exp_f, exp_b, shift)
    return pl.pallas_call(
        _ssd_bwd_kernel,
        grid=(b, n_steps),
        in_specs=[rev(BC_WIDTH), rev(SSD_WIDTH), rev(SSD_WIDTH), rev(LANES), rev(SSD_WIDTH),
                  _const_spec((1, LANES)), _const_spec((1, LANES)), _const_spec((1, SSD_WIDTH)),
                  _const_spec(tri_up.shape), _const_spec(exp_b.shape)],
        out_specs=rev(SSD_WIDTH),
        out_shape=jax.ShapeDtypeStruct((b, s, SSD_WIDTH), BF16),
        scratch_shapes=[pltpu.VMEM((D_STATE, SSD_WIDTH), F32)],
        compiler_params=params,
        name="ssd_bwd",
    )(bc, xdtb, ypart, dt, z, dtb, alog, norm_w, tri_up, exp_b)


def _head_prep_t(t, w_b, cos, sin):
    ms = jnp.sum(t * t, axis=0, keepdims=True) * (1.0 / HEAD_DIM)
    tn = t * lax.rsqrt(ms + EPS) * w_b
    half = ROPE_DIMS // 2
    t1 = tn[0:half]
    t2 = tn[half:ROPE_DIMS]
    return jnp.concatenate([t1 * cos - t2 * sin, t2 * cos + t1 * sin, tn[ROPE_DIMS:]], axis=0)


def _attn_kernel(q_ref, kvp_ref, kvc_ref, kvn_ref, csp_ref, csc_ref, csn_ref, qw_ref, kw_ref, sink_ref, o_ref):
    i = pl.program_id(1)
    n_steps = pl.num_programs(1)
    tq = q_ref.shape[1]
    nsub = tq // BLOCK
    nw = tq + 2 * BLOCK
    kvw = ATTN_KV_HEADS * HEAD_DIM
    half = ROPE_DIMS // 2
    q_per_kv = ATTN_HEADS // ATTN_KV_HEADS

    kv = jnp.concatenate([kvp_ref[0], kvc_ref[0], kvn_ref[0]], axis=0)
    cs = jnp.concatenate([csp_ref[0], csc_ref[0], csn_ref[0]], axis=1)
    k_t = kv[:, :kvw].T
    v_t = kv[:, kvw:].T.astype(BF16)
    kw_b = jnp.tile(kw_ref[...], (1, nw // LANES))
    k_prep = jnp.concatenate(
        [_head_prep_t(k_t[g * HEAD_DIM:(g + 1) * HEAD_DIM], kw_b, cs[0:half], cs[half:])
         for g in range(ATTN_KV_HEADS)], axis=0)
    keys = k_prep.T.astype(BF16)

    q_t = q_ref[0].T
    qw_b = jnp.tile(qw_ref[...] * (HEAD_DIM ** -0.5), (1, tq // LANES))
    cs_q = csc_ref[0]
    zeros = jnp.zeros((HEAD_DIM, tq), BF16)
    q_heads = []
    for h in range(ATTN_HEADS):
        qh = _head_prep_t(q_t[h * HEAD_DIM:(h + 1) * HEAD_DIM], qw_b, cs_q[0:half], cs_q[half:]).astype(BF16)
        q_heads.append(jnp.concatenate([qh, zeros] if h // q_per_kv == 0 else [zeros, qh], axis=0))

    kr = _iota2((BLOCK, BLOCK), 0)
    qc = _iota2((BLOCK, BLOCK), 1)
    bias_prev = jnp.where(kr >= qc, 0.0, NEG_BIG)
    bias_next = jnp.where(kr <= qc, 0.0, NEG_BIG)
    sink_row = jnp.concatenate([jnp.full((1, BLOCK), sink_ref[h], F32) for h in range(ATTN_HEADS)], axis=1)
    gq = q_per_kv * BLOCK
    for j in range(nsub):
        bp = bias_prev if j > 0 else jnp.where(i > 0, bias_prev, NEG_BIG)
        bn = bias_next if j < nsub - 1 else jnp.where(i < n_steps - 1, bias_next, NEG_BIG)
        lo, hi = j * BLOCK, (j + 3) * BLOCK
        q_all = jnp.concatenate([qh[:, lo:lo + BLOCK] for qh in q_heads], axis=1)
        s = _dot(keys[lo:hi], q_all)
        s0 = s[0:BLOCK] + jnp.tile(bp, (1, ATTN_HEADS))
        s1 = s[BLOCK:2 * BLOCK]
        s2 = s[2 * BLOCK:] + jnp.tile(bn, (1, ATTN_HEADS))
        m = jnp.maximum(jnp.max(jnp.maximum(jnp.maximum(s0, s1), s2), axis=0, keepdims=True), sink_row)
        p0 = jnp.exp(s0 - m)
        p1 = jnp.exp(s1 - m)
        p2 = jnp.exp(s2 - m)
        denom = jnp.sum(p0 + p1 + p2, axis=0, keepdims=True) + jnp.exp(sink_row - m)
        p = jnp.concatenate([p0, p1, p2], axis=0).astype(BF16)
        inv = 1.0 / denom
        outs = []
        for g in range(ATTN_KV_HEADS):
            o = _dot(v_t[g * HEAD_DIM:(g + 1) * HEAD_DIM, lo:hi], p[:, g * gq:(g + 1) * gq])
            o = o * inv[:, g * gq:(g + 1) * gq]
            outs += [o[:, k * BLOCK:(k + 1) * BLOCK] for k in range(q_per_kv)]
        o_ref[0, lo:lo + BLOCK, :] = jnp.concatenate(outs, axis=0).T.astype(o_ref.dtype)


def _window_attn(q, kv, cs_tab, qw_b, kw_b, sink):
    b, s, aw = q.shape
    nb = s // BLOCK
    nsub = SEQ_ROWS // BLOCK
    kvw2 = kv.shape[-1]
    rows = cs_tab.shape[1]
    prev_blk = lambda i: jnp.maximum(i * nsub - 1, 0)
    next_blk = lambda i: jnp.minimum((i + 1) * nsub, nb - 1)
    return pl.pallas_call(
        _attn_kernel,
        grid=(b, s // SEQ_ROWS),
        in_specs=[pl.BlockSpec((1, SEQ_ROWS, aw), lambda bi, i: (bi, i, 0)),
                  pl.BlockSpec((1, BLOCK, kvw2), lambda bi, i: (bi, prev_blk(i), 0)),
                  pl.BlockSpec((1, SEQ_ROWS, kvw2), lambda bi, i: (bi, i, 0)),
                  pl.BlockSpec((1, BLOCK, kvw2), lambda bi, i: (bi, next_blk(i), 0)),
                  pl.BlockSpec((1, rows, BLOCK), lambda bi, i: (bi, 0, prev_blk(i))),
                  pl.BlockSpec((1, rows, SEQ_ROWS), lambda bi, i: (bi, 0, i)),
                  pl.BlockSpec((1, rows, BLOCK), lambda bi, i: (bi, 0, next_blk(i))),
                  _const_spec((HEAD_DIM, LANES)), _const_spec((HEAD_DIM, LANES)),
                  pl.BlockSpec(memory_space=pltpu.SMEM)],
        out_specs=pl.BlockSpec((1, SEQ_ROWS, aw), lambda bi, i: (bi, i, 0)),
        out_shape=jax.ShapeDtypeStruct((b, s, aw), BF16),
        compiler_params=pltpu.CompilerParams(dimension_semantics=("arbitrary", "arbitrary")),
        name="window_attn",
    )(q, kv, kv, kv, cs_tab, cs_tab, cs_tab, qw_b, kw_b, sink)


def _pad_inproj(w_in):
    n_dt = 2 * SSD_HEADS
    s_xbc = SSD_WIDTH + CONV_DIM
    s_dt = s_xbc + n_dt
    dt_cols = jnp.pad(w_in[:, s_xbc:s_dt], ((0, 0), (0, LANES - n_dt)))
    return jnp.concatenate([w_in[:, :s_xbc], w_in[:, s_dt:], dt_cols], axis=1).astype(BF16)


def _pad_lanes(v, width=LANES):
    v = v.reshape(1, -1)
    return jnp.pad(v, ((0, 0), (0, width - v.shape[1])))


def _lane_bcast(v):
    return jnp.broadcast_to(v[:, None], (v.shape[0], LANES))


def kernel(x, c, positions, w_ada, b_ada, norm_ffn1, ffn1_wg, ffn1_wu, ffn1_wd, norm_mix, w_in, conv_w,
           conv_b, dt_bias, a_log, d_skip, ssd_norm_w, q_norm_w, k_norm_w, sink_logit, w_out, norm_ffn2,
           ffn2_wg, ffn2_wu, ffn2_wd):
    depth = w_ada.shape[0]
    b, s, d = x.shape
    h = x.astype(F32)
    c_pad = jnp.pad(c.astype(F32), ((0, -b % SUBLANES), (0, 0)))
    cs_tab = _rope_table(positions)
    for l in range(depth):
        mod = _adaln_mod(c_pad, w_ada[l], b_ada[l])[:b].reshape(b, N_MOD, d)
        h1, z, xbc, q, kv, dt = _ffn1_inproj(
            h, mod, norm_ffn1[l].reshape(1, d), norm_mix[l].reshape(1, d),
            ffn1_wg[l].astype(BF16), ffn1_wu[l].astype(BF16), ffn1_wd[l].astype(BF16), _pad_inproj(w_in[l]))
        conv_w8 = jnp.pad(conv_w[l], ((0, SUBLANES - CONV_K), (0, 0)))
        y_ssd = _ssd(xbc, dt, z, conv_w8, conv_b[l].reshape(1, -1),
                     _pad_lanes(dt_bias[l]), _pad_lanes(a_log[l]),
                     jnp.repeat(d_skip[l], SSD_HEAD_DIM).reshape(1, -1), ssd_norm_w[l].reshape(1, -1))
        y_attn = _window_attn(q, kv, cs_tab, _lane_bcast(q_norm_w[l]), _lane_bcast(k_norm_w[l]), sink_logit[l])
        h = _outproj_ffn2(h1, y_ssd, y_attn, mod, norm_ffn2[l].reshape(1, d),
                          w_out[l].astype(BF16), ffn2_wg[l].astype(BF16), ffn2_wu[l].astype(BF16),
                          ffn2_wd[l].astype(BF16))
    return h.astype(x.dtype)
```

```python
import functools

import jax
import jax.numpy as jnp
import numpy as np
from jax import lax
from jax.experimental import pallas as pl
from jax.experimental.pallas import tpu as pltpu

F32 = jnp.float32
BF16 = jnp.bfloat16

SSD_HEAD_DIM = 64
SSD_HEADS = 8
SSD_GROUPS = 2
D_STATE = 128
CONV_K = 5
CHUNK = 128
HEAD_DIM = 64
ATTN_HEADS = 8
ATTN_KV_HEADS = 2
WINDOW = 128
BLOCK = 128
ROPE_DIMS = 16
ROPE_THETA = 500000.0
N_MOD = 9
EPS = 1e-6

SSD_WIDTH = SSD_HEADS * SSD_HEAD_DIM
BC_WIDTH = 2 * SSD_GROUPS * D_STATE
CONV_DIM = SSD_WIDTH + BC_WIDTH

LANES = 128
SUBLANES = 8
VMEM_LIMIT_BYTES = 56 * 1024 * 1024

FFN_ROWS = 512
FFN_CHUNK = 256
SEQ_ROWS = 512
CONV_HALO = 64

NEG_BIG = -1e30


def _dot(a, b):
    return jnp.dot(a, b, preferred_element_type=F32)


def _dot_nt(a, b):
    return lax.dot_general(a, b, (((1,), (1,)), ((), ())), preferred_element_type=F32)


def _split_dot(x, m_bf16, terms=3, lhs=True):
    acc = None
    r = x
    for t in range(terms):
        h = r.astype(BF16)
        d = _dot(h, m_bf16) if lhs else _dot(m_bf16, h)
        acc = d if acc is None else acc + d
        if t + 1 < terms:
            r = r - h.astype(F32)
    return acc


def _silu(x):
    return x * jax.nn.sigmoid(x)


def _iota2(shape, dim):
    return lax.broadcasted_iota(jnp.int32, shape, dim)


def _const_spec(shape):
    nd = len(shape)
    return pl.BlockSpec(shape, lambda *_: (0,) * nd, pipeline_mode=pl.Buffered(1))


def _mod_kernel(c_ref, w_ref, b_ref, o_ref):
    cs = _silu(c_ref[...])
    o_ref[...] = _dot(cs, w_ref[...]) + b_ref[...]


def _adaln_mod(c_pad, w_ada, b_ada):
    rows, d = c_pad.shape
    n = w_ada.shape[1]
    bn = d
    return pl.pallas_call(
        _mod_kernel,
        grid=(n // bn,),
        in_specs=[pl.BlockSpec((rows, d), lambda j: (0, 0)),
                  pl.BlockSpec((d, bn), lambda j: (0, j)),
                  pl.BlockSpec((1, bn), lambda j: (0, j))],
        out_specs=pl.BlockSpec((rows, bn), lambda j: (0, j)),
        out_shape=jax.ShapeDtypeStruct((rows, n), F32),
        name="adaln_mod",
    )(c_pad, w_ada, b_ada.reshape(1, n))


def _rope_kernel(pos_ref, inv_ref, o_ref):
    half = ROPE_DIMS // 2
    s = pos_ref.shape[-1]
    p = pos_ref[0].astype(F32)
    ang = jnp.tile(inv_ref[...], (1, s // LANES)) * p
    o_ref[0, 0:half, :] = jnp.cos(ang)
    o_ref[0, half:, :] = jnp.sin(ang)


def _rope_table(positions):
    half = ROPE_DIMS // 2
    inv = ROPE_THETA ** (-jnp.arange(half, dtype=F32) * 2.0 / ROPE_DIMS)
    b, s = positions.shape
    return pl.pallas_call(
        _rope_kernel,
        grid=(b,),
        in_specs=[pl.BlockSpec((1, 1, s), lambda bi: (bi, 0, 0)), _const_spec((half, LANES))],
        out_specs=pl.BlockSpec((1, 2 * half, s), lambda bi: (bi, 0, 0)),
        out_shape=jax.ShapeDtypeStruct((b, 2 * half, s), F32),
        name="rope_table",
    )(positions.reshape(b, 1, s), jnp.broadcast_to(inv[:, None], (half, LANES)))


def _ada_norm(x, gain_scale, shift):
    ms = jnp.mean(x * x, axis=-1, keepdims=True)
    return x * lax.rsqrt(ms + EPS) * gain_scale + shift


def _swiglu(ub, wg_ref, wu_ref, wd_ref, side_work=()):
    d_ff = wg_ref.shape[1]
    acc = None
    for c0 in range(0, d_ff, FFN_CHUNK):
        c1 = min(c0 + FFN_CHUNK, d_ff)
        g = _dot(ub, wg_ref[:, c0:c1])
        up = _dot(ub, wu_ref[:, c0:c1])
        a = (_silu(g) * up).astype(BF16)
        d = _dot(a, wd_ref[c0:c1, :])
        acc = d if acc is None else acc + d
        for gen in side_work:
            next(gen, None)
    for gen in side_work:
        for _ in gen:
            pass
    return acc


_INPROJ_LAYOUT = (("z", SSD_WIDTH, F32), ("xbc", CONV_DIM, BF16), ("q", ATTN_HEADS * HEAD_DIM, F32),
                  ("kv", 2 * ATTN_KV_HEADS * HEAD_DIM, F32), ("dt", LANES, F32))


def _ffn1_kernel(x_ref, mod_ref, n1_ref, n2_ref, wg_ref, wu_ref, wd_ref, win_ref, h_ref, *proj_refs):
    x = x_ref[0]
    mod = mod_ref[0]
    u = _ada_norm(x, n1_ref[...] * (1.0 + mod[1:2]), mod[0:1]).astype(BF16)
    ff = _swiglu(u, wg_ref, wu_ref, wd_ref)
    h = x + (0.5 * (1.0 + mod[2:3])) * ff
    h_ref[0] = h
    u2 = _ada_norm(h, n2_ref[...] * (1.0 + mod[4:5]), mod[3:4]).astype(BF16)
    c0 = 0
    for ref, (_, w, _) in zip(proj_refs, _INPROJ_LAYOUT):
        ref[0] = _dot(u2, win_ref[:, c0:c0 + w]).astype(ref.dtype)
        c0 += w


def _ffn1_inproj(x, mod, norm1, norm2, wg, wu, wd, w_in_p):
    b, s, d = x.shape
    d_ff = wg.shape[1]
    tok = lambda w: pl.BlockSpec((1, FFN_ROWS, w), lambda bi, i: (bi, i, 0))
    out_shapes = [jax.ShapeDtypeStruct((b, s, d), F32)]
    out_shapes += [jax.ShapeDtypeStruct((b, s, w), dt) for _, w, dt in _INPROJ_LAYOUT]
    out_specs = [tok(d)] + [tok(w) for _, w, _ in _INPROJ_LAYOUT]
    return pl.pallas_call(
        _ffn1_kernel,
        grid=(b, s // FFN_ROWS),
        in_specs=[tok(d),
                  pl.BlockSpec((1, N_MOD, d), lambda bi, i: (bi, 0, 0)),
                  _const_spec((1, d)), _const_spec((1, d)),
                  _const_spec((d, d_ff)), _const_spec((d, d_ff)), _const_spec((d_ff, d)),
                  _const_spec(w_in_p.shape)],
        out_specs=out_specs,
        out_shape=out_shapes,
        compiler_params=pltpu.CompilerParams(dimension_semantics=("arbitrary", "arbitrary"),
                                             vmem_limit_bytes=VMEM_LIMIT_BYTES),
        name="ffn1_inproj",
    )(x, mod, norm1, norm2, wg, wu, wd, w_in_p)


def _ssd_constants():
    r = np.arange(CHUNK)[:, None]
    c = np.arange(CHUNK)[None, :]
    tri_lo = (c <= r)
    tri_up = (c >= r)
    lane = np.arange(LANES)[:, None]
    col = np.arange(SSD_WIDTH)[None, :] // SSD_HEAD_DIM
    exp_f = (col == lane)
    exp_b = (col + SSD_HEADS == lane)
    taps = [k - CONV_K // 2 for k in range(CONV_K) if k != CONV_K // 2]
    rows = np.arange(len(taps) * CHUNK)[:, None]
    off = np.asarray(taps)[rows // CHUNK]
    wcol = np.arange(2 * CHUNK)[None, :]
    shift = (wcol == CONV_HALO + rows % CHUNK + off)
    return [jnp.asarray(m, dtype=BF16) for m in (tri_lo, tri_up, exp_f, exp_b, shift)]


def _dt_and_da(dt_raw, dtb_ref, alog_ref):
    lane = _iota2(dt_raw.shape, 1)
    dt = jnp.where(lane < 2 * SSD_HEADS, jax.nn.softplus(dt_raw + dtb_ref[...]), 0.0)
    a = -jnp.exp(alog_ref[...])
    return dt, dt * a


def _ssd_chunk(direction, da, tri_ref, e_ref, cm, bm, xdt_fn, h_ref):
    gw = SSD_WIDTH // SSD_GROUPS
    lane0 = 0 if direction == "fwd" else SSD_HEADS
    tot_row = CHUNK - 1 if direction == "fwd" else 0
    cgs = [cm[:, g * D_STATE:(g + 1) * D_STATE] for g in range(SSD_GROUPS)]
    bgs = [bm[:, g * D_STATE:(g + 1) * D_STATE] for g in range(SSD_GROUPS)]
    hgs = [h_ref[:, g * gw:(g + 1) * gw] for g in range(SSD_GROUPS)]

    cs = _split_dot(da, tri_ref[...], lhs=False)
    cbs = [_dot_nt(cgs[g], bgs[g]) for g in range(SSD_GROUPS)]
    chs = [_dot(cgs[g], hgs[g].astype(BF16)) for g in range(SSD_GROUPS)]
    yield

    csx = _split_dot(cs, e_ref[...])
    yield

    xdt = xdt_fn()
    tot = csx[tot_row:tot_row + 1, :]
    e_in = jnp.exp(csx)
    xdec = (xdt * jnp.exp(tot - csx)).astype(BF16)
    cdec = jnp.exp(tot)
    cst = cs.T
    row = _iota2((CHUNK, CHUNK), 0)
    col = _iota2((CHUNK, CHUNK), 1)
    keep = (col <= row) if direction == "fwd" else (col > row)
    half = _iota2((CHUNK, LANES), 1) // SSD_HEAD_DIM
    xdt_b = xdt.astype(BF16)
    ys = []
    for g in range(SSD_GROUPS):
        pairs = []
        for pp in range(gw // LANES):
            p = g * (gw // LANES) + pp
            xp = xdt_b[:, p * LANES:(p + 1) * LANES]
            yp = None
            for e in range(2):
                hcol = lane0 + 2 * p + e
                seg = cs[:, hcol:hcol + 1] - cst[hcol:hcol + 1, :]
                m = (cbs[g] * jnp.exp(jnp.where(keep, seg, NEG_BIG))).astype(BF16)
                d = _dot(m, jnp.where(half == e, xp, jnp.zeros_like(xp)))
                yp = d if yp is None else yp + d
            pairs.append(yp)
        ys.append(jnp.concatenate(pairs, axis=1) + chs[g] * e_in[:, g * gw:(g + 1) * gw])
        bgt = bgs[g].astype(F32).T.astype(BF16)
        st = _dot(bgt, xdec[:, g * gw:(g + 1) * gw])
        h_ref[:, g * gw:(g + 1) * gw] = hgs[g] * cdec[:, g * gw:(g + 1) * gw] + st
    return jnp.concatenate(ys, axis=1)


def _ssd_fwd_tile(xm_ref, xp_ref, xn_ref, dt_ref, cw_ref, cbias_ref, dtb_ref, alog_ref, dskip_ref,
                  tri_ref, ef_ref, eb_ref, shift_ref, bc_ref, xdtb_ref, yp_ref, h_ref, has_prev, has_next):
    prev = xp_ref[0]
    nxt = xn_ref[0]
    prev = jnp.where(has_prev, prev, jnp.zeros_like(prev))
    nxt = jnp.where(has_next, nxt, jnp.zeros_like(nxt))
    xpad = jnp.concatenate([prev, xm_ref[0], nxt], axis=0)
    mid = CONV_K // 2
    for j in range(xm_ref.shape[1] // CHUNK):
        rows = pl.ds(j * CHUNK, CHUNK)
        win = xpad[j * CHUNK:(j + 2) * CHUNK]
        sh = _dot(shift_ref[...], win)
        dt, da = _dt_and_da(dt_ref[0, rows, :], dtb_ref, alog_ref)
        dtx_f = _split_dot(dt, ef_ref[...])
        dtx_b = _split_dot(dt, eb_ref[...])
        yield
        acc = cbias_ref[...] + win[CONV_HALO:CONV_HALO + CHUNK].astype(F32) * cw_ref[mid:mid + 1, :]
        for a, k in enumerate([k for k in range(CONV_K) if k != mid]):
            acc = acc + sh[a * CHUNK:(a + 1) * CHUNK] * cw_ref[k:k + 1, :]
        act = _silu(acc)
        xs = act[:, :SSD_WIDTH]
        bc = act[:, SSD_WIDTH:].astype(BF16)
        xdtb_ref[0, rows, :] = (xs * dtx_b).astype(BF16)
        bc_ref[0, rows, :] = bc
        y = yield from _ssd_chunk("fwd", da, tri_ref, ef_ref, bc[:, SSD_GROUPS * D_STATE:],
                                  bc[:, :SSD_GROUPS * D_STATE], lambda: xs * dtx_f, h_ref)
        yp_ref[0, rows, :] = y + xs * dskip_ref[...]
        yield


def _ssd_fwd_kernel(*refs):
    i = pl.program_id(1)
    h_ref = refs[-1]

    @pl.when(i == 0)
    def _():
        h_ref[...] = jnp.zeros_like(h_ref)

    for _ in _ssd_fwd_tile(*refs, i > 0, i < pl.num_programs(1) - 1):
        pass


def _ssd_bwd_tile(bc_ref, xdtb_ref, yp_ref, dt_ref, z_ref, dtb_ref, alog_ref, nw_ref, tri_ref, eb_ref,
                  h_ref, reset, store):
    h_ref[...] = jnp.where(reset, 0.0, h_ref[...])
    for j in reversed(range(bc_ref.shape[1] // CHUNK)):
        rows = pl.ds(j * CHUNK, CHUNK)
        bc = bc_ref[0, rows, :]
        _, da = _dt_and_da(dt_ref[0, rows, :], dtb_ref, alog_ref)
        y = yield from _ssd_chunk("bwd", da, tri_ref, eb_ref, bc[:, SSD_GROUPS * D_STATE:],
                                  bc[:, :SSD_GROUPS * D_STATE],
                                  lambda: xdtb_ref[0, rows, :].astype(F32), h_ref)
        y = (y + yp_ref[0, rows, :]) * _silu(z_ref[0, rows, :])
        ms = jnp.mean(y * y, axis=-1, keepdims=True)
        store(j * CHUNK, (y * lax.rsqrt(ms + EPS) * nw_ref[...]).astype(BF16))
        yield


def _ssd_fwd(xbc, dt, conv_w8, conv_b, dtb, alog, dskip):
    b, s, conv_dim = xbc.shape
    n_steps = s // SEQ_ROWS
    per_halo = SEQ_ROWS // CONV_HALO
    n_halo = s // CONV_HALO
    tri_lo, _, exp_f, exp_b, shift = _ssd_constants()
    fwd = lambda w: pl.BlockSpec((1, SEQ_ROWS, w), lambda bi, i: (bi, i, 0))
    params = pltpu.CompilerParams(dimension_semantics=("arbitrary", "arbitrary"))
    return pl.pallas_call(
        _ssd_fwd_kernel,
        grid=(b, n_steps),
        in_specs=[fwd(conv_dim),
                  pl.BlockSpec((1, CONV_HALO, conv_dim), lambda bi, i: (bi, jnp.maximum(i * per_halo - 1, 0), 0)),
                  pl.BlockSpec((1, CONV_HALO, conv_dim),
                               lambda bi, i: (bi, jnp.minimum((i + 1) * per_halo, n_halo - 1), 0)),
                  fwd(LANES),
                  _const_spec(conv_w8.shape), _const_spec((1, conv_dim)),
                  _const_spec((1, LANES)), _const_spec((1, LANES)), _const_spec((1, SSD_WIDTH)),
                  _const_spec(tri_lo.shape), _const_spec(exp_f.shape), _const_spec(exp_b.shape),
                  _const_spec(shift.shape)],
        out_specs=[fwd(BC_WIDTH), fwd(SSD_WIDTH), fwd(SSD_WIDTH)],
        out_shape=[jax.ShapeDtypeStruct((b, s, BC_WIDTH), BF16),
                   jax.ShapeDtypeStruct((b, s, SSD_WIDTH), BF16),
                   jax.ShapeDtypeStruct((b, s, SSD_WIDTH), F32)],
        scratch_shapes=[pltpu.VMEM((D_STATE, SSD_WIDTH), F32)],
        compiler_params=params,
        name="ssd_fwd",
    )(xbc, xbc, xbc, dt, conv_w8, conv_b, dtb, alog, dskip, tri_lo, exp_f, exp_b, shift)


def _head_prep_t(t, w_b, cos, sin):
    ms = jnp.sum(t * t, axis=0, keepdims=True) * (1.0 / HEAD_DIM)
    tn = t * lax.rsqrt(ms + EPS) * w_b
    half = ROPE_DIMS // 2
    t1 = tn[0:half]
    t2 = tn[half:ROPE_DIMS]
    return jnp.concatenate([t1 * cos - t2 * sin, t2 * cos + t1 * sin, tn[ROPE_DIMS:]], axis=0)


def _attn_tile(q_ref, kvp_ref, kvc_ref, kvn_ref, csp_ref, csc_ref, csn_ref, qw_ref, kw_ref, sink_ref,
               has_prev, has_next, store):
    tq = q_ref.shape[1]
    nsub = tq // BLOCK
    nw = tq + 2 * BLOCK
    kvw = ATTN_KV_HEADS * HEAD_DIM
    half = ROPE_DIMS // 2
    q_per_kv = ATTN_HEADS // ATTN_KV_HEADS

    kv = jnp.concatenate([kvp_ref[0], kvc_ref[0], kvn_ref[0]], axis=0)
    cs = jnp.concatenate([csp_ref[0], csc_ref[0], csn_ref[0]], axis=1)
    k_t = kv[:, :kvw].T
    v_t = kv[:, kvw:].T.astype(BF16)
    kw_b = jnp.tile(kw_ref[...], (1, nw // LANES))
    k_prep = jnp.concatenate(
        [_head_prep_t(k_t[g * HEAD_DIM:(g + 1) * HEAD_DIM], kw_b, cs[0:half], cs[half:])
         for g in range(ATTN_KV_HEADS)], axis=0)
    keys = k_prep.T.astype(BF16)

    q_t = q_ref[0].T
    qw_b = jnp.tile(qw_ref[...] * (HEAD_DIM ** -0.5), (1, tq // LANES))
    cs_q = csc_ref[0]
    zeros = jnp.zeros((HEAD_DIM, tq), BF16)
    q_heads = []
    for h in range(ATTN_HEADS):
        qh = _head_prep_t(q_t[h * HEAD_DIM:(h + 1) * HEAD_DIM], qw_b, cs_q[0:half], cs_q[half:]).astype(BF16)
        q_heads.append(jnp.concatenate([qh, zeros] if h // q_per_kv == 0 else [zeros, qh], axis=0))

    yield
    kr = _iota2((BLOCK, BLOCK), 0)
    qc = _iota2((BLOCK, BLOCK), 1)
    bias_prev = jnp.where(kr >= qc, 0.0, NEG_BIG)
    bias_next = jnp.where(kr <= qc, 0.0, NEG_BIG)
    sink_row = jnp.concatenate([jnp.full((1, BLOCK), sink_ref[h], F32) for h in range(ATTN_HEADS)], axis=1)
    gq = q_per_kv * BLOCK
    for j in range(nsub):
        bp = bias_prev if j > 0 else jnp.where(has_prev, bias_prev, NEG_BIG)
        bn = bias_next if j < nsub - 1 else jnp.where(has_next, bias_next, NEG_BIG)
        lo, hi = j * BLOCK, (j + 3) * BLOCK
        q_all = jnp.concatenate([qh[:, lo:lo + BLOCK] for qh in q_heads], axis=1)
        s = _dot(keys[lo:hi], q_all)
        yield
        s0 = s[0:BLOCK] + jnp.tile(bp, (1, ATTN_HEADS))
        s1 = s[BLOCK:2 * BLOCK]
        s2 = s[2 * BLOCK:] + jnp.tile(bn, (1, ATTN_HEADS))
        m = jnp.maximum(jnp.max(jnp.maximum(jnp.maximum(s0, s1), s2), axis=0, keepdims=True), sink_row)
        p0 = jnp.exp(s0 - m)
        p1 = jnp.exp(s1 - m)
        p2 = jnp.exp(s2 - m)
        denom = jnp.sum(p0 + p1 + p2, axis=0, keepdims=True) + jnp.exp(sink_row - m)
        p = jnp.concatenate([p0, p1, p2], axis=0).astype(BF16)
        inv = 1.0 / denom
        outs = []
        for g in range(ATTN_KV_HEADS):
            o = _dot(v_t[g * HEAD_DIM:(g + 1) * HEAD_DIM, lo:hi], p[:, g * gq:(g + 1) * gq])
            o = o * inv[:, g * gq:(g + 1) * gq]
            outs += [o[:, k * BLOCK:(k + 1) * BLOCK] for k in range(q_per_kv)]
        store(lo, jnp.concatenate(outs, axis=0).T.astype(BF16))
        yield


def _tail_kernel(n_tiles, tiles_per_seq,
                 h_ref, mod_ref, n3_ref, wo_ref, wg_ref, wu_ref, wd_ref,
                 q_ref, kvp_ref, kvc_ref, kvn_ref, csp_ref, csc_ref, csn_ref, qw_ref, kw_ref, sink_ref,
                 bc_ref, xdtb_ref, yp_ref, dt_ref, z_ref, dtb_ref, alog_ref, nw_ref, tri_ref, eb_ref,
                 o_ref, y_scr, hstate_ref):
    k = pl.program_id(0)

    @pl.when(k == 0)
    def _():
        y_scr[...] = jnp.zeros_like(y_scr)
        hstate_ref[...] = jnp.zeros_like(hstate_ref)

    rd = k % 2
    wr = 1 - rd

    t = jnp.maximum(n_tiles - 1 - k, 0) % tiles_per_seq
    is_first = t == 0
    is_last = t == tiles_per_seq - 1

    def store_ssd(row0, y):
        y_scr[wr, row0:row0 + CHUNK, 0:SSD_WIDTH] = y

    def store_attn(row0, y):
        y_scr[wr, row0:row0 + BLOCK, SSD_WIDTH:] = y

    mixers = (
        _attn_tile(q_ref, kvp_ref, kvc_ref, kvn_ref, csp_ref, csc_ref, csn_ref, qw_ref, kw_ref, sink_ref,
                   jnp.logical_not(is_first), jnp.logical_not(is_last), store_attn),
        _ssd_bwd_tile(bc_ref, xdtb_ref, yp_ref, dt_ref, z_ref, dtb_ref, alog_ref, nw_ref, tri_ref, eb_ref,
                      hstate_ref, is_last, store_ssd))

    h1 = h_ref[0]
    mod = mod_ref[0]
    h2 = h1 + (1.0 + mod[5:6]) * _dot(y_scr[rd], wo_ref[...])
    u = _ada_norm(h2, n3_ref[...] * (1.0 + mod[7:8]), mod[6:7]).astype(BF16)
    ff = _swiglu(u, wg_ref, wu_ref, wd_ref, side_work=mixers)
    o_ref[0] = h2 + (0.5 * (1.0 + mod[8:9])) * ff


def _tail(h1, mod, norm3, w_out, wg, wu, wd, q, kv, cs_tab, qw_b, kw_b, sink,
          bc, xdtb, ypart, dt, z, dtb, alog, norm_w):
    b, s, d = h1.shape
    d_ff = wg.shape[1]
    tps = s // SEQ_ROWS
    n_tiles = b * tps
    nsub = SEQ_ROWS // BLOCK
    nb = s // BLOCK
    aw = q.shape[-1]
    kvw2 = kv.shape[-1]
    cs_rows = cs_tab.shape[1]
    _, tri_up, _, exp_b, _ = _ssd_constants()

    ffn_tile = lambda k: jnp.clip(n_tiles - k, 0, n_tiles - 1)
    mix_tile = lambda k: jnp.maximum(n_tiles - 1 - k, 0)
    prev_blk = lambda t: jnp.maximum(t * nsub - 1, 0)
    next_blk = lambda t: jnp.minimum((t + 1) * nsub, nb - 1)

    def ffn_tok(w):
        return pl.BlockSpec((1, FFN_ROWS, w), lambda k: (ffn_tile(k) // tps, ffn_tile(k) % tps, 0))

    def mix_tok(w):
        return pl.BlockSpec((1, SEQ_ROWS, w), lambda k: (mix_tile(k) // tps, mix_tile(k) % tps, 0))

    in_specs = [
        ffn_tok(d),
        pl.BlockSpec((1, N_MOD, d), lambda k: (ffn_tile(k) // tps, 0, 0)),
        _const_spec((1, d)), _const_spec(w_out.shape),
        _const_spec((d, d_ff)), _const_spec((d, d_ff)), _const_spec((d_ff, d)),
        mix_tok(aw),
        pl.BlockSpec((1, BLOCK, kvw2), lambda k: (mix_tile(k) // tps, prev_blk(mix_tile(k) % tps), 0)),
        mix_tok(kvw2),
        pl.BlockSpec((1, BLOCK, kvw2), lambda k: (mix_tile(k) // tps, next_blk(mix_tile(k) % tps), 0)),
        pl.BlockSpec((1, cs_rows, BLOCK), lambda k: (mix_tile(k) // tps, 0, prev_blk(mix_tile(k) % tps))),
        pl.BlockSpec((1, cs_rows, SEQ_ROWS), lambda k: (mix_tile(k) // tps, 0, mix_tile(k) % tps)),
        pl.BlockSpec((1, cs_rows, BLOCK), lambda k: (mix_tile(k) // tps, 0, next_blk(mix_tile(k) % tps))),
        _const_spec((HEAD_DIM, LANES)), _const_spec((HEAD_DIM, LANES)),
        pl.BlockSpec(memory_space=pltpu.SMEM),
        mix_tok(BC_WIDTH), mix_tok(SSD_WIDTH), mix_tok(SSD_WIDTH), mix_tok(LANES), mix_tok(SSD_WIDTH),
        _const_spec((1, LANES)), _const_spec((1, LANES)), _const_spec((1, SSD_WIDTH)),
        _const_spec(tri_up.shape), _const_spec(exp_b.shape),
    ]
    return pl.pallas_call(
        functools.partial(_tail_kernel, n_tiles, tps),
        grid=(n_tiles + 1,),
        in_specs=in_specs,
        out_specs=ffn_tok(d),
        out_shape=jax.ShapeDtypeStruct((b, s, d), F32),
        scratch_shapes=[pltpu.VMEM((2, SEQ_ROWS, SSD_WIDTH + aw), BF16),
                        pltpu.VMEM((D_STATE, SSD_WIDTH), F32)],
        compiler_params=pltpu.CompilerParams(dimension_semantics=("arbitrary",),
                                             vmem_limit_bytes=VMEM_LIMIT_BYTES),
        name="mixers_outproj_ffn2",
    )(h1, mod, norm3, w_out, wg, wu, wd, q, kv, kv, kv, cs_tab, cs_tab, cs_tab, qw_b, kw_b, sink,
      bc, xdtb, ypart, dt, z, dtb, alog, norm_w, tri_up, exp_b)


def _pad_inproj(w_in):
    n_dt = 2 * SSD_HEADS
    s_xbc = SSD_WIDTH + CONV_DIM
    s_dt = s_xbc + n_dt
    dt_cols = jnp.pad(w_in[:, s_xbc:s_dt], ((0, 0), (0, LANES - n_dt)))
    return jnp.concatenate([w_in[:, :s_xbc], w_in[:, s_dt:], dt_cols], axis=1).astype(BF16)


def _pad_lanes(v, width=LANES):
    v = v.reshape(1, -1)
    return jnp.pad(v, ((0, 0), (0, width - v.shape[1])))


def _lane_bcast(v):
    return jnp.broadcast_to(v[:, None], (v.shape[0], LANES))


def kernel(x, c, positions, w_ada, b_ada, norm_ffn1, ffn1_wg, ffn1_wu, ffn1_wd, norm_mix, w_in, conv_w,
           conv_b, dt_bias, a_log, d_skip, ssd_norm_w, q_norm_w, k_norm_w, sink_logit, w_out, norm_ffn2,
           ffn2_wg, ffn2_wu, ffn2_wd):
    depth = w_ada.shape[0]
    b, s, d = x.shape
    h = x.astype(F32)
    c_pad = jnp.pad(c.astype(F32), ((0, -b % SUBLANES), (0, 0)))
    cs_tab = _rope_table(positions)
    for l in range(depth):
        mod = _adaln_mod(c_pad, w_ada[l], b_ada[l])[:b].reshape(b, N_MOD, d)
        h1, z, xbc, q, kv, dt = _ffn1_inproj(
            h, mod, norm_ffn1[l].reshape(1, d), norm_mix[l].reshape(1, d),
            ffn1_wg[l].astype(BF16), ffn1_wu[l].astype(BF16), ffn1_wd[l].astype(BF16), _pad_inproj(w_in[l]))
        conv_w8 = jnp.pad(conv_w[l], ((0, SUBLANES - CONV_K), (0, 0)))
        dtb, alog = _pad_lanes(dt_bias[l]), _pad_lanes(a_log[l])
        bc, xdtb, ypart = _ssd_fwd(xbc, dt, conv_w8, conv_b[l].reshape(1, -1), dtb, alog,
                                   jnp.repeat(d_skip[l], SSD_HEAD_DIM).reshape(1, -1))
        h = _tail(h1, mod, norm_ffn2[l].reshape(1, d), w_out[l].astype(BF16),
                  ffn2_wg[l].astype(BF16), ffn2_wu[l].astype(BF16), ffn2_wd[l].astype(BF16),
                  q, kv, cs_tab, _lane_bcast(q_norm_w[l]), _lane_bcast(k_norm_w[l]), sink_logit[l],
                  bc, xdtb, ypart, dt, z, dtb, alog, ssd_norm_w[l].reshape(1, -1))
    return h.astype(x.dtype)
```

```python
import functools

import jax
import jax.numpy as jnp
import numpy as np
from jax import lax
from jax.experimental import pallas as pl
from jax.experimental.pallas import tpu as pltpu

F32 = jnp.float32
BF16 = jnp.bfloat16

SSD_HEAD_DIM = 64
SSD_HEADS = 8
SSD_GROUPS = 2
D_STATE = 128
CONV_K = 5
CHUNK = 128
HEAD_DIM = 64
ATTN_HEADS = 8
ATTN_KV_HEADS = 2
WINDOW = 128
BLOCK = 128
ROPE_DIMS = 16
ROPE_THETA = 500000.0
N_MOD = 9
EPS = 1e-6

SSD_WIDTH = SSD_HEADS * SSD_HEAD_DIM
BC_WIDTH = 2 * SSD_GROUPS * D_STATE
CONV_DIM = SSD_WIDTH + BC_WIDTH

LANES = 128
SUBLANES = 8
VMEM_LIMIT_BYTES = 56 * 1024 * 1024

FFN_ROWS = 512
FFN_CHUNK = 256
SEQ_ROWS = 512
CONV_HALO = 16

NEG_BIG = -1e30


def _dot(a, b):
    return jnp.dot(a, b, preferred_element_type=F32)


def _dot_nt(a, b):
    return lax.dot_general(a, b, (((1,), (1,)), ((), ())), preferred_element_type=F32)


def _split_dot(x, m_bf16, terms=3, lhs=True):
    acc = None
    r = x
    for t in range(terms):
        h = r.astype(BF16)
        d = _dot(h, m_bf16) if lhs else _dot(m_bf16, h)
        acc = d if acc is None else acc + d
        if t + 1 < terms:
            r = r - h.astype(F32)
    return acc


def _silu(x):
    return x * jax.nn.sigmoid(x)


def _iota2(shape, dim):
    return lax.broadcasted_iota(jnp.int32, shape, dim)


def _const_spec(shape):
    nd = len(shape)
    return pl.BlockSpec(shape, lambda *_: (0,) * nd, pipeline_mode=pl.Buffered(1))


def _mod_kernel(c_ref, w_ref, b_ref, o_ref):
    cs = _silu(c_ref[...])
    o_ref[...] = _dot(cs, w_ref[...]) + b_ref[...]


def _adaln_mod(c_pad, w_ada, b_ada):
    rows, d = c_pad.shape
    n = w_ada.shape[1]
    bn = d
    return pl.pallas_call(
        _mod_kernel,
        grid=(n // bn,),
        in_specs=[pl.BlockSpec((rows, d), lambda j: (0, 0)),
                  pl.BlockSpec((d, bn), lambda j: (0, j)),
                  pl.BlockSpec((1, bn), lambda j: (0, j))],
        out_specs=pl.BlockSpec((rows, bn), lambda j: (0, j)),
        out_shape=jax.ShapeDtypeStruct((rows, n), F32),
        name="adaln_mod",
    )(c_pad, w_ada, b_ada.reshape(1, n))


def _rope_kernel(pos_ref, inv_ref, o_ref):
    half = ROPE_DIMS // 2
    s = pos_ref.shape[-1]
    p = pos_ref[0].astype(F32)
    ang = jnp.tile(inv_ref[...], (1, s // LANES)) * p
    o_ref[0, 0:half, :] = jnp.cos(ang)
    o_ref[0, half:, :] = jnp.sin(ang)


def _rope_table(positions):
    half = ROPE_DIMS // 2
    inv = ROPE_THETA ** (-jnp.arange(half, dtype=F32) * 2.0 / ROPE_DIMS)
    b, s = positions.shape
    return pl.pallas_call(
        _rope_kernel,
        grid=(b,),
        in_specs=[pl.BlockSpec((1, 1, s), lambda bi: (bi, 0, 0)), _const_spec((half, LANES))],
        out_specs=pl.BlockSpec((1, 2 * half, s), lambda bi: (bi, 0, 0)),
        out_shape=jax.ShapeDtypeStruct((b, 2 * half, s), F32),
        name="rope_table",
    )(positions.reshape(b, 1, s), jnp.broadcast_to(inv[:, None], (half, LANES)))


def _ada_norm(x, gain_scale, shift):
    ms = jnp.mean(x * x, axis=-1, keepdims=True)
    return x * lax.rsqrt(ms + EPS) * gain_scale + shift


def _swiglu(ub, wg_ref, wu_ref, wd_ref, side_work=(), drain=True):
    d_ff = wg_ref.shape[1]
    acc = None
    for c0 in range(0, d_ff, FFN_CHUNK):
        c1 = min(c0 + FFN_CHUNK, d_ff)
        g = _dot(ub, wg_ref[:, c0:c1])
        up = _dot(ub, wu_ref[:, c0:c1])
        a = (_silu(g) * up).astype(BF16)
        d = _dot(a, wd_ref[c0:c1, :])
        acc = d if acc is None else acc + d
        for gen in side_work:
            next(gen, None)
    for gen in side_work if drain else ():
        for _ in gen:
            pass
    return acc


_INPROJ_LAYOUT = (("z", SSD_WIDTH, F32), ("xbc", CONV_DIM, BF16), ("q", ATTN_HEADS * HEAD_DIM, F32),
                  ("kv", 2 * ATTN_KV_HEADS * HEAD_DIM, F32), ("dt", LANES, F32))


def _ssd_constants():
    r = np.arange(CHUNK)[:, None]
    c = np.arange(CHUNK)[None, :]
    return jnp.asarray(c <= r, dtype=BF16), jnp.asarray(c >= r, dtype=BF16)


def _dt_and_da(dt_raw, dtb_ref, alog_ref):
    lane = _iota2(dt_raw.shape, 1)
    dt = jnp.where(lane < 2 * SSD_HEADS, jax.nn.softplus(dt_raw + dtb_ref[...]), 0.0)
    a = -jnp.exp(alog_ref[...])
    return dt, dt * a


def _head_columns(x, lane0):
    return [jnp.broadcast_to(x[:, lane0 + h:lane0 + h + 1], x.shape) for h in range(SSD_HEADS)]


def _expand_heads(cols):
    first = _iota2(cols[0].shape, 1) < SSD_HEAD_DIM
    return jnp.concatenate([jnp.where(first, cols[2 * p], cols[2 * p + 1]) for p in range(SSD_HEADS // 2)],
                           axis=1)


def _ssd_chunk(direction, da, tri_ref, cm, bm, xdt_fn, h_ref):
    gw = SSD_WIDTH // SSD_GROUPS
    lane0 = 0 if direction == "fwd" else SSD_HEADS
    tot_row = CHUNK - 1 if direction == "fwd" else 0
    cgs = [cm[:, g * D_STATE:(g + 1) * D_STATE] for g in range(SSD_GROUPS)]
    bgs = [bm[:, g * D_STATE:(g + 1) * D_STATE] for g in range(SSD_GROUPS)]
    hgs = [h_ref[:, g * gw:(g + 1) * gw] for g in range(SSD_GROUPS)]

    cs = _split_dot(da, tri_ref[...], lhs=False)
    cbs = [_dot_nt(cgs[g], bgs[g]) for g in range(SSD_GROUPS)]
    chs = [_dot(cgs[g], hgs[g].astype(BF16)) for g in range(SSD_GROUPS)]
    yield

    xdt = xdt_fn()
    cs_cols = _head_columns(cs, lane0)
    csx = _expand_heads(cs_cols)
    tot = csx[tot_row:tot_row + 1, :]
    e_in = jnp.exp(csx)
    xdec = (xdt * jnp.exp(tot - csx)).astype(BF16)
    cdec = jnp.exp(tot)
    cst = cs.T
    row = _iota2((CHUNK, CHUNK), 0)
    col = _iota2((CHUNK, CHUNK), 1)
    keep = (col <= row) if direction == "fwd" else (col > row)
    half = _iota2((CHUNK, LANES), 1) // SSD_HEAD_DIM
    xdt_b = xdt.astype(BF16)
    ys = []
    for g in range(SSD_GROUPS):
        pairs = []
        for pp in range(gw // LANES):
            p = g * (gw // LANES) + pp
            xp = xdt_b[:, p * LANES:(p + 1) * LANES]
            yp = None
            for e in range(2):
                hcol = lane0 + 2 * p + e
                seg = cs_cols[2 * p + e] - cst[hcol:hcol + 1, :]
                m = (cbs[g] * jnp.exp(jnp.where(keep, seg, NEG_BIG))).astype(BF16)
                d = _dot(m, jnp.where(half == e, xp, jnp.zeros_like(xp)))
                yp = d if yp is None else yp + d
            pairs.append(yp)
        ys.append(jnp.concatenate(pairs, axis=1) + chs[g] * e_in[:, g * gw:(g + 1) * gw])
        bgt = bgs[g].astype(F32).T.astype(BF16)
        st = _dot(bgt, xdec[:, g * gw:(g + 1) * gw])
        h_ref[:, g * gw:(g + 1) * gw] = hgs[g] * cdec[:, g * gw:(g + 1) * gw] + st
    return jnp.concatenate(ys, axis=1)


def _ssd_fwd_tile(window_fn, dt_fn, cw_ref, cbias_ref, dtb_ref, alog_ref, dskip_ref,
                  tri_ref, bc_ref, xdtb_ref, yp_ref, h_ref):
    mid = CONV_K // 2
    for j in range(SEQ_ROWS // CHUNK):
        rows = pl.ds(j * CHUNK, CHUNK)
        win = window_fn(j)
        dt, da = _dt_and_da(dt_fn(j), dtb_ref, alog_ref)
        dtx_f = _expand_heads(_head_columns(dt, 0))
        dtx_b = _expand_heads(_head_columns(dt, SSD_HEADS))
        yield
        winf = win.astype(F32)
        acc = cbias_ref[...] + winf[CONV_HALO:CONV_HALO + CHUNK] * cw_ref[mid:mid + 1, :]
        for k in [k for k in range(CONV_K) if k != mid]:
            rolled = pltpu.roll(winf, (mid - k) % winf.shape[0], axis=0)
            acc = acc + rolled[CONV_HALO:CONV_HALO + CHUNK] * cw_ref[k:k + 1, :]
        act = _silu(acc)
        xs = act[:, :SSD_WIDTH]
        bc = act[:, SSD_WIDTH:].astype(BF16)
        xdtb_ref[0, rows, :] = (xs * dtx_b).astype(BF16)
        bc_ref[0, rows, :] = bc
        y = yield from _ssd_chunk("fwd", da, tri_ref, bc[:, SSD_GROUPS * D_STATE:],
                                  bc[:, :SSD_GROUPS * D_STATE], lambda: xs * dtx_f, h_ref)
        yp_ref[0, rows, :] = y + xs * dskip_ref[...]
        yield


def _ssd_bwd_tile(bc_ref, xdtb_ref, yp_ref, dt_ref, z_ref, dtb_ref, alog_ref, nw_ref, tri_ref,
                  h_ref, reset, store):
    h_ref[...] = jnp.where(reset, 0.0, h_ref[...])
    for j in reversed(range(bc_ref.shape[1] // CHUNK)):
        rows = pl.ds(j * CHUNK, CHUNK)
        bc = bc_ref[0, rows, :]
        _, da = _dt_and_da(dt_ref[0, rows, :], dtb_ref, alog_ref)
        y = yield from _ssd_chunk("bwd", da, tri_ref, bc[:, SSD_GROUPS * D_STATE:],
                                  bc[:, :SSD_GROUPS * D_STATE],
                                  lambda: xdtb_ref[0, rows, :].astype(F32), h_ref)
        y = (y + yp_ref[0, rows, :]) * _silu(z_ref[0, rows, :])
        ms = jnp.mean(y * y, axis=-1, keepdims=True)
        store(j * CHUNK, (y * lax.rsqrt(ms + EPS) * nw_ref[...]).astype(BF16))
        yield


RING = 3


def _head_kernel(n_tiles, tiles_per_seq,
                 x_ref, mod_ref, n1_ref, n2_ref, wg_ref, wu_ref, wd_ref, win_ref,
                 cw_ref, cbias_ref, dtb_ref, alog_ref, dskip_ref, tri_ref,
                 h_ref, z_ref, q_ref, kv_ref, dt_ref, bc_ref, xdtb_ref, yp_ref,
                 xstage, dtstage, xring, dtring, hstate_ref):
    k = pl.program_id(0)

    @pl.when(k == 0)
    def _():
        xstage[...] = jnp.zeros_like(xstage)
        dtstage[...] = jnp.zeros_like(dtstage)
        xring[...] = jnp.zeros_like(xring)
        dtring[...] = jnp.zeros_like(dtring)
        hstate_ref[...] = jnp.zeros_like(hstate_ref)

    s_new = (k + RING - 1) % RING
    xring[s_new] = xstage[...]
    dtring[s_new] = dtstage[...]

    t = jnp.maximum(k - 2, 0) % tiles_per_seq
    s_main = (k + RING - 2) % RING
    s_prev = k % RING
    has_prev = t > 0
    has_next = t < tiles_per_seq - 1
    hstate_ref[...] = jnp.where(has_prev, hstate_ref[...], 0.0)
    nsub = SEQ_ROWS // CHUNK

    def window(j):
        lo = j * CHUNK - CONV_HALO
        hi = (j + 1) * CHUNK + CONV_HALO
        parts = []
        if lo < 0:
            halo = xring[s_prev, SEQ_ROWS + lo:, :]
            parts.append(jnp.where(has_prev, halo, jnp.zeros_like(halo)))
        parts.append(xring[s_main, max(lo, 0):min(hi, SEQ_ROWS), :])
        if hi > SEQ_ROWS:
            halo = xring[s_new, 0:hi - SEQ_ROWS, :]
            parts.append(jnp.where(has_next, halo, jnp.zeros_like(halo)))
        return parts[0] if len(parts) == 1 else jnp.concatenate(parts, axis=0)

    scan = _ssd_fwd_tile(window, lambda j: dtring[s_main, j * CHUNK:(j + 1) * CHUNK, :],
                         cw_ref, cbias_ref, dtb_ref, alog_ref, dskip_ref, tri_ref,
                         bc_ref, xdtb_ref, yp_ref, hstate_ref)

    x = x_ref[0]
    mod = mod_ref[0]
    u = _ada_norm(x, n1_ref[...] * (1.0 + mod[1:2]), mod[0:1]).astype(BF16)
    ff = _swiglu(u, wg_ref, wu_ref, wd_ref, side_work=(scan,), drain=False)
    h = x + (0.5 * (1.0 + mod[2:3])) * ff
    h_ref[0] = h
    u2 = _ada_norm(h, n2_ref[...] * (1.0 + mod[4:5]), mod[3:4]).astype(BF16)
    dests = {"z": (z_ref,), "xbc": (xstage,), "q": (q_ref,), "kv": (kv_ref,), "dt": (dt_ref, dtstage)}
    c0 = 0
    for name, w, _ in _INPROJ_LAYOUT:
        piece = _dot(u2, win_ref[:, c0:c0 + w])
        for ref in dests[name]:
            if len(ref.shape) == 3:
                ref[0] = piece.astype(ref.dtype)
            else:
                ref[...] = piece.astype(ref.dtype)
        c0 += w
        next(scan, None)
    for _ in scan:
        pass


def _head(x, mod, norm1, norm2, wg, wu, wd, w_in_p, conv_w8, conv_b, dtb, alog, dskip):
    b, s, d = x.shape
    d_ff = wg.shape[1]
    tps = s // SEQ_ROWS
    n_tiles = b * tps
    tri_lo, _ = _ssd_constants()
    ffn_tile = lambda k: jnp.minimum(k, n_tiles - 1)
    scan_tile = lambda k: jnp.maximum(k - 2, 0)

    def ffn_tok(w):
        return pl.BlockSpec((1, FFN_ROWS, w), lambda k: (ffn_tile(k) // tps, ffn_tile(k) % tps, 0))

    def scan_tok(w):
        return pl.BlockSpec((1, SEQ_ROWS, w), lambda k: (scan_tile(k) // tps, scan_tile(k) % tps, 0))

    widths = {name: (w, dt) for name, w, dt in _INPROJ_LAYOUT}
    hbm_outs = ("z", "q", "kv", "dt")
    out_shape = [jax.ShapeDtypeStruct((b, s, d), F32)]
    out_shape += [jax.ShapeDtypeStruct((b, s, widths[n][0]), widths[n][1]) for n in hbm_outs]
    out_shape += [jax.ShapeDtypeStruct((b, s, BC_WIDTH), BF16), jax.ShapeDtypeStruct((b, s, SSD_WIDTH), BF16),
                  jax.ShapeDtypeStruct((b, s, SSD_WIDTH), F32)]
    out_specs = [ffn_tok(d)] + [ffn_tok(widths[n][0]) for n in hbm_outs]
    out_specs += [scan_tok(BC_WIDTH), scan_tok(SSD_WIDTH), scan_tok(SSD_WIDTH)]
    return pl.pallas_call(
        functools.partial(_head_kernel, n_tiles, tps),
        grid=(n_tiles + 2,),
        in_specs=[ffn_tok(d),
                  pl.BlockSpec((1, N_MOD, d), lambda k: (ffn_tile(k) // tps, 0, 0)),
                  _const_spec((1, d)), _const_spec((1, d)),
                  _const_spec((d, d_ff)), _const_spec((d, d_ff)), _const_spec((d_ff, d)),
                  _const_spec(w_in_p.shape),
                  _const_spec(conv_w8.shape), _const_spec((1, CONV_DIM)),
                  _const_spec((1, LANES)), _const_spec((1, LANES)), _const_spec((1, SSD_WIDTH)),
                  _const_spec(tri_lo.shape)],
        out_specs=out_specs,
        out_shape=out_shape,
        scratch_shapes=[pltpu.VMEM((SEQ_ROWS, CONV_DIM), BF16), pltpu.VMEM((SEQ_ROWS, LANES), F32),
                        pltpu.VMEM((RING, SEQ_ROWS, CONV_DIM), BF16), pltpu.VMEM((RING, SEQ_ROWS, LANES), F32),
                        pltpu.VMEM((D_STATE, SSD_WIDTH), F32)],
        compiler_params=pltpu.CompilerParams(dimension_semantics=("arbitrary",),
                                             vmem_limit_bytes=VMEM_LIMIT_BYTES),
        name="ffn1_inproj_scan",
    )(x, mod, norm1, norm2, wg, wu, wd, w_in_p, conv_w8, conv_b, dtb, alog, dskip, tri_lo)


def _head_prep_t(t, w_b, cos, sin):
    ms = jnp.sum(t * t, axis=0, keepdims=True) * (1.0 / HEAD_DIM)
    tn = t * lax.rsqrt(ms + EPS) * w_b
    half = ROPE_DIMS // 2
    t1 = tn[0:half]
    t2 = tn[half:ROPE_DIMS]
    return jnp.concatenate([t1 * cos - t2 * sin, t2 * cos + t1 * sin, tn[ROPE_DIMS:]], axis=0)


def _attn_tile(q_ref, kvp_ref, kvc_ref, kvn_ref, csp_ref, csc_ref, csn_ref, qw_ref, kw_ref, sink_ref,
               has_prev, has_next, store):
    tq = q_ref.shape[1]
    nsub = tq // BLOCK
    nw = tq + 2 * BLOCK
    kvw = ATTN_KV_HEADS * HEAD_DIM
    half = ROPE_DIMS // 2
    q_per_kv = ATTN_HEADS // ATTN_KV_HEADS

    kv = jnp.concatenate([kvp_ref[0], kvc_ref[0], kvn_ref[0]], axis=0)
    cs = jnp.concatenate([csp_ref[0], csc_ref[0], csn_ref[0]], axis=1)
    k_t = kv[:, :kvw].T
    v_t = kv[:, kvw:].T.astype(BF16)
    kw_b = jnp.tile(kw_ref[...], (1, nw // LANES))
    k_prep = jnp.concatenate(
        [_head_prep_t(k_t[g * HEAD_DIM:(g + 1) * HEAD_DIM], kw_b, cs[0:half], cs[half:])
         for g in range(ATTN_KV_HEADS)], axis=0)
    keys = k_prep.T.astype(BF16)

    q_t = q_ref[0].T
    qw_b = jnp.tile(qw_ref[...] * (HEAD_DIM ** -0.5), (1, tq // LANES))
    cs_q = csc_ref[0]
    zeros = jnp.zeros((HEAD_DIM, tq), BF16)
    q_heads = []
    for h in range(ATTN_HEADS):
        qh = _head_prep_t(q_t[h * HEAD_DIM:(h + 1) * HEAD_DIM], qw_b, cs_q[0:half], cs_q[half:]).astype(BF16)
        q_heads.append(jnp.concatenate([qh, zeros] if h // q_per_kv == 0 else [zeros, qh], axis=0))

    yield
    kr = _iota2((BLOCK, BLOCK), 0)
    qc = _iota2((BLOCK, BLOCK), 1)
    bias_prev = jnp.where(kr >= qc, 0.0, NEG_BIG)
    bias_next = jnp.where(kr <= qc, 0.0, NEG_BIG)
    sink_row = jnp.concatenate([jnp.full((1, BLOCK), sink_ref[h], F32) for h in range(ATTN_HEADS)], axis=1)
    gq = q_per_kv * BLOCK
    for j in range(nsub):
        bp = bias_prev if j > 0 else jnp.where(has_prev, bias_prev, NEG_BIG)
        bn = bias_next if j < nsub - 1 else jnp.where(has_next, bias_next, NEG_BIG)
        lo, hi = j * BLOCK, (j + 3) * BLOCK
        q_all = jnp.concatenate([qh[:, lo:lo + BLOCK] for qh in q_heads], axis=1)
        s = _dot(keys[lo:hi], q_all)
        yield
        s0 = s[0:BLOCK] + jnp.tile(bp, (1, ATTN_HEADS))
        s1 = s[BLOCK:2 * BLOCK]
        s2 = s[2 * BLOCK:] + jnp.tile(bn, (1, ATTN_HEADS))
        m = jnp.maximum(jnp.max(jnp.maximum(jnp.maximum(s0, s1), s2), axis=0, keepdims=True), sink_row)
        p0 = jnp.exp(s0 - m)
        p1 = jnp.exp(s1 - m)
        p2 = jnp.exp(s2 - m)
        denom = jnp.sum(p0 + p1 + p2, axis=0, keepdims=True) + jnp.exp(sink_row - m)
        p = jnp.concatenate([p0, p1, p2], axis=0).astype(BF16)
        inv = 1.0 / denom
        outs = []
        for g in range(ATTN_KV_HEADS):
            o = _dot(v_t[g * HEAD_DIM:(g + 1) * HEAD_DIM, lo:hi], p[:, g * gq:(g + 1) * gq])
            o = o * inv[:, g * gq:(g + 1) * gq]
            outs += [o[:, k * BLOCK:(k + 1) * BLOCK] for k in range(q_per_kv)]
        store(lo, jnp.concatenate(outs, axis=0).T.astype(BF16))
        yield


def _tail_kernel(n_tiles, tiles_per_seq,
                 h_ref, mod_ref, n3_ref, wo_ref, wg_ref, wu_ref, wd_ref,
                 q_ref, kvp_ref, kvc_ref, kvn_ref, csp_ref, csc_ref, csn_ref, qw_ref, kw_ref, sink_ref,
                 bc_ref, xdtb_ref, yp_ref, dt_ref, z_ref, dtb_ref, alog_ref, nw_ref, tri_ref,
                 o_ref, y_scr, hstate_ref):
    k = pl.program_id(0)

    @pl.when(k == 0)
    def _():
        y_scr[...] = jnp.zeros_like(y_scr)
        hstate_ref[...] = jnp.zeros_like(hstate_ref)

    rd = k % 2
    wr = 1 - rd

    t = jnp.maximum(n_tiles - 1 - k, 0) % tiles_per_seq
    is_first = t == 0
    is_last = t == tiles_per_seq - 1

    def store_ssd(row0, y):
        y_scr[wr, row0:row0 + CHUNK, 0:SSD_WIDTH] = y

    def store_attn(row0, y):
        y_scr[wr, row0:row0 + BLOCK, SSD_WIDTH:] = y

    mixers = (
        _attn_tile(q_ref, kvp_ref, kvc_ref, kvn_ref, csp_ref, csc_ref, csn_ref, qw_ref, kw_ref, sink_ref,
                   jnp.logical_not(is_first), jnp.logical_not(is_last), store_attn),
        _ssd_bwd_tile(bc_ref, xdtb_ref, yp_ref, dt_ref, z_ref, dtb_ref, alog_ref, nw_ref, tri_ref,
                      hstate_ref, is_last, store_ssd))

    h1 = h_ref[0]
    mod = mod_ref[0]
    h2 = h1 + (1.0 + mod[5:6]) * _dot(y_scr[rd], wo_ref[...])
    u = _ada_norm(h2, n3_ref[...] * (1.0 + mod[7:8]), mod[6:7]).astype(BF16)
    ff = _swiglu(u, wg_ref, wu_ref, wd_ref, side_work=mixers)
    o_ref[0] = h2 + (0.5 * (1.0 + mod[8:9])) * ff


def _tail(h1, mod, norm3, w_out, wg, wu, wd, q, kv, cs_tab, qw_b, kw_b, sink,
          bc, xdtb, ypart, dt, z, dtb, alog, norm_w):
    b, s, d = h1.shape
    d_ff = wg.shape[1]
    tps = s // SEQ_ROWS
    n_tiles = b * tps
    nsub = SEQ_ROWS // BLOCK
    nb = s // BLOCK
    aw = q.shape[-1]
    kvw2 = kv.shape[-1]
    cs_rows = cs_tab.shape[1]
    _, tri_up = _ssd_constants()

    ffn_tile = lambda k: jnp.clip(n_tiles - k, 0, n_tiles - 1)
    mix_tile = lambda k: jnp.maximum(n_tiles - 1 - k, 0)
    prev_blk = lambda t: jnp.maximum(t * nsub - 1, 0)
    next_blk = lambda t: jnp.minimum((t + 1) * nsub, nb - 1)

    def ffn_tok(w):
        return pl.BlockSpec((1, FFN_ROWS, w), lambda k: (ffn_tile(k) // tps, ffn_tile(k) % tps, 0))

    def mix_tok(w):
        return pl.BlockSpec((1, SEQ_ROWS, w), lambda k: (mix_tile(k) // tps, mix_tile(k) % tps, 0))

    in_specs = [
        ffn_tok(d),
        pl.BlockSpec((1, N_MOD, d), lambda k: (ffn_tile(k) // tps, 0, 0)),
        _const_spec((1, d)), _const_spec(w_out.shape),
        _const_spec((d, d_ff)), _const_spec((d, d_ff)), _const_spec((d_ff, d)),
        mix_tok(aw),
        pl.BlockSpec((1, BLOCK, kvw2), lambda k: (mix_tile(k) // tps, prev_blk(mix_tile(k) % tps), 0)),
        mix_tok(kvw2),
        pl.BlockSpec((1, BLOCK, kvw2), lambda k: (mix_tile(k) // tps, next_blk(mix_tile(k) % tps), 0)),
        pl.BlockSpec((1, cs_rows, BLOCK), lambda k: (mix_tile(k) // tps, 0, prev_blk(mix_tile(k) % tps))),
        pl.BlockSpec((1, cs_rows, SEQ_ROWS), lambda k: (mix_tile(k) // tps, 0, mix_tile(k) % tps)),
        pl.BlockSpec((1, cs_rows, BLOCK), lambda k: (mix_tile(k) // tps, 0, next_blk(mix_tile(k) % tps))),
        _const_spec((HEAD_DIM, LANES)), _const_spec((HEAD_DIM, LANES)),
        pl.BlockSpec(memory_space=pltpu.SMEM),
        mix_tok(BC_WIDTH), mix_tok(SSD_WIDTH), mix_tok(SSD_WIDTH), mix_tok(LANES), mix_tok(SSD_WIDTH),
        _const_spec((1, LANES)), _const_spec((1, LANES)), _const_spec((1, SSD_WIDTH)),
        _const_spec(tri_up.shape),
    ]
    return pl.pallas_call(
        functools.partial(_tail_kernel, n_tiles, tps),
        grid=(n_tiles + 1,),
        in_specs=in_specs,
        out_specs=ffn_tok(d),
        out_shape=jax.ShapeDtypeStruct((b, s, d), F32),
        scratch_shapes=[pltpu.VMEM((2, SEQ_ROWS, SSD_WIDTH + aw), BF16),
                        pltpu.VMEM((D_STATE, SSD_WIDTH), F32)],
        compiler_params=pltpu.CompilerParams(dimension_semantics=("arbitrary",),
                                             vmem_limit_bytes=VMEM_LIMIT_BYTES),
        name="mixers_outproj_ffn2",
    )(h1, mod, norm3, w_out, wg, wu, wd, q, kv, kv, kv, cs_tab, cs_tab, cs_tab, qw_b, kw_b, sink,
      bc, xdtb, ypart, dt, z, dtb, alog, norm_w, tri_up)


def _pad_inproj(w_in):
    n_dt = 2 * SSD_HEADS
    s_xbc = SSD_WIDTH + CONV_DIM
    s_dt = s_xbc + n_dt
    dt_cols = jnp.pad(w_in[:, s_xbc:s_dt], ((0, 0), (0, LANES - n_dt)))
    return jnp.concatenate([w_in[:, :s_xbc], w_in[:, s_dt:], dt_cols], axis=1).astype(BF16)


def _pad_lanes(v, width=LANES):
    v = v.reshape(1, -1)
    return jnp.pad(v, ((0, 0), (0, width - v.shape[1])))


def _lane_bcast(v):
    return jnp.broadcast_to(v[:, None], (v.shape[0], LANES))


def kernel(x, c, positions, w_ada, b_ada, norm_ffn1, ffn1_wg, ffn1_wu, ffn1_wd, norm_mix, w_in, conv_w,
           conv_b, dt_bias, a_log, d_skip, ssd_norm_w, q_norm_w, k_norm_w, sink_logit, w_out, norm_ffn2,
           ffn2_wg, ffn2_wu, ffn2_wd):
    depth = w_ada.shape[0]
    b, s, d = x.shape
    h = x.astype(F32)
    c_pad = jnp.pad(c.astype(F32), ((0, -b % SUBLANES), (0, 0)))
    cs_tab = _rope_table(positions)
    for l in range(depth):
        mod = _adaln_mod(c_pad, w_ada[l], b_ada[l])[:b].reshape(b, N_MOD, d)
        conv_w8 = jnp.pad(conv_w[l], ((0, SUBLANES - CONV_K), (0, 0)))
        dtb, alog = _pad_lanes(dt_bias[l]), _pad_lanes(a_log[l])
        h1, z, q, kv, dt, bc, xdtb, ypart = _head(
            h, mod, norm_ffn1[l].reshape(1, d), norm_mix[l].reshape(1, d),
            ffn1_wg[l].astype(BF16), ffn1_wu[l].astype(BF16), ffn1_wd[l].astype(BF16), _pad_inproj(w_in[l]),
            conv_w8, conv_b[l].reshape(1, -1), dtb, alog, jnp.repeat(d_skip[l], SSD_HEAD_DIM).reshape(1, -1))
        h = _tail(h1, mod, norm_ffn2[l].reshape(1, d), w_out[l].astype(BF16),
                  ffn2_wg[l].astype(BF16), ffn2_wu[l].astype(BF16), ffn2_wd[l].astype(BF16),
                  q, kv, cs_tab, _lane_bcast(q_norm_w[l]), _lane_bcast(k_norm_w[l]), sink_logit[l],
                  bc, xdtb, ypart, dt, z, dtb, alog, ssd_norm_w[l].reshape(1, -1))
    return h.astype(x.dtype)
```

```python
import functools

import jax
import jax.numpy as jnp
import numpy as np
from jax import lax
from jax.experimental import pallas as pl
from jax.experimental.pallas import tpu as pltpu

F32 = jnp.float32
BF16 = jnp.bfloat16

SSD_HEAD_DIM = 64
SSD_HEADS = 8
SSD_GROUPS = 2
D_STATE = 128
CONV_K = 5
CHUNK = 128
HEAD_DIM = 64
ATTN_HEADS = 8
ATTN_KV_HEADS = 2
WINDOW = 128
BLOCK = 128
ROPE_DIMS = 16
ROPE_THETA = 500000.0
N_MOD = 9
EPS = 1e-6

SSD_WIDTH = SSD_HEADS * SSD_HEAD_DIM
BC_WIDTH = 2 * SSD_GROUPS * D_STATE
CONV_DIM = SSD_WIDTH + BC_WIDTH

LANES = 128
SUBLANES = 8
VMEM_LIMIT_BYTES = 56 * 1024 * 1024

FFN_ROWS = 512
FFN_CHUNK = 256
SEQ_ROWS = 512
CONV_HALO = 16

NEG_BIG = -1e30


def _dot(a, b):
    return jnp.dot(a, b, preferred_element_type=F32)


def _dot_nt(a, b):
    return lax.dot_general(a, b, (((1,), (1,)), ((), ())), preferred_element_type=F32)


def _split_dot(x, m_bf16, terms=3, lhs=True):
    acc = None
    r = x
    for t in range(terms):
        h = r.astype(BF16)
        d = _dot(h, m_bf16) if lhs else _dot(m_bf16, h)
        acc = d if acc is None else acc + d
        if t + 1 < terms:
            r = r - h.astype(F32)
    return acc


def _silu(x):
    return x * jax.nn.sigmoid(x)


def _iota2(shape, dim):
    return lax.broadcasted_iota(jnp.int32, shape, dim)


def _const_spec(shape):
    nd = len(shape)
    return pl.BlockSpec(shape, lambda *_: (0,) * nd, pipeline_mode=pl.Buffered(1))


def _mod_kernel(c_ref, w_ref, b_ref, o_ref):
    cs = _silu(c_ref[...])
    o_ref[...] = _dot(cs, w_ref[...]) + b_ref[...]


def _adaln_mod(c_pad, w_ada, b_ada):
    rows, d = c_pad.shape
    n = w_ada.shape[1]
    bn = d
    return pl.pallas_call(
        _mod_kernel,
        grid=(n // bn,),
        in_specs=[pl.BlockSpec((rows, d), lambda j: (0, 0)),
                  pl.BlockSpec((d, bn), lambda j: (0, j)),
                  pl.BlockSpec((1, bn), lambda j: (0, j))],
        out_specs=pl.BlockSpec((rows, bn), lambda j: (0, j)),
        out_shape=jax.ShapeDtypeStruct((rows, n), F32),
        name="adaln_mod",
    )(c_pad, w_ada, b_ada.reshape(1, n))


def _rope_kernel(pos_ref, inv_ref, o_ref):
    half = ROPE_DIMS // 2
    s = pos_ref.shape[-1]
    p = pos_ref[0].astype(F32)
    ang = jnp.tile(inv_ref[...], (1, s // LANES)) * p
    o_ref[0, 0:half, :] = jnp.cos(ang)
    o_ref[0, half:, :] = jnp.sin(ang)


def _rope_table(positions):
    half = ROPE_DIMS // 2
    inv = ROPE_THETA ** (-jnp.arange(half, dtype=F32) * 2.0 / ROPE_DIMS)
    b, s = positions.shape
    return pl.pallas_call(
        _rope_kernel,
        grid=(b,),
        in_specs=[pl.BlockSpec((1, 1, s), lambda bi: (bi, 0, 0)), _const_spec((half, LANES))],
        out_specs=pl.BlockSpec((1, 2 * half, s), lambda bi: (bi, 0, 0)),
        out_shape=jax.ShapeDtypeStruct((b, 2 * half, s), F32),
        name="rope_table",
    )(positions.reshape(b, 1, s), jnp.broadcast_to(inv[:, None], (half, LANES)))


def _ada_norm(x, gain_scale, shift):
    ms = jnp.mean(x * x, axis=-1, keepdims=True)
    return x * lax.rsqrt(ms + EPS) * gain_scale + shift


def _swiglu(ub, wg_ref, wu_ref, wd_ref, side_work=(), drain=True):
    d_ff = wg_ref.shape[1]
    acc = None
    for c0 in range(0, d_ff, FFN_CHUNK):
        c1 = min(c0 + FFN_CHUNK, d_ff)
        g = _dot(ub, wg_ref[:, c0:c1])
        up = _dot(ub, wu_ref[:, c0:c1])
        a = (_silu(g) * up).astype(BF16)
        d = _dot(a, wd_ref[c0:c1, :])
        acc = d if acc is None else acc + d
        for gen in side_work:
            next(gen, None)
    for gen in side_work if drain else ():
        for _ in gen:
            pass
    return acc


_INPROJ_LAYOUT = (("z", SSD_WIDTH, F32), ("xbc", CONV_DIM, BF16), ("q", ATTN_HEADS * HEAD_DIM, F32),
                  ("kv", 2 * ATTN_KV_HEADS * HEAD_DIM, F32), ("dt", LANES, F32))


def _ssd_constants():
    r = np.arange(CHUNK)[:, None]
    c = np.arange(CHUNK)[None, :]
    return jnp.asarray(c <= r, dtype=BF16), jnp.asarray(c >= r, dtype=BF16)


def _dt_and_da(dt_raw, dtb_ref, alog_ref):
    lane = _iota2(dt_raw.shape, 1)
    dt = jnp.where(lane < 2 * SSD_HEADS, jax.nn.softplus(dt_raw + dtb_ref[...]), 0.0)
    a = -jnp.exp(alog_ref[...])
    return dt, dt * a


def _head_columns(x, lane0):
    return [jnp.broadcast_to(x[:, lane0 + h:lane0 + h + 1], x.shape) for h in range(SSD_HEADS)]


def _expand_heads(cols):
    first = _iota2(cols[0].shape, 1) < SSD_HEAD_DIM
    return jnp.concatenate([jnp.where(first, cols[2 * p], cols[2 * p + 1]) for p in range(SSD_HEADS // 2)],
                           axis=1)


def _ssd_chunk(direction, da, tri_ref, cm, bm, xdt_fn, h_ref):
    gw = SSD_WIDTH // SSD_GROUPS
    lane0 = 0 if direction == "fwd" else SSD_HEADS
    tot_row = CHUNK - 1 if direction == "fwd" else 0
    cgs = [cm[:, g * D_STATE:(g + 1) * D_STATE] for g in range(SSD_GROUPS)]
    bgs = [bm[:, g * D_STATE:(g + 1) * D_STATE] for g in range(SSD_GROUPS)]
    hgs = [h_ref[:, g * gw:(g + 1) * gw] for g in range(SSD_GROUPS)]

    cs = _split_dot(da, tri_ref[...], lhs=False)
    cbs = [_dot_nt(cgs[g], bgs[g]) for g in range(SSD_GROUPS)]
    chs = [_dot(cgs[g], hgs[g].astype(BF16)) for g in range(SSD_GROUPS)]
    yield

    xdt = xdt_fn()
    cs_cols = _head_columns(cs, lane0)
    csx = _expand_heads(cs_cols)
    tot = csx[tot_row:tot_row + 1, :]
    e_in = jnp.exp(csx)
    xdec = (xdt * jnp.exp(tot - csx)).astype(BF16)
    cdec = jnp.exp(tot)
    cst = cs.T
    row = _iota2((CHUNK, CHUNK), 0)
    col = _iota2((CHUNK, CHUNK), 1)
    keep = (col <= row) if direction == "fwd" else (col > row)
    head_of_lane = _iota2((CHUNK, gw), 1) // SSD_HEAD_DIM
    xdt_b = xdt.astype(BF16)
    per_group = SSD_HEADS // SSD_GROUPS
    ys = []
    for g in range(SSD_GROUPS):
        xg = xdt_b[:, g * gw:(g + 1) * gw]
        ms, xs_masked = [], []
        for r in range(per_group):
            h = g * per_group + r
            seg = cs_cols[h] - cst[lane0 + h:lane0 + h + 1, :]
            ms.append((cbs[g] * jnp.exp(jnp.where(keep, seg, NEG_BIG))).astype(BF16))
            xs_masked.append(jnp.where(head_of_lane == r, xg, jnp.zeros_like(xg)))
        y_diag = _dot(jnp.concatenate(ms, axis=1), jnp.concatenate(xs_masked, axis=0))
        ys.append(y_diag + chs[g] * e_in[:, g * gw:(g + 1) * gw])
        bgt = bgs[g].astype(F32).T.astype(BF16)
        st = _dot(bgt, xdec[:, g * gw:(g + 1) * gw])
        h_ref[:, g * gw:(g + 1) * gw] = hgs[g] * cdec[:, g * gw:(g + 1) * gw] + st
    return jnp.concatenate(ys, axis=1)


def _ssd_fwd_tile(window_fn, dt_fn, cw_ref, cbias_ref, dtb_ref, alog_ref, dskip_ref,
                  tri_ref, bc_ref, xdtb_ref, yp_ref, h_ref):
    mid = CONV_K // 2
    for j in range(SEQ_ROWS // CHUNK):
        rows = pl.ds(j * CHUNK, CHUNK)
        win = window_fn(j)
        dt, da = _dt_and_da(dt_fn(j), dtb_ref, alog_ref)
        dtx_f = _expand_heads(_head_columns(dt, 0))
        dtx_b = _expand_heads(_head_columns(dt, SSD_HEADS))
        yield
        winf = win.astype(F32)
        acc = cbias_ref[...] + winf[CONV_HALO:CONV_HALO + CHUNK] * cw_ref[mid:mid + 1, :]
        for k in [k for k in range(CONV_K) if k != mid]:
            rolled = pltpu.roll(winf, (mid - k) % winf.shape[0], axis=0)
            acc = acc + rolled[CONV_HALO:CONV_HALO + CHUNK] * cw_ref[k:k + 1, :]
        act = _silu(acc)
        xs = act[:, :SSD_WIDTH]
        bc = act[:, SSD_WIDTH:].astype(BF16)
        xdtb_ref[0, rows, :] = (xs * dtx_b).astype(BF16)
        bc_ref[0, rows, :] = bc
        y = yield from _ssd_chunk("fwd", da, tri_ref, bc[:, SSD_GROUPS * D_STATE:],
                                  bc[:, :SSD_GROUPS * D_STATE], lambda: xs * dtx_f, h_ref)
        yp_ref[0, rows, :] = y + xs * dskip_ref[...]
        yield


def _ssd_bwd_tile(bc_ref, xdtb_ref, yp_ref, dt_ref, z_ref, dtb_ref, alog_ref, nw_ref, tri_ref,
                  h_ref, reset, store):
    h_ref[...] = jnp.where(reset, 0.0, h_ref[...])
    for j in reversed(range(bc_ref.shape[1] // CHUNK)):
        rows = pl.ds(j * CHUNK, CHUNK)
        bc = bc_ref[0, rows, :]
        _, da = _dt_and_da(dt_ref[0, rows, :], dtb_ref, alog_ref)
        y = yield from _ssd_chunk("bwd", da, tri_ref, bc[:, SSD_GROUPS * D_STATE:],
                                  bc[:, :SSD_GROUPS * D_STATE],
                                  lambda: xdtb_ref[0, rows, :].astype(F32), h_ref)
        y = (y + yp_ref[0, rows, :]) * _silu(z_ref[0, rows, :])
        ms = jnp.mean(y * y, axis=-1, keepdims=True)
        store(j * CHUNK, (y * lax.rsqrt(ms + EPS) * nw_ref[...]).astype(BF16))
        yield


RING = 3


def _head_kernel(n_tiles, tiles_per_seq,
                 x_ref, mod_ref, n1_ref, n2_ref, wg_ref, wu_ref, wd_ref, win_ref,
                 cw_ref, cbias_ref, dtb_ref, alog_ref, dskip_ref, tri_ref,
                 h_ref, z_ref, q_ref, kv_ref, dt_ref, bc_ref, xdtb_ref, yp_ref,
                 xstage, dtstage, xring, dtring, hstate_ref):
    k = pl.program_id(0)

    @pl.when(k == 0)
    def _():
        xstage[...] = jnp.zeros_like(xstage)
        dtstage[...] = jnp.zeros_like(dtstage)
        xring[...] = jnp.zeros_like(xring)
        dtring[...] = jnp.zeros_like(dtring)
        hstate_ref[...] = jnp.zeros_like(hstate_ref)

    t = jnp.maximum(k - 2, 0) % tiles_per_seq
    s_new = (k + RING - 1) % RING
    s_main = (k + RING - 2) % RING
    s_prev = k % RING
    has_prev = t > 0
    has_next = t < tiles_per_seq - 1

    def advance_ring():
        xring[s_new] = xstage[...]
        dtring[s_new] = dtstage[...]
        hstate_ref[...] = jnp.where(has_prev, hstate_ref[...], 0.0)

    def window(j):
        lo = j * CHUNK - CONV_HALO
        hi = (j + 1) * CHUNK + CONV_HALO
        parts = []
        if lo < 0:
            halo = xring[s_prev, SEQ_ROWS + lo:, :]
            parts.append(jnp.where(has_prev, halo, jnp.zeros_like(halo)))
        parts.append(xring[s_main, max(lo, 0):min(hi, SEQ_ROWS), :])
        if hi > SEQ_ROWS:
            halo = xring[s_new, 0:hi - SEQ_ROWS, :]
            parts.append(jnp.where(has_next, halo, jnp.zeros_like(halo)))
        return parts[0] if len(parts) == 1 else jnp.concatenate(parts, axis=0)

    def make_scan():
        return _ssd_fwd_tile(window, lambda j: dtring[s_main, j * CHUNK:(j + 1) * CHUNK, :],
                             cw_ref, cbias_ref, dtb_ref, alog_ref, dskip_ref, tri_ref,
                             bc_ref, xdtb_ref, yp_ref, hstate_ref)

    @pl.when(k < n_tiles)
    def _():
        advance_ring()
        scan = make_scan()
        x = x_ref[0]
        mod = mod_ref[0]
        u = _ada_norm(x, n1_ref[...] * (1.0 + mod[1:2]), mod[0:1]).astype(BF16)
        ff = _swiglu(u, wg_ref, wu_ref, wd_ref, side_work=(scan,), drain=False)
        h = x + (0.5 * (1.0 + mod[2:3])) * ff
        h_ref[0] = h
        u2 = _ada_norm(h, n2_ref[...] * (1.0 + mod[4:5]), mod[3:4]).astype(BF16)
        dests = {"z": (z_ref,), "xbc": (xstage,), "q": (q_ref,), "kv": (kv_ref,), "dt": (dt_ref, dtstage)}
        c0 = 0
        for name, w, _ in _INPROJ_LAYOUT:
            piece = _dot(u2, win_ref[:, c0:c0 + w])
            for ref in dests[name]:
                if len(ref.shape) == 3:
                    ref[0] = piece.astype(ref.dtype)
                else:
                    ref[...] = piece.astype(ref.dtype)
            c0 += w
            next(scan, None)
        for _ in scan:
            pass

    @pl.when(k >= n_tiles)
    def _():
        advance_ring()
        for _ in make_scan():
            pass


def _head(x, mod, norm1, norm2, wg, wu, wd, w_in_p, conv_w8, conv_b, dtb, alog, dskip):
    b, s, d = x.shape
    d_ff = wg.shape[1]
    tps = s // SEQ_ROWS
    n_tiles = b * tps
    tri_lo, _ = _ssd_constants()
    ffn_tile = lambda k: jnp.minimum(k, n_tiles - 1)
    scan_tile = lambda k: jnp.maximum(k - 2, 0)

    def ffn_tok(w):
        return pl.BlockSpec((1, FFN_ROWS, w), lambda k: (ffn_tile(k) // tps, ffn_tile(k) % tps, 0))

    def scan_tok(w):
        return pl.BlockSpec((1, SEQ_ROWS, w), lambda k: (scan_tile(k) // tps, scan_tile(k) % tps, 0))

    widths = {name: (w, dt) for name, w, dt in _INPROJ_LAYOUT}
    hbm_outs = ("z", "q", "kv", "dt")
    out_shape = [jax.ShapeDtypeStruct((b, s, d), F32)]
    out_shape += [jax.ShapeDtypeStruct((b, s, widths[n][0]), widths[n][1]) for n in hbm_outs]
    out_shape += [jax.ShapeDtypeStruct((b, s, BC_WIDTH), BF16), jax.ShapeDtypeStruct((b, s, SSD_WIDTH), BF16),
                  jax.ShapeDtypeStruct((b, s, SSD_WIDTH), F32)]
    out_specs = [ffn_tok(d)] + [ffn_tok(widths[n][0]) for n in hbm_outs]
    out_specs += [scan_tok(BC_WIDTH), scan_tok(SSD_WIDTH), scan_tok(SSD_WIDTH)]
    return pl.pallas_call(
        functools.partial(_head_kernel, n_tiles, tps),
        grid=(n_tiles + 2,),
        in_specs=[ffn_tok(d),
                  pl.BlockSpec((1, N_MOD, d), lambda k: (ffn_tile(k) // tps, 0, 0)),
                  _const_spec((1, d)), _const_spec((1, d)),
                  _const_spec((d, d_ff)), _const_spec((d, d_ff)), _const_spec((d_ff, d)),
                  _const_spec(w_in_p.shape),
                  _const_spec(conv_w8.shape), _const_spec((1, CONV_DIM)),
                  _const_spec((1, LANES)), _const_spec((1, LANES)), _const_spec((1, SSD_WIDTH)),
                  _const_spec(tri_lo.shape)],
        out_specs=out_specs,
        out_shape=out_shape,
        scratch_shapes=[pltpu.VMEM((SEQ_ROWS, CONV_DIM), BF16), pltpu.VMEM((SEQ_ROWS, LANES), F32),
                        pltpu.VMEM((RING, SEQ_ROWS, CONV_DIM), BF16), pltpu.VMEM((RING, SEQ_ROWS, LANES), F32),
                        pltpu.VMEM((D_STATE, SSD_WIDTH), F32)],
        compiler_params=pltpu.CompilerParams(dimension_semantics=("arbitrary",),
                                             vmem_limit_bytes=VMEM_LIMIT_BYTES),
        name="ffn1_inproj_scan",
    )(x, mod, norm1, norm2, wg, wu, wd, w_in_p, conv_w8, conv_b, dtb, alog, dskip, tri_lo)


def _head_prep_t(t, w_b, cos, sin):
    ms = jnp.sum(t * t, axis=0, keepdims=True) * (1.0 / HEAD_DIM)
    tn = t * lax.rsqrt(ms + EPS) * w_b
    half = ROPE_DIMS // 2
    t1 = tn[0:half]
    t2 = tn[half:ROPE_DIMS]
    return jnp.concatenate([t1 * cos - t2 * sin, t2 * cos + t1 * sin, tn[ROPE_DIMS:]], axis=0)


def _attn_tile(q_ref, kvp_ref, kvc_ref, kvn_ref, csp_ref, csc_ref, csn_ref, qw_ref, kw_ref, sink_ref,
               has_prev, has_next, store):
    tq = q_ref.shape[1]
    nsub = tq // BLOCK
    nw = tq + 2 * BLOCK
    kvw = ATTN_KV_HEADS * HEAD_DIM
    half = ROPE_DIMS // 2
    q_per_kv = ATTN_HEADS // ATTN_KV_HEADS

    kv = jnp.concatenate([kvp_ref[0], kvc_ref[0], kvn_ref[0]], axis=0)
    cs = jnp.concatenate([csp_ref[0], csc_ref[0], csn_ref[0]], axis=1)
    k_t = kv[:, :kvw].T
    v_t = kv[:, kvw:].T.astype(BF16)
    kw_b = jnp.tile(kw_ref[...], (1, nw // LANES))
    k_prep = jnp.concatenate(
        [_head_prep_t(k_t[g * HEAD_DIM:(g + 1) * HEAD_DIM], kw_b, cs[0:half], cs[half:])
         for g in range(ATTN_KV_HEADS)], axis=0)
    keys = k_prep.T.astype(BF16)

    q_t = q_ref[0].T
    qw_b = jnp.tile(qw_ref[...] * (HEAD_DIM ** -0.5), (1, tq // LANES))
    cs_q = csc_ref[0]
    zeros = jnp.zeros((HEAD_DIM, tq), BF16)
    q_heads = []
    for h in range(ATTN_HEADS):
        qh = _head_prep_t(q_t[h * HEAD_DIM:(h + 1) * HEAD_DIM], qw_b, cs_q[0:half], cs_q[half:]).astype(BF16)
        q_heads.append(jnp.concatenate([qh, zeros] if h // q_per_kv == 0 else [zeros, qh], axis=0))

    yield
    kr = _iota2((BLOCK, BLOCK), 0)
    qc = _iota2((BLOCK, BLOCK), 1)
    bias_prev = jnp.where(kr >= qc, 0.0, NEG_BIG)
    bias_next = jnp.where(kr <= qc, 0.0, NEG_BIG)
    sink_row = jnp.concatenate([jnp.full((1, BLOCK), sink_ref[h], F32) for h in range(ATTN_HEADS)], axis=1)
    gq = q_per_kv * BLOCK
    for j in range(nsub):
        bp = bias_prev if j > 0 else jnp.where(has_prev, bias_prev, NEG_BIG)
        bn = bias_next if j < nsub - 1 else jnp.where(has_next, bias_next, NEG_BIG)
        lo, hi = j * BLOCK, (j + 3) * BLOCK
        q_all = jnp.concatenate([qh[:, lo:lo + BLOCK] for qh in q_heads], axis=1)
        s = _dot(keys[lo:hi], q_all)
        yield
        s0 = s[0:BLOCK] + jnp.tile(bp, (1, ATTN_HEADS))
        s1 = s[BLOCK:2 * BLOCK]
        s2 = s[2 * BLOCK:] + jnp.tile(bn, (1, ATTN_HEADS))
        m = jnp.maximum(jnp.max(jnp.maximum(jnp.maximum(s0, s1), s2), axis=0, keepdims=True), sink_row)
        p0 = jnp.exp(s0 - m)
        p1 = jnp.exp(s1 - m)
        p2 = jnp.exp(s2 - m)
        denom = jnp.sum(p0 + p1 + p2, axis=0, keepdims=True) + jnp.exp(sink_row - m)
        p = jnp.concatenate([p0, p1, p2], axis=0).astype(BF16)
        inv = 1.0 / denom
        outs = []
        for g in range(ATTN_KV_HEADS):
            o = _dot(v_t[g * HEAD_DIM:(g + 1) * HEAD_DIM, lo:hi], p[:, g * gq:(g + 1) * gq])
            o = o * inv[:, g * gq:(g + 1) * gq]
            outs += [o[:, k * BLOCK:(k + 1) * BLOCK] for k in range(q_per_kv)]
        store(lo, jnp.concatenate(outs, axis=0).T.astype(BF16))
        yield


def _tail_kernel(n_tiles, tiles_per_seq,
                 h_ref, mod_ref, n3_ref, wo_ref, wg_ref, wu_ref, wd_ref,
                 q_ref, kvp_ref, kvc_ref, kvn_ref, csp_ref, csc_ref, csn_ref, qw_ref, kw_ref, sink_ref,
                 bc_ref, xdtb_ref, yp_ref, dt_ref, z_ref, dtb_ref, alog_ref, nw_ref, tri_ref,
                 o_ref, y_scr, hstate_ref):
    k = pl.program_id(0)

    @pl.when(k == 0)
    def _():
        hstate_ref[...] = jnp.zeros_like(hstate_ref)

    rd = k % 2
    wr = 1 - rd

    t = jnp.maximum(n_tiles - 1 - k, 0) % tiles_per_seq
    is_first = t == 0
    is_last = t == tiles_per_seq - 1

    def store_ssd(row0, y):
        y_scr[wr, row0:row0 + CHUNK, 0:SSD_WIDTH] = y

    def store_attn(row0, y):
        y_scr[wr, row0:row0 + BLOCK, SSD_WIDTH:] = y

    def make_mixers():
        return (
            _attn_tile(q_ref, kvp_ref, kvc_ref, kvn_ref, csp_ref, csc_ref, csn_ref, qw_ref, kw_ref, sink_ref,
                       jnp.logical_not(is_first), jnp.logical_not(is_last), store_attn),
            _ssd_bwd_tile(bc_ref, xdtb_ref, yp_ref, dt_ref, z_ref, dtb_ref, alog_ref, nw_ref, tri_ref,
                          hstate_ref, is_last, store_ssd))

    @pl.when(k == 0)
    def _():
        for gen in make_mixers():
            for _ in gen:
                pass

    @pl.when(k > 0)
    def _():
        attn, scan = make_mixers()
        h1 = h_ref[0]
        mod = mod_ref[0]
        mix = _dot(y_scr[rd], wo_ref[...])
        next(scan, None)
        h2 = h1 + (1.0 + mod[5:6]) * mix
        u = _ada_norm(h2, n3_ref[...] * (1.0 + mod[7:8]), mod[6:7]).astype(BF16)
        ff = _swiglu(u, wg_ref, wu_ref, wd_ref, side_work=(attn, scan))
        o_ref[0] = h2 + (0.5 * (1.0 + mod[8:9])) * ff


def _tail(h1, mod, norm3, w_out, wg, wu, wd, q, kv, cs_tab, qw_b, kw_b, sink,
          bc, xdtb, ypart, dt, z, dtb, alog, norm_w):
    b, s, d = h1.shape
    d_ff = wg.shape[1]
    tps = s // SEQ_ROWS
    n_tiles = b * tps
    nsub = SEQ_ROWS // BLOCK
    nb = s // BLOCK
    aw = q.shape[-1]
    kvw2 = kv.shape[-1]
    cs_rows = cs_tab.shape[1]
    _, tri_up = _ssd_constants()

    ffn_tile = lambda k: jnp.clip(n_tiles - k, 0, n_tiles - 1)
    mix_tile = lambda k: jnp.maximum(n_tiles - 1 - k, 0)
    prev_blk = lambda t: jnp.maximum(t * nsub - 1, 0)
    next_blk = lambda t: jnp.minimum((t + 1) * nsub, nb - 1)

    def ffn_tok(w):
        return pl.BlockSpec((1, FFN_ROWS, w), lambda k: (ffn_tile(k) // tps, ffn_tile(k) % tps, 0))

    def mix_tok(w):
        return pl.BlockSpec((1, SEQ_ROWS, w), lambda k: (mix_tile(k) // tps, mix_tile(k) % tps, 0))

    in_specs = [
        ffn_tok(d),
        pl.BlockSpec((1, N_MOD, d), lambda k: (ffn_tile(k) // tps, 0, 0)),
        _const_spec((1, d)), _const_spec(w_out.shape),
        _const_spec((d, d_ff)), _const_spec((d, d_ff)), _const_spec((d_ff, d)),
        mix_tok(aw),
        pl.BlockSpec((1, BLOCK, kvw2), lambda k: (mix_tile(k) // tps, prev_blk(mix_tile(k) % tps), 0)),
        mix_tok(kvw2),
        pl.BlockSpec((1, BLOCK, kvw2), lambda k: (mix_tile(k) // tps, next_blk(mix_tile(k) % tps), 0)),
        pl.BlockSpec((1, cs_rows, BLOCK), lambda k: (mix_tile(k) // tps, 0, prev_blk(mix_tile(k) % tps))),
        pl.BlockSpec((1, cs_rows, SEQ_ROWS), lambda k: (mix_tile(k) // tps, 0, mix_tile(k) % tps)),
        pl.BlockSpec((1, cs_rows, BLOCK), lambda k: (mix_tile(k) // tps, 0, next_blk(mix_tile(k) % tps))),
        _const_spec((HEAD_DIM, LANES)), _const_spec((HEAD_DIM, LANES)),
        pl.BlockSpec(memory_space=pltpu.SMEM),
        mix_tok(BC_WIDTH), mix_tok(SSD_WIDTH), mix_tok(SSD_WIDTH), mix_tok(LANES), mix_tok(SSD_WIDTH),
        _const_spec((1, LANES)), _const_spec((1, LANES)), _const_spec((1, SSD_WIDTH)),
        _const_spec(tri_up.shape),
    ]
    return pl.pallas_call(
        functools.partial(_tail_kernel, n_tiles, tps),
        grid=(n_tiles + 1,),
        in_specs=in_specs,
        out_specs=ffn_tok(d),
        out_shape=jax.ShapeDtypeStruct((b, s, d), F32),
        scratch_shapes=[pltpu.VMEM((2, SEQ_ROWS, SSD_WIDTH + aw), BF16),
                        pltpu.VMEM((D_STATE, SSD_WIDTH), F32)],
        compiler_params=pltpu.CompilerParams(dimension_semantics=("arbitrary",),
                                             vmem_limit_bytes=VMEM_LIMIT_BYTES),
        name="mixers_outproj_ffn2",
    )(h1, mod, norm3, w_out, wg, wu, wd, q, kv, kv, kv, cs_tab, cs_tab, cs_tab, qw_b, kw_b, sink,
      bc, xdtb, ypart, dt, z, dtb, alog, norm_w, tri_up)


def _pad_inproj(w_in):
    n_dt = 2 * SSD_HEADS
    s_xbc = SSD_WIDTH + CONV_DIM
    s_dt = s_xbc + n_dt
    dt_cols = jnp.pad(w_in[:, s_xbc:s_dt], ((0, 0), (0, LANES - n_dt)))
    return jnp.concatenate([w_in[:, :s_xbc], w_in[:, s_dt:], dt_cols], axis=1).astype(BF16)


def _pad_lanes(v, width=LANES):
    v = v.reshape(1, -1)
    return jnp.pad(v, ((0, 0), (0, width - v.shape[1])))


def _lane_bcast(v):
    return jnp.broadcast_to(v[:, None], (v.shape[0], LANES))


def kernel(x, c, positions, w_ada, b_ada, norm_ffn1, ffn1_wg, ffn1_wu, ffn1_wd, norm_mix, w_in, conv_w,
           conv_b, dt_bias, a_log, d_skip, ssd_norm_w, q_norm_w, k_norm_w, sink_logit, w_out, norm_ffn2,
           ffn2_wg, ffn2_wu, ffn2_wd):
    depth = w_ada.shape[0]
    b, s, d = x.shape
    h = x.astype(F32)
    c_pad = jnp.pad(c.astype(F32), ((0, -b % SUBLANES), (0, 0)))
    cs_tab = _rope_table(positions)
    for l in range(depth):
        mod = _adaln_mod(c_pad, w_ada[l], b_ada[l])[:b].reshape(b, N_MOD, d)
        conv_w8 = jnp.pad(conv_w[l], ((0, SUBLANES - CONV_K), (0, 0)))
        dtb, alog = _pad_lanes(dt_bias[l]), _pad_lanes(a_log[l])
        h1, z, q, kv, dt, bc, xdtb, ypart = _head(
            h, mod, norm_ffn1[l].reshape(1, d), norm_mix[l].reshape(1, d),
            ffn1_wg[l].astype(BF16), ffn1_wu[l].astype(BF16), ffn1_wd[l].astype(BF16), _pad_inproj(w_in[l]),
            conv_w8, conv_b[l].reshape(1, -1), dtb, alog, jnp.repeat(d_skip[l], SSD_HEAD_DIM).reshape(1, -1))
        h = _tail(h1, mod, norm_ffn2[l].reshape(1, d), w_out[l].astype(BF16),
                  ffn2_wg[l].astype(BF16), ffn2_wu[l].astype(BF16), ffn2_wd[l].astype(BF16),
                  q, kv, cs_tab, _lane_bcast(q_norm_w[l]), _lane_bcast(k_norm_w[l]), sink_logit[l],
                  bc, xdtb, ypart, dt, z, dtb, alog, ssd_norm_w[l].reshape(1, -1))
    return h.astype(x.dtype)
```

```python
import functools

import jax
import jax.numpy as jnp
import numpy as np
from jax import lax
from jax.experimental import pallas as pl
from jax.experimental.pallas import tpu as pltpu

F32 = jnp.float32
BF16 = jnp.bfloat16

SSD_HEAD_DIM = 64
SSD_HEADS = 8
SSD_GROUPS = 2
D_STATE = 128
CONV_K = 5
CHUNK = 128
HEAD_DIM = 64
ATTN_HEADS = 8
ATTN_KV_HEADS = 2
WINDOW = 128
BLOCK = 128
ROPE_DIMS = 16
ROPE_THETA = 500000.0
N_MOD = 9
EPS = 1e-6

SSD_WIDTH = SSD_HEADS * SSD_HEAD_DIM
BC_WIDTH = 2 * SSD_GROUPS * D_STATE
CONV_DIM = SSD_WIDTH + BC_WIDTH

LANES = 128
SUBLANES = 8
VMEM_LIMIT_BYTES = 56 * 1024 * 1024

FFN_ROWS = 512
FFN_CHUNK = 256
SEQ_ROWS = 512
CONV_HALO = 16

NEG_BIG = -1e30


def _dot(a, b):
    return jnp.dot(a, b, preferred_element_type=F32)


def _dot_nt(a, b):
    return lax.dot_general(a, b, (((1,), (1,)), ((), ())), preferred_element_type=F32)


def _row_sums(tri_bf16, x, terms=3):
    acc = None
    r = x
    for t in range(terms):
        h = r.astype(BF16)
        d = _dot(tri_bf16, h)
        acc = d if acc is None else acc + d
        if t + 1 < terms:
            r = r - h.astype(F32)
    return acc


def _silu(x):
    return x * jax.nn.sigmoid(x)


def _iota2(shape, dim):
    return lax.broadcasted_iota(jnp.int32, shape, dim)


def _const_spec(shape):
    nd = len(shape)
    return pl.BlockSpec(shape, lambda *_: (0,) * nd, pipeline_mode=pl.Buffered(1))


def _mod_kernel(c_ref, w_ref, b_ref, o_ref):
    cs = _silu(c_ref[...])
    o_ref[...] = _dot(cs, w_ref[...]) + b_ref[...]


def _adaln_mod(c_pad, w_ada, b_ada):
    rows, d = c_pad.shape
    n = w_ada.shape[1]
    bn = d
    return pl.pallas_call(
        _mod_kernel,
        grid=(n // bn,),
        in_specs=[pl.BlockSpec((rows, d), lambda j: (0, 0)),
                  pl.BlockSpec((d, bn), lambda j: (0, j)),
                  pl.BlockSpec((1, bn), lambda j: (0, j))],
        out_specs=pl.BlockSpec((rows, bn), lambda j: (0, j)),
        out_shape=jax.ShapeDtypeStruct((rows, n), F32),
        name="adaln_mod",
    )(c_pad, w_ada, b_ada.reshape(1, n))


def _rope_kernel(pos_ref, inv_ref, o_ref):
    half = ROPE_DIMS // 2
    s = pos_ref.shape[-1]
    p = pos_ref[0].astype(F32)
    ang = jnp.tile(inv_ref[...], (1, s // LANES)) * p
    o_ref[0, 0:half, :] = jnp.cos(ang)
    o_ref[0, half:, :] = jnp.sin(ang)


def _rope_table(positions):
    half = ROPE_DIMS // 2
    inv = ROPE_THETA ** (-jnp.arange(half, dtype=F32) * 2.0 / ROPE_DIMS)
    b, s = positions.shape
    return pl.pallas_call(
        _rope_kernel,
        grid=(b,),
        in_specs=[pl.BlockSpec((1, 1, s), lambda bi: (bi, 0, 0)), _const_spec((half, LANES))],
        out_specs=pl.BlockSpec((1, 2 * half, s), lambda bi: (bi, 0, 0)),
        out_shape=jax.ShapeDtypeStruct((b, 2 * half, s), F32),
        name="rope_table",
    )(positions.reshape(b, 1, s), jnp.broadcast_to(inv[:, None], (half, LANES)))


def _ada_norm(x, gain_scale, shift):
    ms = jnp.mean(x * x, axis=-1, keepdims=True)
    return x * lax.rsqrt(ms + EPS) * gain_scale + shift


def _swiglu(ub, wg_ref, wu_ref, wd_ref, side_work=(), drain=True):
    d_ff = wg_ref.shape[1]
    acc = None
    for c0 in range(0, d_ff, FFN_CHUNK):
        c1 = min(c0 + FFN_CHUNK, d_ff)
        g = _dot(ub, wg_ref[:, c0:c1])
        up = _dot(ub, wu_ref[:, c0:c1])
        a = (_silu(g) * up).astype(BF16)
        d = _dot(a, wd_ref[c0:c1, :])
        acc = d if acc is None else acc + d
        for gen in side_work:
            next(gen, None)
    for gen in side_work if drain else ():
        for _ in gen:
            pass
    return acc


_INPROJ_LAYOUT = (("z", SSD_WIDTH, F32), ("xbc", CONV_DIM, BF16), ("q", ATTN_HEADS * HEAD_DIM, F32),
                  ("kv", 2 * ATTN_KV_HEADS * HEAD_DIM, F32), ("dt", LANES, F32))


def _ssd_constants():
    r = np.arange(CHUNK)[:, None]
    c = np.arange(CHUNK)[None, :]
    return jnp.asarray(c <= r, dtype=BF16), jnp.asarray(c >= r, dtype=BF16)


def _dt_and_da(dt_raw, dtb_ref, alog_ref):
    lane = _iota2(dt_raw.shape, 1)
    dt = jnp.where(lane < 2 * SSD_HEADS, jax.nn.softplus(dt_raw + dtb_ref[...]), 0.0)
    a = -jnp.exp(alog_ref[...])
    return dt, dt * a


def _head_columns(x, lane0):
    return [jnp.broadcast_to(x[:, lane0 + h:lane0 + h + 1], x.shape) for h in range(SSD_HEADS)]


def _expand_heads(cols):
    first = _iota2(cols[0].shape, 1) < SSD_HEAD_DIM
    return jnp.concatenate([jnp.where(first, cols[2 * p], cols[2 * p + 1]) for p in range(SSD_HEADS // 2)],
                           axis=1)


def _ssd_chunk(direction, da, tri_ref, cm, bm, xdt_fn, h_ref):
    gw = SSD_WIDTH // SSD_GROUPS
    lane0 = 0 if direction == "fwd" else SSD_HEADS
    tot_row = CHUNK - 1 if direction == "fwd" else 0
    cgs = [cm[:, g * D_STATE:(g + 1) * D_STATE] for g in range(SSD_GROUPS)]
    bgs = [bm[:, g * D_STATE:(g + 1) * D_STATE] for g in range(SSD_GROUPS)]
    hgs = [h_ref[:, g * gw:(g + 1) * gw] for g in range(SSD_GROUPS)]

    cs = _row_sums(tri_ref[...], da)
    cbs = [_dot_nt(cgs[g], bgs[g]) for g in range(SSD_GROUPS)]
    chs = [_dot(cgs[g], hgs[g].astype(BF16)) for g in range(SSD_GROUPS)]
    yield

    xdt = xdt_fn()
    cs_cols = _head_columns(cs, lane0)
    csx = _expand_heads(cs_cols)
    tot = csx[tot_row:tot_row + 1, :]
    e_in = jnp.exp(csx)
    xdec = (xdt * jnp.exp(tot - csx)).astype(BF16)
    cdec = jnp.exp(tot)
    cst = cs.T
    row = _iota2((CHUNK, CHUNK), 0)
    col = _iota2((CHUNK, CHUNK), 1)
    keep = (col <= row) if direction == "fwd" else (col > row)
    head_of_lane = _iota2((CHUNK, gw), 1) // SSD_HEAD_DIM
    xdt_b = xdt.astype(BF16)
    per_group = SSD_HEADS // SSD_GROUPS
    ys = []
    for g in range(SSD_GROUPS):
        xg = xdt_b[:, g * gw:(g + 1) * gw]
        ms, xs_masked = [], []
        for r in range(per_group):
            h = g * per_group + r
            seg = cs_cols[h] - cst[lane0 + h:lane0 + h + 1, :]
            ms.append((cbs[g] * jnp.exp(jnp.where(keep, seg, NEG_BIG))).astype(BF16))
            xs_masked.append(jnp.where(head_of_lane == r, xg, jnp.zeros_like(xg)))
        y_diag = _dot(jnp.concatenate(ms, axis=1), jnp.concatenate(xs_masked, axis=0))
        ys.append(y_diag + chs[g] * e_in[:, g * gw:(g + 1) * gw])
        bgt = bgs[g].astype(F32).T.astype(BF16)
        st = _dot(bgt, xdec[:, g * gw:(g + 1) * gw])
        h_ref[:, g * gw:(g + 1) * gw] = hgs[g] * cdec[:, g * gw:(g + 1) * gw] + st
    return jnp.concatenate(ys, axis=1)


def _ssd_fwd_tile(window_fn, dt_fn, cw_ref, cbias_ref, dtb_ref, alog_ref, dskip_ref,
                  tri_ref, bc_ref, xdtb_ref, yp_ref, h_ref):
    mid = CONV_K // 2
    for j in range(SEQ_ROWS // CHUNK):
        rows = pl.ds(j * CHUNK, CHUNK)
        win = window_fn(j)
        dt, da = _dt_and_da(dt_fn(j), dtb_ref, alog_ref)
        dtx_f = _expand_heads(_head_columns(dt, 0))
        dtx_b = _expand_heads(_head_columns(dt, SSD_HEADS))
        yield
        winf = win.astype(F32)
        acc = cbias_ref[...] + winf[CONV_HALO:CONV_HALO + CHUNK] * cw_ref[mid:mid + 1, :]
        for k in [k for k in range(CONV_K) if k != mid]:
            rolled = pltpu.roll(winf, (mid - k) % winf.shape[0], axis=0)
            acc = acc + rolled[CONV_HALO:CONV_HALO + CHUNK] * cw_ref[k:k + 1, :]
        act = _silu(acc)
        xs = act[:, :SSD_WIDTH]
        bc = act[:, SSD_WIDTH:].astype(BF16)
        xdtb_ref[0, rows, :] = (xs * dtx_b).astype(BF16)
        bc_ref[0, rows, :] = bc
        y = yield from _ssd_chunk("fwd", da, tri_ref, bc[:, SSD_GROUPS * D_STATE:],
                                  bc[:, :SSD_GROUPS * D_STATE], lambda: xs * dtx_f, h_ref)
        yp_ref[0, rows, :] = y + xs * dskip_ref[...]
        yield


def _ssd_bwd_tile(bc_ref, xdtb_ref, yp_ref, dt_ref, z_ref, dtb_ref, alog_ref, nw_ref, tri_ref,
                  h_ref, reset, store):
    h_ref[...] = jnp.where(reset, 0.0, h_ref[...])
    for j in reversed(range(bc_ref.shape[1] // CHUNK)):
        rows = pl.ds(j * CHUNK, CHUNK)
        bc = bc_ref[0, rows, :]
        _, da = _dt_and_da(dt_ref[0, rows, :], dtb_ref, alog_ref)
        y = yield from _ssd_chunk("bwd", da, tri_ref, bc[:, SSD_GROUPS * D_STATE:],
                                  bc[:, :SSD_GROUPS * D_STATE],
                                  lambda: xdtb_ref[0, rows, :].astype(F32), h_ref)
        y = (y + yp_ref[0, rows, :]) * _silu(z_ref[0, rows, :])
        ms = jnp.mean(y * y, axis=-1, keepdims=True)
        store(j * CHUNK, (y * lax.rsqrt(ms + EPS) * nw_ref[...]).astype(BF16))
        yield


RING = 3


def _head_kernel(n_tiles, tiles_per_seq,
                 x_ref, mod_ref, n1_ref, n2_ref, wg_ref, wu_ref, wd_ref, win_ref,
                 cw_ref, cbias_ref, dtb_ref, alog_ref, dskip_ref, tri_ref,
                 h_ref, z_ref, q_ref, kv_ref, dt_ref, bc_ref, xdtb_ref, yp_ref,
                 xstage, dtstage, xring, dtring, hstate_ref):
    k = pl.program_id(0)

    @pl.when(k == 0)
    def _():
        xstage[...] = jnp.zeros_like(xstage)
        dtstage[...] = jnp.zeros_like(dtstage)
        xring[...] = jnp.zeros_like(xring)
        dtring[...] = jnp.zeros_like(dtring)
        hstate_ref[...] = jnp.zeros_like(hstate_ref)

    t = jnp.maximum(k - 2, 0) % tiles_per_seq
    s_new = (k + RING - 1) % RING
    s_main = (k + RING - 2) % RING
    s_prev = k % RING
    has_prev = t > 0
    has_next = t < tiles_per_seq - 1

    def advance_ring():
        xring[s_new] = xstage[...]
        dtring[s_new] = dtstage[...]
        hstate_ref[...] = jnp.where(has_prev, hstate_ref[...], 0.0)

    def window(j):
        lo = j * CHUNK - CONV_HALO
        hi = (j + 1) * CHUNK + CONV_HALO
        parts = []
        if lo < 0:
            halo = xring[s_prev, SEQ_ROWS + lo:, :]
            parts.append(jnp.where(has_prev, halo, jnp.zeros_like(halo)))
        parts.append(xring[s_main, max(lo, 0):min(hi, SEQ_ROWS), :])
        if hi > SEQ_ROWS:
            halo = xring[s_new, 0:hi - SEQ_ROWS, :]
            parts.append(jnp.where(has_next, halo, jnp.zeros_like(halo)))
        return parts[0] if len(parts) == 1 else jnp.concatenate(parts, axis=0)

    advance_ring()
    scan = _ssd_fwd_tile(window, lambda j: dtring[s_main, j * CHUNK:(j + 1) * CHUNK, :],
                         cw_ref, cbias_ref, dtb_ref, alog_ref, dskip_ref, tri_ref,
                         bc_ref, xdtb_ref, yp_ref, hstate_ref)

    x = x_ref[0]
    mod = mod_ref[0]
    u = _ada_norm(x, n1_ref[...] * (1.0 + mod[1:2]), mod[0:1]).astype(BF16)
    ff = _swiglu(u, wg_ref, wu_ref, wd_ref, side_work=(scan,), drain=False)
    h = x + (0.5 * (1.0 + mod[2:3])) * ff
    h_ref[0] = h
    u2 = _ada_norm(h, n2_ref[...] * (1.0 + mod[4:5]), mod[3:4]).astype(BF16)
    dests = {"z": (z_ref,), "xbc": (xstage,), "q": (q_ref,), "kv": (kv_ref,), "dt": (dt_ref, dtstage)}
    c0 = 0
    for name, w, _ in _INPROJ_LAYOUT:
        piece = _dot(u2, win_ref[:, c0:c0 + w])
        for ref in dests[name]:
            if len(ref.shape) == 3:
                ref[0] = piece.astype(ref.dtype)
            else:
                ref[...] = piece.astype(ref.dtype)
        c0 += w
        next(scan, None)
    for _ in scan:
        pass


def _head(x, mod, norm1, norm2, wg, wu, wd, w_in_p, conv_w8, conv_b, dtb, alog, dskip):
    b, s, d = x.shape
    d_ff = wg.shape[1]
    tps = s // SEQ_ROWS
    n_tiles = b * tps
    tri_lo, _ = _ssd_constants()
    ffn_tile = lambda k: jnp.minimum(k, n_tiles - 1)
    scan_tile = lambda k: jnp.maximum(k - 2, 0)

    def ffn_tok(w):
        return pl.BlockSpec((1, FFN_ROWS, w), lambda k: (ffn_tile(k) // tps, ffn_tile(k) % tps, 0))

    def scan_tok(w):
        return pl.BlockSpec((1, SEQ_ROWS, w), lambda k: (scan_tile(k) // tps, scan_tile(k) % tps, 0))

    widths = {name: (w, dt) for name, w, dt in _INPROJ_LAYOUT}
    hbm_outs = ("z", "q", "kv", "dt")
    out_shape = [jax.ShapeDtypeStruct((b, s, d), F32)]
    out_shape += [jax.ShapeDtypeStruct((b, s, widths[n][0]), widths[n][1]) for n in hbm_outs]
    out_shape += [jax.ShapeDtypeStruct((b, s, BC_WIDTH), BF16), jax.ShapeDtypeStruct((b, s, SSD_WIDTH), BF16),
                  jax.ShapeDtypeStruct((b, s, SSD_WIDTH), F32)]
    out_specs = [ffn_tok(d)] + [ffn_tok(widths[n][0]) for n in hbm_outs]
    out_specs += [scan_tok(BC_WIDTH), scan_tok(SSD_WIDTH), scan_tok(SSD_WIDTH)]
    return pl.pallas_call(
        functools.partial(_head_kernel, n_tiles, tps),
        grid=(n_tiles + 2,),
        in_specs=[ffn_tok(d),
                  pl.BlockSpec((1, N_MOD, d), lambda k: (ffn_tile(k) // tps, 0, 0)),
                  _const_spec((1, d)), _const_spec((1, d)),
                  _const_spec((d, d_ff)), _const_spec((d, d_ff)), _const_spec((d_ff, d)),
                  _const_spec(w_in_p.shape),
                  _const_spec(conv_w8.shape), _const_spec((1, CONV_DIM)),
                  _const_spec((1, LANES)), _const_spec((1, LANES)), _const_spec((1, SSD_WIDTH)),
                  _const_spec(tri_lo.shape)],
        out_specs=out_specs,
        out_shape=out_shape,
        scratch_shapes=[pltpu.VMEM((SEQ_ROWS, CONV_DIM), BF16), pltpu.VMEM((SEQ_ROWS, LANES), F32),
                        pltpu.VMEM((RING, SEQ_ROWS, CONV_DIM), BF16), pltpu.VMEM((RING, SEQ_ROWS, LANES), F32),
                        pltpu.VMEM((D_STATE, SSD_WIDTH), F32)],
        compiler_params=pltpu.CompilerParams(dimension_semantics=("arbitrary",),
                                             vmem_limit_bytes=VMEM_LIMIT_BYTES),
        name="ffn1_inproj_scan",
    )(x, mod, norm1, norm2, wg, wu, wd, w_in_p, conv_w8, conv_b, dtb, alog, dskip, tri_lo)


def _head_prep_t(t, w_b, cos, sin):
    ms = jnp.sum(t * t, axis=0, keepdims=True) * (1.0 / HEAD_DIM)
    tn = t * lax.rsqrt(ms + EPS) * w_b
    half = ROPE_DIMS // 2
    t1 = tn[0:half]
    t2 = tn[half:ROPE_DIMS]
    return jnp.concatenate([t1 * cos - t2 * sin, t2 * cos + t1 * sin, tn[ROPE_DIMS:]], axis=0)


def _attn_tile(q_ref, kvp_ref, kvc_ref, kvn_ref, csp_ref, csc_ref, csn_ref, qw_ref, kw_ref, sink_ref,
               has_prev, has_next, store):
    tq = q_ref.shape[1]
    nsub = tq // BLOCK
    nw = tq + 2 * BLOCK
    kvw = ATTN_KV_HEADS * HEAD_DIM
    half = ROPE_DIMS // 2
    q_per_kv = ATTN_HEADS // ATTN_KV_HEADS

    kv = jnp.concatenate([kvp_ref[0], kvc_ref[0], kvn_ref[0]], axis=0)
    cs = jnp.concatenate([csp_ref[0], csc_ref[0], csn_ref[0]], axis=1)
    k_t = kv[:, :kvw].T
    v_t = kv[:, kvw:].T.astype(BF16)
    kw_b = jnp.tile(kw_ref[...], (1, nw // LANES))
    k_prep = jnp.concatenate(
        [_head_prep_t(k_t[g * HEAD_DIM:(g + 1) * HEAD_DIM], kw_b, cs[0:half], cs[half:])
         for g in range(ATTN_KV_HEADS)], axis=0)
    keys = k_prep.T.astype(BF16)

    q_t = q_ref[0].T
    qw_b = jnp.tile(qw_ref[...] * (HEAD_DIM ** -0.5), (1, tq // LANES))
    cs_q = csc_ref[0]
    zeros = jnp.zeros((HEAD_DIM, tq), BF16)
    q_heads = []
    for h in range(ATTN_HEADS):
        qh = _head_prep_t(q_t[h * HEAD_DIM:(h + 1) * HEAD_DIM], qw_b, cs_q[0:half], cs_q[half:]).astype(BF16)
        q_heads.append(jnp.concatenate([qh, zeros] if h // q_per_kv == 0 else [zeros, qh], axis=0))

    yield
    kr = _iota2((BLOCK, BLOCK), 0)
    qc = _iota2((BLOCK, BLOCK), 1)
    bias_prev = jnp.where(kr >= qc, 0.0, NEG_BIG)
    bias_next = jnp.where(kr <= qc, 0.0, NEG_BIG)
    sink_row = jnp.concatenate([jnp.full((1, BLOCK), sink_ref[h], F32) for h in range(ATTN_HEADS)], axis=1)
    gq = q_per_kv * BLOCK
    for j in range(nsub):
        bp = bias_prev if j > 0 else jnp.where(has_prev, bias_prev, NEG_BIG)
        bn = bias_next if j < nsub - 1 else jnp.where(has_next, bias_next, NEG_BIG)
        lo, hi = j * BLOCK, (j + 3) * BLOCK
        q_all = jnp.concatenate([qh[:, lo:lo + BLOCK] for qh in q_heads], axis=1)
        s = _dot(keys[lo:hi], q_all)
        yield
        s0 = s[0:BLOCK] + jnp.tile(bp, (1, ATTN_HEADS))
        s1 = s[BLOCK:2 * BLOCK]
        s2 = s[2 * BLOCK:] + jnp.tile(bn, (1, ATTN_HEADS))
        m = jnp.maximum(jnp.max(jnp.maximum(jnp.maximum(s0, s1), s2), axis=0, keepdims=True), sink_row)
        p0 = jnp.exp(s0 - m)
        p1 = jnp.exp(s1 - m)
        p2 = jnp.exp(s2 - m)
        denom = jnp.sum(p0 + p1 + p2, axis=0, keepdims=True) + jnp.exp(sink_row - m)
        p = jnp.concatenate([p0, p1, p2], axis=0).astype(BF16)
        inv = 1.0 / denom
        outs = []
        for g in range(ATTN_KV_HEADS):
            o = _dot(v_t[g * HEAD_DIM:(g + 1) * HEAD_DIM, lo:hi], p[:, g * gq:(g + 1) * gq])
            o = o * inv[:, g * gq:(g + 1) * gq]
            outs += [o[:, k * BLOCK:(k + 1) * BLOCK] for k in range(q_per_kv)]
        store(lo, jnp.concatenate(outs, axis=0).T.astype(BF16))
        yield


def _tail_kernel(n_tiles, tiles_per_seq,
                 h_ref, mod_ref, n3_ref, wo_ref, wg_ref, wu_ref, wd_ref,
                 q_ref, kvp_ref, kvc_ref, kvn_ref, csp_ref, csc_ref, csn_ref, qw_ref, kw_ref, sink_ref,
                 bc_ref, xdtb_ref, yp_ref, dt_ref, z_ref, dtb_ref, alog_ref, nw_ref, tri_ref,
                 o_ref, y_scr, hstate_ref):
    k = pl.program_id(0)

    @pl.when(k == 0)
    def _():
        y_scr[...] = jnp.zeros_like(y_scr)
        hstate_ref[...] = jnp.zeros_like(hstate_ref)

    rd = k % 2
    wr = 1 - rd

    t = jnp.maximum(n_tiles - 1 - k, 0) % tiles_per_seq
    is_first = t == 0
    is_last = t == tiles_per_seq - 1

    def store_ssd(row0, y):
        y_scr[wr, row0:row0 + CHUNK, 0:SSD_WIDTH] = y

    def store_attn(row0, y):
        y_scr[wr, row0:row0 + BLOCK, SSD_WIDTH:] = y

    attn = _attn_tile(q_ref, kvp_ref, kvc_ref, kvn_ref, csp_ref, csc_ref, csn_ref, qw_ref, kw_ref, sink_ref,
                      jnp.logical_not(is_first), jnp.logical_not(is_last), store_attn)
    scan = _ssd_bwd_tile(bc_ref, xdtb_ref, yp_ref, dt_ref, z_ref, dtb_ref, alog_ref, nw_ref, tri_ref,
                         hstate_ref, is_last, store_ssd)

    h1 = h_ref[0]
    mod = mod_ref[0]
    mix = _dot(y_scr[rd], wo_ref[...])
    next(scan, None)
    h2 = h1 + (1.0 + mod[5:6]) * mix
    u = _ada_norm(h2, n3_ref[...] * (1.0 + mod[7:8]), mod[6:7]).astype(BF16)
    ff = _swiglu(u, wg_ref, wu_ref, wd_ref, side_work=(attn, scan))
    o_ref[0] = h2 + (0.5 * (1.0 + mod[8:9])) * ff


def _tail(h1, mod, norm3, w_out, wg, wu, wd, q, kv, cs_tab, qw_b, kw_b, sink,
          bc, xdtb, ypart, dt, z, dtb, alog, norm_w):
    b, s, d = h1.shape
    d_ff = wg.shape[1]
    tps = s // SEQ_ROWS
    n_tiles = b * tps
    nsub = SEQ_ROWS // BLOCK
    nb = s // BLOCK
    aw = q.shape[-1]
    kvw2 = kv.shape[-1]
    cs_rows = cs_tab.shape[1]
    _, tri_up = _ssd_constants()

    ffn_tile = lambda k: jnp.clip(n_tiles - k, 0, n_tiles - 1)
    mix_tile = lambda k: jnp.maximum(n_tiles - 1 - k, 0)
    prev_blk = lambda t: jnp.maximum(t * nsub - 1, 0)
    next_blk = lambda t: jnp.minimum((t + 1) * nsub, nb - 1)

    def ffn_tok(w):
        return pl.BlockSpec((1, FFN_ROWS, w), lambda k: (ffn_tile(k) // tps, ffn_tile(k) % tps, 0))

    def mix_tok(w):
        return pl.BlockSpec((1, SEQ_ROWS, w), lambda k: (mix_tile(k) // tps, mix_tile(k) % tps, 0))

    in_specs = [
        ffn_tok(d),
        pl.BlockSpec((1, N_MOD, d), lambda k: (ffn_tile(k) // tps, 0, 0)),
        _const_spec((1, d)), _const_spec(w_out.shape),
        _const_spec((d, d_ff)), _const_spec((d, d_ff)), _const_spec((d_ff, d)),
        mix_tok(aw),
        pl.BlockSpec((1, BLOCK, kvw2), lambda k: (mix_tile(k) // tps, prev_blk(mix_tile(k) % tps), 0)),
        mix_tok(kvw2),
        pl.BlockSpec((1, BLOCK, kvw2), lambda k: (mix_tile(k) // tps, next_blk(mix_tile(k) % tps), 0)),
        pl.BlockSpec((1, cs_rows, BLOCK), lambda k: (mix_tile(k) // tps, 0, prev_blk(mix_tile(k) % tps))),
        pl.BlockSpec((1, cs_rows, SEQ_ROWS), lambda k: (mix_tile(k) // tps, 0, mix_tile(k) % tps)),
        pl.BlockSpec((1, cs_rows, BLOCK), lambda k: (mix_tile(k) // tps, 0, next_blk(mix_tile(k) % tps))),
        _const_spec((HEAD_DIM, LANES)), _const_spec((HEAD_DIM, LANES)),
        pl.BlockSpec(memory_space=pltpu.SMEM),
        mix_tok(BC_WIDTH), mix_tok(SSD_WIDTH), mix_tok(SSD_WIDTH), mix_tok(LANES), mix_tok(SSD_WIDTH),
        _const_spec((1, LANES)), _const_spec((1, LANES)), _const_spec((1, SSD_WIDTH)),
        _const_spec(tri_up.shape),
    ]
    return pl.pallas_call(
        functools.partial(_tail_kernel, n_tiles, tps),
        grid=(n_tiles + 1,),
        in_specs=in_specs,
        out_specs=ffn_tok(d),
        out_shape=jax.ShapeDtypeStruct((b, s, d), F32),
        scratch_shapes=[pltpu.VMEM((2, SEQ_ROWS, SSD_WIDTH + aw), BF16),
                        pltpu.VMEM((D_STATE, SSD_WIDTH), F32)],
        compiler_params=pltpu.CompilerParams(dimension_semantics=("arbitrary",),
                                             vmem_limit_bytes=VMEM_LIMIT_BYTES),
        name="mixers_outproj_ffn2",
    )(h1, mod, norm3, w_out, wg, wu, wd, q, kv, kv, kv, cs_tab, cs_tab, cs_tab, qw_b, kw_b, sink,
      bc, xdtb, ypart, dt, z, dtb, alog, norm_w, tri_up)


def _pad_inproj(w_in):
    n_dt = 2 * SSD_HEADS
    s_xbc = SSD_WIDTH + CONV_DIM
    s_dt = s_xbc + n_dt
    dt_cols = jnp.pad(w_in[:, s_xbc:s_dt], ((0, 0), (0, LANES - n_dt)))
    return jnp.concatenate([w_in[:, :s_xbc], w_in[:, s_dt:], dt_cols], axis=1).astype(BF16)


def _pad_lanes(v, width=LANES):
    v = v.reshape(1, -1)
    return jnp.pad(v, ((0, 0), (0, width - v.shape[1])))


def _lane_bcast(v):
    return jnp.broadcast_to(v[:, None], (v.shape[0], LANES))


def kernel(x, c, positions, w_ada, b_ada, norm_ffn1, ffn1_wg, ffn1_wu, ffn1_wd, norm_mix, w_in, conv_w,
           conv_b, dt_bias, a_log, d_skip, ssd_norm_w, q_norm_w, k_norm_w, sink_logit, w_out, norm_ffn2,
           ffn2_wg, ffn2_wu, ffn2_wd):
    depth = w_ada.shape[0]
    b, s, d = x.shape
    h = x.astype(F32)
    c_pad = jnp.pad(c.astype(F32), ((0, -b % SUBLANES), (0, 0)))
    cs_tab = _rope_table(positions)
    for l in range(depth):
        mod = _adaln_mod(c_pad, w_ada[l], b_ada[l])[:b].reshape(b, N_MOD, d)
        conv_w8 = jnp.pad(conv_w[l], ((0, SUBLANES - CONV_K), (0, 0)))
        dtb, alog = _pad_lanes(dt_bias[l]), _pad_lanes(a_log[l])
        h1, z, q, kv, dt, bc, xdtb, ypart = _head(
            h, mod, norm_ffn1[l].reshape(1, d), norm_mix[l].reshape(1, d),
            ffn1_wg[l].astype(BF16), ffn1_wu[l].astype(BF16), ffn1_wd[l].astype(BF16), _pad_inproj(w_in[l]),
            conv_w8, conv_b[l].reshape(1, -1), dtb, alog, jnp.repeat(d_skip[l], SSD_HEAD_DIM).reshape(1, -1))
        h = _tail(h1, mod, norm_ffn2[l].reshape(1, d), w_out[l].astype(BF16),
                  ffn2_wg[l].astype(BF16), ffn2_wu[l].astype(BF16), ffn2_wd[l].astype(BF16),
                  q, kv, cs_tab, _lane_bcast(q_norm_w[l]), _lane_bcast(k_norm_w[l]), sink_logit[l],
                  bc, xdtb, ypart, dt, z, dtb, alog, ssd_norm_w[l].reshape(1, -1))
    return h.astype(x.dtype)
```

```python
import functools

import jax
import jax.numpy as jnp
import numpy as np
from jax import lax
from jax.experimental import pallas as pl
from jax.experimental.pallas import tpu as pltpu

F32 = jnp.float32
BF16 = jnp.bfloat16

SSD_HEAD_DIM = 64
SSD_HEADS = 8
SSD_GROUPS = 2
D_STATE = 128
CONV_K = 5
CHUNK = 128
HEAD_DIM = 64
ATTN_HEADS = 8
ATTN_KV_HEADS = 2
WINDOW = 128
BLOCK = 128
ROPE_DIMS = 16
ROPE_THETA = 500000.0
N_MOD = 9
EPS = 1e-6

SSD_WIDTH = SSD_HEADS * SSD_HEAD_DIM
BC_WIDTH = 2 * SSD_GROUPS * D_STATE
CONV_DIM = SSD_WIDTH + BC_WIDTH

LANES = 128
SUBLANES = 8
VMEM_LIMIT_BYTES = 56 * 1024 * 1024

FFN_ROWS = 512
FFN_CHUNK = 256
SEQ_ROWS = 512
CONV_HALO = 16

NEG_BIG = -1e30


def _dot(a, b):
    return jnp.dot(a, b, preferred_element_type=F32)


def _dot_nt(a, b):
    return lax.dot_general(a, b, (((1,), (1,)), ((), ())), preferred_element_type=F32)


def _row_sums(tri_bf16, x, terms=3):
    acc = None
    r = x
    for t in range(terms):
        h = r.astype(BF16)
        d = _dot(tri_bf16, h)
        acc = d if acc is None else acc + d
        if t + 1 < terms:
            r = r - h.astype(F32)
    return acc


def _silu(x):
    return x * jax.nn.sigmoid(x)


def _iota2(shape, dim):
    return lax.broadcasted_iota(jnp.int32, shape, dim)


def _const_spec(shape):
    nd = len(shape)
    return pl.BlockSpec(shape, lambda *_: (0,) * nd, pipeline_mode=pl.Buffered(1))


def _mod_kernel(c_ref, w_ref, b_ref, o_ref):
    cs = _silu(c_ref[...])
    o_ref[...] = _dot(cs, w_ref[...]) + b_ref[...]


def _adaln_mod(c_pad, w_ada, b_ada):
    rows, d = c_pad.shape
    n = w_ada.shape[1]
    bn = d
    return pl.pallas_call(
        _mod_kernel,
        grid=(n // bn,),
        in_specs=[pl.BlockSpec((rows, d), lambda j: (0, 0)),
                  pl.BlockSpec((d, bn), lambda j: (0, j)),
                  pl.BlockSpec((1, bn), lambda j: (0, j))],
        out_specs=pl.BlockSpec((rows, bn), lambda j: (0, j)),
        out_shape=jax.ShapeDtypeStruct((rows, n), F32),
        name="adaln_mod",
    )(c_pad, w_ada, b_ada.reshape(1, n))


def _rope_kernel(pos_ref, inv_ref, o_ref):
    half = ROPE_DIMS // 2
    s = pos_ref.shape[-1]
    p = pos_ref[0].astype(F32)
    ang = jnp.tile(inv_ref[...], (1, s // LANES)) * p
    o_ref[0, 0:half, :] = jnp.cos(ang)
    o_ref[0, half:, :] = jnp.sin(ang)


def _rope_table(positions):
    half = ROPE_DIMS // 2
    inv = ROPE_THETA ** (-jnp.arange(half, dtype=F32) * 2.0 / ROPE_DIMS)
    b, s = positions.shape
    return pl.pallas_call(
        _rope_kernel,
        grid=(b,),
        in_specs=[pl.BlockSpec((1, 1, s), lambda bi: (bi, 0, 0)), _const_spec((half, LANES))],
        out_specs=pl.BlockSpec((1, 2 * half, s), lambda bi: (bi, 0, 0)),
        out_shape=jax.ShapeDtypeStruct((b, 2 * half, s), F32),
        name="rope_table",
    )(positions.reshape(b, 1, s), jnp.broadcast_to(inv[:, None], (half, LANES)))


def _ada_norm(x, gain_scale, shift):
    ms = jnp.mean(x * x, axis=-1, keepdims=True)
    return x * lax.rsqrt(ms + EPS) * gain_scale + shift


def _swiglu(ub, wg_ref, wu_ref, wd_ref, side_work=(), drain=True):
    d_ff = wg_ref.shape[1]
    acc = None
    for c0 in range(0, d_ff, FFN_CHUNK):
        c1 = min(c0 + FFN_CHUNK, d_ff)
        g = _dot(ub, wg_ref[:, c0:c1])
        up = _dot(ub, wu_ref[:, c0:c1])
        a = (_silu(g) * up).astype(BF16)
        d = _dot(a, wd_ref[c0:c1, :])
        acc = d if acc is None else acc + d
        for gen in side_work:
            next(gen, None)
    for gen in side_work if drain else ():
        for _ in gen:
            pass
    return acc


_INPROJ_LAYOUT = (("z", SSD_WIDTH, F32), ("xbc", CONV_DIM, BF16), ("q", ATTN_HEADS * HEAD_DIM, F32),
                  ("kv", 2 * ATTN_KV_HEADS * HEAD_DIM, F32), ("dt", LANES, F32))


def _ssd_constants():
    r = np.arange(CHUNK)[:, None]
    c = np.arange(CHUNK)[None, :]
    return jnp.asarray(c <= r, dtype=BF16), jnp.asarray(c >= r, dtype=BF16)


def _dt_and_da(dt_raw, dtb_ref, alog_ref):
    lane = _iota2(dt_raw.shape, 1)
    dt = jnp.where(lane < 2 * SSD_HEADS, jax.nn.softplus(dt_raw + dtb_ref[...]), 0.0)
    a = -jnp.exp(alog_ref[...])
    return dt, dt * a


def _head_columns(x, lane0):
    return [jnp.broadcast_to(x[:, lane0 + h:lane0 + h + 1], x.shape) for h in range(SSD_HEADS)]


def _expand_heads(cols):
    first = _iota2(cols[0].shape, 1) < SSD_HEAD_DIM
    return jnp.concatenate([jnp.where(first, cols[2 * p], cols[2 * p + 1]) for p in range(SSD_HEADS // 2)],
                           axis=1)


def _ssd_chunk(direction, da, tri_ref, cm, bm, xdt_fn, h_ref):
    gw = SSD_WIDTH // SSD_GROUPS
    lane0 = 0 if direction == "fwd" else SSD_HEADS
    tot_row = CHUNK - 1 if direction == "fwd" else 0
    cgs = [cm[:, g * D_STATE:(g + 1) * D_STATE] for g in range(SSD_GROUPS)]
    bgs = [bm[:, g * D_STATE:(g + 1) * D_STATE] for g in range(SSD_GROUPS)]
    hgs = [h_ref[:, g * gw:(g + 1) * gw] for g in range(SSD_GROUPS)]

    cs = _row_sums(tri_ref[...], da)
    cbs = [_dot_nt(cgs[g], bgs[g]) for g in range(SSD_GROUPS)]
    chs = [_dot(cgs[g], hgs[g].astype(BF16)) for g in range(SSD_GROUPS)]
    yield

    xdt = xdt_fn()
    cs_cols = _head_columns(cs, lane0)
    csx = _expand_heads(cs_cols)
    tot = csx[tot_row:tot_row + 1, :]
    e_in = jnp.exp(csx)
    xdec = (xdt * jnp.exp(tot - csx)).astype(BF16)
    cdec = jnp.exp(tot)
    cst = cs.T
    row = _iota2((CHUNK, CHUNK), 0)
    col = _iota2((CHUNK, CHUNK), 1)
    keep = (col <= row) if direction == "fwd" else (col > row)
    half = _iota2((CHUNK, LANES), 1) // SSD_HEAD_DIM
    xdt_b = xdt.astype(BF16)
    ys = []
    for g in range(SSD_GROUPS):
        pairs = []
        for pp in range(gw // LANES):
            p = g * (gw // LANES) + pp
            xp = xdt_b[:, p * LANES:(p + 1) * LANES]
            yp = None
            for e in range(2):
                hcol = lane0 + 2 * p + e
                seg = cs_cols[2 * p + e] - cst[hcol:hcol + 1, :]
                m = (cbs[g] * jnp.exp(jnp.where(keep, seg, NEG_BIG))).astype(BF16)
                d = _dot(m, jnp.where(half == e, xp, jnp.zeros_like(xp)))
                yp = d if yp is None else yp + d
            pairs.append(yp)
        ys.append(jnp.concatenate(pairs, axis=1) + chs[g] * e_in[:, g * gw:(g + 1) * gw])
        bgt = bgs[g].astype(F32).T.astype(BF16)
        st = _dot(bgt, xdec[:, g * gw:(g + 1) * gw])
        h_ref[:, g * gw:(g + 1) * gw] = hgs[g] * cdec[:, g * gw:(g + 1) * gw] + st
    return jnp.concatenate(ys, axis=1)


def _ssd_fwd_tile(window_fn, dt_fn, cw_ref, cbias_ref, dtb_ref, alog_ref, dskip_ref,
                  tri_ref, bc_ref, xdtb_ref, yp_ref, h_ref):
    mid = CONV_K // 2
    for j in range(SEQ_ROWS // CHUNK):
        rows = pl.ds(j * CHUNK, CHUNK)
        win = window_fn(j)
        dt, da = _dt_and_da(dt_fn(j), dtb_ref, alog_ref)
        dtx_f = _expand_heads(_head_columns(dt, 0))
        dtx_b = _expand_heads(_head_columns(dt, SSD_HEADS))
        yield
        winf = win.astype(F32)
        acc = cbias_ref[...] + winf[CONV_HALO:CONV_HALO + CHUNK] * cw_ref[mid:mid + 1, :]
        for k in [k for k in range(CONV_K) if k != mid]:
            rolled = pltpu.roll(winf, (mid - k) % winf.shape[0], axis=0)
            acc = acc + rolled[CONV_HALO:CONV_HALO + CHUNK] * cw_ref[k:k + 1, :]
        act = _silu(acc)
        xs = act[:, :SSD_WIDTH]
        bc = act[:, SSD_WIDTH:].astype(BF16)
        xdtb_ref[0, rows, :] = (xs * dtx_b).astype(BF16)
        bc_ref[0, rows, :] = bc
        y = yield from _ssd_chunk("fwd", da, tri_ref, bc[:, SSD_GROUPS * D_STATE:],
                                  bc[:, :SSD_GROUPS * D_STATE], lambda: xs * dtx_f, h_ref)
        yp_ref[0, rows, :] = y + xs * dskip_ref[...]
        yield


def _ssd_bwd_tile(bc_ref, xdtb_ref, yp_ref, dt_ref, z_ref, dtb_ref, alog_ref, nw_ref, tri_ref,
                  h_ref, reset, store):
    h_ref[...] = jnp.where(reset, 0.0, h_ref[...])
    for j in reversed(range(bc_ref.shape[1] // CHUNK)):
        rows = pl.ds(j * CHUNK, CHUNK)
        bc = bc_ref[0, rows, :]
        _, da = _dt_and_da(dt_ref[0, rows, :], dtb_ref, alog_ref)
        y = yield from _ssd_chunk("bwd", da, tri_ref, bc[:, SSD_GROUPS * D_STATE:],
                                  bc[:, :SSD_GROUPS * D_STATE],
                                  lambda: xdtb_ref[0, rows, :].astype(F32), h_ref)
        y = (y + yp_ref[0, rows, :]) * _silu(z_ref[0, rows, :])
        ms = jnp.mean(y * y, axis=-1, keepdims=True)
        store(j * CHUNK, (y * lax.rsqrt(ms + EPS) * nw_ref[...]).astype(BF16))
        yield


RING = 3


def _head_kernel(n_tiles, tiles_per_seq,
                 x_ref, mod_ref, n1_ref, n2_ref, wg_ref, wu_ref, wd_ref, win_ref,
                 cw_ref, cbias_ref, dtb_ref, alog_ref, dskip_ref, tri_ref,
                 h_ref, z_ref, q_ref, kv_ref, dt_ref, bc_ref, xdtb_ref, yp_ref,
                 xstage, dtstage, xring, dtring, hstate_ref):
    k = pl.program_id(0)

    @pl.when(k == 0)
    def _():
        xstage[...] = jnp.zeros_like(xstage)
        dtstage[...] = jnp.zeros_like(dtstage)
        xring[...] = jnp.zeros_like(xring)
        dtring[...] = jnp.zeros_like(dtring)
        hstate_ref[...] = jnp.zeros_like(hstate_ref)

    t = jnp.maximum(k - 2, 0) % tiles_per_seq
    s_new = (k + RING - 1) % RING
    s_main = (k + RING - 2) % RING
    s_prev = k % RING
    has_prev = t > 0
    has_next = t < tiles_per_seq - 1

    def advance_ring():
        xring[s_new] = xstage[...]
        dtring[s_new] = dtstage[...]
        hstate_ref[...] = jnp.where(has_prev, hstate_ref[...], 0.0)

    def window(j):
        lo = j * CHUNK - CONV_HALO
        hi = (j + 1) * CHUNK + CONV_HALO
        parts = []
        if lo < 0:
            halo = xring[s_prev, SEQ_ROWS + lo:, :]
            parts.append(jnp.where(has_prev, halo, jnp.zeros_like(halo)))
        parts.append(xring[s_main, max(lo, 0):min(hi, SEQ_ROWS), :])
        if hi > SEQ_ROWS:
            halo = xring[s_new, 0:hi - SEQ_ROWS, :]
            parts.append(jnp.where(has_next, halo, jnp.zeros_like(halo)))
        return parts[0] if len(parts) == 1 else jnp.concatenate(parts, axis=0)

    advance_ring()
    scan = _ssd_fwd_tile(window, lambda j: dtring[s_main, j * CHUNK:(j + 1) * CHUNK, :],
                         cw_ref, cbias_ref, dtb_ref, alog_ref, dskip_ref, tri_ref,
                         bc_ref, xdtb_ref, yp_ref, hstate_ref)

    x = x_ref[0]
    mod = mod_ref[0]
    u = _ada_norm(x, n1_ref[...] * (1.0 + mod[1:2]), mod[0:1]).astype(BF16)
    ff = _swiglu(u, wg_ref, wu_ref, wd_ref, side_work=(scan,), drain=False)
    h = x + (0.5 * (1.0 + mod[2:3])) * ff
    h_ref[0] = h
    u2 = _ada_norm(h, n2_ref[...] * (1.0 + mod[4:5]), mod[3:4]).astype(BF16)
    dests = {"z": (z_ref,), "xbc": (xstage,), "q": (q_ref,), "kv": (kv_ref,), "dt": (dt_ref, dtstage)}
    c0 = 0
    for name, w, _ in _INPROJ_LAYOUT:
        piece = _dot(u2, win_ref[:, c0:c0 + w])
        for ref in dests[name]:
            if len(ref.shape) == 3:
                ref[0] = piece.astype(ref.dtype)
            else:
                ref[...] = piece.astype(ref.dtype)
        c0 += w
        next(scan, None)
    for _ in scan:
        pass


def _head(x, mod, norm1, norm2, wg, wu, wd, w_in_p, conv_w8, conv_b, dtb, alog, dskip):
    b, s, d = x.shape
    d_ff = wg.shape[1]
    tps = s // SEQ_ROWS
    n_tiles = b * tps
    tri_lo, _ = _ssd_constants()
    ffn_tile = lambda k: jnp.minimum(k, n_tiles - 1)
    scan_tile = lambda k: jnp.maximum(k - 2, 0)

    def ffn_tok(w):
        return pl.BlockSpec((1, FFN_ROWS, w), lambda k: (ffn_tile(k) // tps, ffn_tile(k) % tps, 0))

    def scan_tok(w):
        return pl.BlockSpec((1, SEQ_ROWS, w), lambda k: (scan_tile(k) // tps, scan_tile(k) % tps, 0))

    widths = {name: (w, dt) for name, w, dt in _INPROJ_LAYOUT}
    hbm_outs = ("z", "q", "kv", "dt")
    out_shape = [jax.ShapeDtypeStruct((b, s, d), F32)]
    out_shape += [jax.ShapeDtypeStruct((b, s, widths[n][0]), widths[n][1]) for n in hbm_outs]
    out_shape += [jax.ShapeDtypeStruct((b, s, BC_WIDTH), BF16), jax.ShapeDtypeStruct((b, s, SSD_WIDTH), BF16),
                  jax.ShapeDtypeStruct((b, s, SSD_WIDTH), F32)]
    out_specs = [ffn_tok(d)] + [ffn_tok(widths[n][0]) for n in hbm_outs]
    out_specs += [scan_tok(BC_WIDTH), scan_tok(SSD_WIDTH), scan_tok(SSD_WIDTH)]
    return pl.pallas_call(
        functools.partial(_head_kernel, n_tiles, tps),
        grid=(n_tiles + 2,),
        in_specs=[ffn_tok(d),
                  pl.BlockSpec((1, N_MOD, d), lambda k: (ffn_tile(k) // tps, 0, 0)),
                  _const_spec((1, d)), _const_spec((1, d)),
                  _const_spec((d, d_ff)), _const_spec((d, d_ff)), _const_spec((d_ff, d)),
                  _const_spec(w_in_p.shape),
                  _const_spec(conv_w8.shape), _const_spec((1, CONV_DIM)),
                  _const_spec((1, LANES)), _const_spec((1, LANES)), _const_spec((1, SSD_WIDTH)),
                  _const_spec(tri_lo.shape)],
        out_specs=out_specs,
        out_shape=out_shape,
        scratch_shapes=[pltpu.VMEM((SEQ_ROWS, CONV_DIM), BF16), pltpu.VMEM((SEQ_ROWS, LANES), F32),
                        pltpu.VMEM((RING, SEQ_ROWS, CONV_DIM), BF16), pltpu.VMEM((RING, SEQ_ROWS, LANES), F32),
                        pltpu.VMEM((D_STATE, SSD_WIDTH), F32)],
        compiler_params=pltpu.CompilerParams(dimension_semantics=("arbitrary",),
                                             vmem_limit_bytes=VMEM_LIMIT_BYTES),
        name="ffn1_inproj_scan",
    )(x, mod, norm1, norm2, wg, wu, wd, w_in_p, conv_w8, conv_b, dtb, alog, dskip, tri_lo)


def _head_prep_t(t, w_b, cos, sin):
    ms = jnp.sum(t * t, axis=0, keepdims=True) * (1.0 / HEAD_DIM)
    tn = t * lax.rsqrt(ms + EPS) * w_b
    half = ROPE_DIMS // 2
    t1 = tn[0:half]
    t2 = tn[half:ROPE_DIMS]
    return jnp.concatenate([t1 * cos - t2 * sin, t2 * cos + t1 * sin, tn[ROPE_DIMS:]], axis=0)


def _attn_tile(q_ref, kvp_ref, kvc_ref, kvn_ref, csp_ref, csc_ref, csn_ref, qw_ref, kw_ref, sink_ref,
               has_prev, has_next, store):
    tq = q_ref.shape[1]
    nsub = tq // BLOCK
    nw = tq + 2 * BLOCK
    kvw = ATTN_KV_HEADS * HEAD_DIM
    half = ROPE_DIMS // 2
    q_per_kv = ATTN_HEADS // ATTN_KV_HEADS

    kv = jnp.concatenate([kvp_ref[0], kvc_ref[0], kvn_ref[0]], axis=0)
    cs = jnp.concatenate([csp_ref[0], csc_ref[0], csn_ref[0]], axis=1)
    k_t = kv[:, :kvw].T
    v_t = kv[:, kvw:].T.astype(BF16)
    kw_b = jnp.tile(kw_ref[...], (1, nw // LANES))
    k_prep = jnp.concatenate(
        [_head_prep_t(k_t[g * HEAD_DIM:(g + 1) * HEAD_DIM], kw_b, cs[0:half], cs[half:])
         for g in range(ATTN_KV_HEADS)], axis=0)
    keys = k_prep.T.astype(BF16)

    q_t = q_ref[0].T
    qw_b = jnp.tile(qw_ref[...] * (HEAD_DIM ** -0.5), (1, tq // LANES))
    cs_q = csc_ref[0]
    zeros = jnp.zeros((HEAD_DIM, tq), BF16)
    q_heads = []
    for h in range(ATTN_HEADS):
        qh = _head_prep_t(q_t[h * HEAD_DIM:(h + 1) * HEAD_DIM], qw_b, cs_q[0:half], cs_q[half:]).astype(BF16)
        q_heads.append(jnp.concatenate([qh, zeros] if h // q_per_kv == 0 else [zeros, qh], axis=0))

    yield
    kr = _iota2((BLOCK, BLOCK), 0)
    qc = _iota2((BLOCK, BLOCK), 1)
    bias_prev = jnp.where(kr >= qc, 0.0, NEG_BIG)
    bias_next = jnp.where(kr <= qc, 0.0, NEG_BIG)
    sink_row = jnp.concatenate([jnp.full((1, BLOCK), sink_ref[h], F32) for h in range(ATTN_HEADS)], axis=1)
    gq = q_per_kv * BLOCK
    for j in range(nsub):
        bp = bias_prev if j > 0 else jnp.where(has_prev, bias_prev, NEG_BIG)
        bn = bias_next if j < nsub - 1 else jnp.where(has_next, bias_next, NEG_BIG)
        lo, hi = j * BLOCK, (j + 3) * BLOCK
        q_all = jnp.concatenate([qh[:, lo:lo + BLOCK] for qh in q_heads], axis=1)
        s = _dot(keys[lo:hi], q_all)
        yield
        s0 = s[0:BLOCK] + jnp.tile(bp, (1, ATTN_HEADS))
        s1 = s[BLOCK:2 * BLOCK]
        s2 = s[2 * BLOCK:] + jnp.tile(bn, (1, ATTN_HEADS))
        m = jnp.maximum(jnp.max(jnp.maximum(jnp.maximum(s0, s1), s2), axis=0, keepdims=True), sink_row)
        p0 = jnp.exp(s0 - m)
        p1 = jnp.exp(s1 - m)
        p2 = jnp.exp(s2 - m)
        denom = jnp.sum(p0 + p1 + p2, axis=0, keepdims=True) + jnp.exp(sink_row - m)
        p = jnp.concatenate([p0, p1, p2], axis=0).astype(BF16)
        inv = 1.0 / denom
        outs = []
        for g in range(ATTN_KV_HEADS):
            o = _dot(v_t[g * HEAD_DIM:(g + 1) * HEAD_DIM, lo:hi], p[:, g * gq:(g + 1) * gq])
            o = o * inv[:, g * gq:(g + 1) * gq]
            outs += [o[:, k * BLOCK:(k + 1) * BLOCK] for k in range(q_per_kv)]
        store(lo, jnp.concatenate(outs, axis=0).T.astype(BF16))
        yield


def _tail_kernel(n_tiles, tiles_per_seq,
                 h_ref, mod_ref, n3_ref, wo_ref, wg_ref, wu_ref, wd_ref,
                 q_ref, kvp_ref, kvc_ref, kvn_ref, csp_ref, csc_ref, csn_ref, qw_ref, kw_ref, sink_ref,
                 bc_ref, xdtb_ref, yp_ref, dt_ref, z_ref, dtb_ref, alog_ref, nw_ref, tri_ref,
                 o_ref, y_scr, hstate_ref):
    k = pl.program_id(0)

    @pl.when(k == 0)
    def _():
        y_scr[...] = jnp.zeros_like(y_scr)
        hstate_ref[...] = jnp.zeros_like(hstate_ref)

    rd = k % 2
    wr = 1 - rd

    t = jnp.maximum(n_tiles - 1 - k, 0) % tiles_per_seq
    is_first = t == 0
    is_last = t == tiles_per_seq - 1

    def store_ssd(row0, y):
        y_scr[wr, row0:row0 + CHUNK, 0:SSD_WIDTH] = y

    def store_attn(row0, y):
        y_scr[wr, row0:row0 + BLOCK, SSD_WIDTH:] = y

    attn = _attn_tile(q_ref, kvp_ref, kvc_ref, kvn_ref, csp_ref, csc_ref, csn_ref, qw_ref, kw_ref, sink_ref,
                      jnp.logical_not(is_first), jnp.logical_not(is_last), store_attn)
    scan = _ssd_bwd_tile(bc_ref, xdtb_ref, yp_ref, dt_ref, z_ref, dtb_ref, alog_ref, nw_ref, tri_ref,
                         hstate_ref, is_last, store_ssd)

    h1 = h_ref[0]
    mod = mod_ref[0]
    mix = _dot(y_scr[rd], wo_ref[...])
    next(scan, None)
    h2 = h1 + (1.0 + mod[5:6]) * mix
    u = _ada_norm(h2, n3_ref[...] * (1.0 + mod[7:8]), mod[6:7]).astype(BF16)
    ff = _swiglu(u, wg_ref, wu_ref, wd_ref, side_work=(attn, scan))
    o_ref[0] = h2 + (0.5 * (1.0 + mod[8:9])) * ff


def _tail(h1, mod, norm3, w_out, wg, wu, wd, q, kv, cs_tab, qw_b, kw_b, sink,
          bc, xdtb, ypart, dt, z, dtb, alog, norm_w):
    b, s, d = h1.shape
    d_ff = wg.shape[1]
    tps = s // SEQ_ROWS
    n_tiles = b * tps
    nsub = SEQ_ROWS // BLOCK
    nb = s // BLOCK
    aw = q.shape[-1]
    kvw2 = kv.shape[-1]
    cs_rows = cs_tab.shape[1]
    _, tri_up = _ssd_constants()

    ffn_tile = lambda k: jnp.clip(n_tiles - k, 0, n_tiles - 1)
    mix_tile = lambda k: jnp.maximum(n_tiles - 1 - k, 0)
    prev_blk = lambda t: jnp.maximum(t * nsub - 1, 0)
    next_blk = lambda t: jnp.minimum((t + 1) * nsub, nb - 1)

    def ffn_tok(w):
        return pl.BlockSpec((1, FFN_ROWS, w), lambda k: (ffn_tile(k) // tps, ffn_tile(k) % tps, 0))

    def mix_tok(w):
        return pl.BlockSpec((1, SEQ_ROWS, w), lambda k: (mix_tile(k) // tps, mix_tile(k) % tps, 0))

    in_specs = [
        ffn_tok(d),
        pl.BlockSpec((1, N_MOD, d), lambda k: (ffn_tile(k) // tps, 0, 0)),
        _const_spec((1, d)), _const_spec(w_out.shape),
        _const_spec((d, d_ff)), _const_spec((d, d_ff)), _const_spec((d_ff, d)),
        mix_tok(aw),
        pl.BlockSpec((1, BLOCK, kvw2), lambda k: (mix_tile(k) // tps, prev_blk(mix_tile(k) % tps), 0)),
        mix_tok(kvw2),
        pl.BlockSpec((1, BLOCK, kvw2), lambda k: (mix_tile(k) // tps, next_blk(mix_tile(k) % tps), 0)),
        pl.BlockSpec((1, cs_rows, BLOCK), lambda k: (mix_tile(k) // tps, 0, prev_blk(mix_tile(k) % tps))),
        pl.BlockSpec((1, cs_rows, SEQ_ROWS), lambda k: (mix_tile(k) // tps, 0, mix_tile(k) % tps)),
        pl.BlockSpec((1, cs_rows, BLOCK), lambda k: (mix_tile(k) // tps, 0, next_blk(mix_tile(k) % tps))),
        _const_spec((HEAD_DIM, LANES)), _const_spec((HEAD_DIM, LANES)),
        pl.BlockSpec(memory_space=pltpu.SMEM),
        mix_tok(BC_WIDTH), mix_tok(SSD_WIDTH), mix_tok(SSD_WIDTH), mix_tok(LANES), mix_tok(SSD_WIDTH),
        _const_spec((1, LANES)), _const_spec((1, LANES)), _const_spec((1, SSD_WIDTH)),
        _const_spec(tri_up.shape),
    ]
    return pl.pallas_call(
        functools.partial(_tail_kernel, n_tiles, tps),
        grid=(n_tiles + 1,),
        in_specs=in_specs,
        out_specs=ffn_tok(d),
        out_shape=jax.ShapeDtypeStruct((b, s, d), F32),
        scratch_shapes=[pltpu.VMEM((2, SEQ_ROWS, SSD_WIDTH + aw), BF16),
                        pltpu.VMEM((D_STATE, SSD_WIDTH), F32)],
        compiler_params=pltpu.CompilerParams(dimension_semantics=("arbitrary",),
                                             vmem_limit_bytes=VMEM_LIMIT_BYTES),
        name="mixers_outproj_ffn2",
    )(h1, mod, norm3, w_out, wg, wu, wd, q, kv, kv, kv, cs_tab, cs_tab, cs_tab, qw_b, kw_b, sink,
      bc, xdtb, ypart, dt, z, dtb, alog, norm_w, tri_up)


def _pad_inproj(w_in):
    n_dt = 2 * SSD_HEADS
    s_xbc = SSD_WIDTH + CONV_DIM
    s_dt = s_xbc + n_dt
    dt_cols = jnp.pad(w_in[:, s_xbc:s_dt], ((0, 0), (0, LANES - n_dt)))
    return jnp.concatenate([w_in[:, :s_xbc], w_in[:, s_dt:], dt_cols], axis=1).astype(BF16)


def _pad_lanes(v, width=LANES):
    v = v.reshape(1, -1)
    return jnp.pad(v, ((0, 0), (0, width - v.shape[1])))


def _lane_bcast(v):
    return jnp.broadcast_to(v[:, None], (v.shape[0], LANES))


def kernel(x, c, positions, w_ada, b_ada, norm_ffn1, ffn1_wg, ffn1_wu, ffn1_wd, norm_mix, w_in, conv_w,
           conv_b, dt_bias, a_log, d_skip, ssd_norm_w, q_norm_w, k_norm_w, sink_logit, w_out, norm_ffn2,
           ffn2_wg, ffn2_wu, ffn2_wd):
    depth = w_ada.shape[0]
    b, s, d = x.shape
    h = x.astype(F32)
    c_pad = jnp.pad(c.astype(F32), ((0, -b % SUBLANES), (0, 0)))
    cs_tab = _rope_table(positions)
    for l in range(depth):
        mod = _adaln_mod(c_pad, w_ada[l], b_ada[l])[:b].reshape(b, N_MOD, d)
        conv_w8 = jnp.pad(conv_w[l], ((0, SUBLANES - CONV_K), (0, 0)))
        dtb, alog = _pad_lanes(dt_bias[l]), _pad_lanes(a_log[l])
        h1, z, q, kv, dt, bc, xdtb, ypart = _head(
            h, mod, norm_ffn1[l].reshape(1, d), norm_mix[l].reshape(1, d),
            ffn1_wg[l].astype(BF16), ffn1_wu[l].astype(BF16), ffn1_wd[l].astype(BF16), _pad_inproj(w_in[l]),
            conv_w8, conv_b[l].reshape(1, -1), dtb, alog, jnp.repeat(d_skip[l], SSD_HEAD_DIM).reshape(1, -1))
        h = _tail(h1, mod, norm_ffn2[l].reshape(1, d), w_out[l].astype(BF16),
                  ffn2_wg[l].astype(BF16), ffn2_wu[l].astype(BF16), ffn2_wd[l].astype(BF16),
                  q, kv, cs_tab, _lane_bcast(q_norm_w[l]), _lane_bcast(k_norm_w[l]), sink_logit[l],
                  bc, xdtb, ypart, dt, z, dtb, alog, ssd_norm_w[l].reshape(1, -1))
    return h.astype(x.dtype)
```

```python
import functools

import jax
import jax.numpy as jnp
import numpy as np
from jax import lax
from jax.experimental import pallas as pl
from jax.experimental.pallas import tpu as pltpu

F32 = jnp.float32
BF16 = jnp.bfloat16

SSD_HEAD_DIM = 64
SSD_HEADS = 8
SSD_GROUPS = 2
D_STATE = 128
CONV_K = 5
CHUNK = 128
HEAD_DIM = 64
ATTN_HEADS = 8
ATTN_KV_HEADS = 2
WINDOW = 128
BLOCK = 128
ROPE_DIMS = 16
ROPE_THETA = 500000.0
N_MOD = 9
EPS = 1e-6

SSD_WIDTH = SSD_HEADS * SSD_HEAD_DIM
BC_WIDTH = 2 * SSD_GROUPS * D_STATE
CONV_DIM = SSD_WIDTH + BC_WIDTH

LANES = 128
SUBLANES = 8
VMEM_LIMIT_BYTES = 56 * 1024 * 1024

SEQ_ROWS = 512
FFN_CHUNK = 256
CONV_HALO = 16

NEG_BIG = -1e30


def _dot(a, b):
    return jnp.dot(a, b, preferred_element_type=F32)


def _dot_nt(a, b):
    return lax.dot_general(a, b, (((1,), (1,)), ((), ())), preferred_element_type=F32)


def _row_sums(tri_bf16, x, terms=3):
    acc = None
    r = x
    for t in range(terms):
        h = r.astype(BF16)
        d = _dot(tri_bf16, h)
        acc = d if acc is None else acc + d
        if t + 1 < terms:
            r = r - h.astype(F32)
    return acc


def _silu(x):
    return x * jax.nn.sigmoid(x)


def _iota2(shape, dim):
    return lax.broadcasted_iota(jnp.int32, shape, dim)


def _const_spec(shape):
    nd = len(shape)
    return pl.BlockSpec(shape, lambda *_: (0,) * nd, pipeline_mode=pl.Buffered(1))


def _mod_kernel(c_ref, w_ref, b_ref, o_ref):
    cs = _silu(c_ref[...])
    o_ref[...] = _dot(cs, w_ref[...]) + b_ref[...]


def _adaln_mod(c_pad, w_ada, b_ada):
    rows, d = c_pad.shape
    n = w_ada.shape[1]
    bn = d
    return pl.pallas_call(
        _mod_kernel,
        grid=(n // bn,),
        in_specs=[pl.BlockSpec((rows, d), lambda j: (0, 0)),
                  pl.BlockSpec((d, bn), lambda j: (0, j)),
                  pl.BlockSpec((1, bn), lambda j: (0, j))],
        out_specs=pl.BlockSpec((rows, bn), lambda j: (0, j)),
        out_shape=jax.ShapeDtypeStruct((rows, n), F32),
        name="adaln_mod",
    )(c_pad, w_ada, b_ada.reshape(1, n))


def _rope_kernel(pos_ref, inv_ref, o_ref):
    half = ROPE_DIMS // 2
    s = pos_ref.shape[-1]
    p = pos_ref[0].astype(F32)
    ang = jnp.tile(inv_ref[...], (1, s // LANES)) * p
    o_ref[0, 0:half, :] = jnp.cos(ang)
    o_ref[0, half:, :] = jnp.sin(ang)


def _rope_table(positions):
    half = ROPE_DIMS // 2
    inv = ROPE_THETA ** (-jnp.arange(half, dtype=F32) * 2.0 / ROPE_DIMS)
    b, s = positions.shape
    return pl.pallas_call(
        _rope_kernel,
        grid=(b,),
        in_specs=[pl.BlockSpec((1, 1, s), lambda bi: (bi, 0, 0)), _const_spec((half, LANES))],
        out_specs=pl.BlockSpec((1, 2 * half, s), lambda bi: (bi, 0, 0)),
        out_shape=jax.ShapeDtypeStruct((b, 2 * half, s), F32),
        name="rope_table",
    )(positions.reshape(b, 1, s), jnp.broadcast_to(inv[:, None], (half, LANES)))


def _ada_norm(x, gain_scale, shift):
    ms = jnp.mean(x * x, axis=-1, keepdims=True)
    return x * lax.rsqrt(ms + EPS) * gain_scale + shift


def _swiglu(ub, wg_ref, wu_ref, wd_ref, side_work=(), drain=True):
    d_ff = wg_ref.shape[1]
    acc = None
    for c0 in range(0, d_ff, FFN_CHUNK):
        c1 = min(c0 + FFN_CHUNK, d_ff)
        g = _dot(ub, wg_ref[:, c0:c1])
        up = _dot(ub, wu_ref[:, c0:c1])
        a = (_silu(g) * up).astype(BF16)
        d = _dot(a, wd_ref[c0:c1, :])
        acc = d if acc is None else acc + d
        for gen in side_work:
            next(gen, None)
    for gen in side_work if drain else ():
        for _ in gen:
            pass
    return acc


_INPROJ_LAYOUT = (("z", SSD_WIDTH, F32), ("xbc", CONV_DIM, BF16), ("q", ATTN_HEADS * HEAD_DIM, F32),
                  ("kv", 2 * ATTN_KV_HEADS * HEAD_DIM, F32), ("dt", LANES, F32))


def _ssd_constants():
    r = np.arange(CHUNK)[:, None]
    c = np.arange(CHUNK)[None, :]
    return jnp.asarray(c <= r, dtype=BF16), jnp.asarray(c >= r, dtype=BF16)


def _dt_and_da(dt_raw, dtb_ref, alog_ref):
    lane = _iota2(dt_raw.shape, 1)
    dt = jnp.where(lane < 2 * SSD_HEADS, jax.nn.softplus(dt_raw + dtb_ref[...]), 0.0)
    a = -jnp.exp(alog_ref[...])
    return dt, dt * a


def _head_columns(x, lane0):
    return [jnp.broadcast_to(x[:, lane0 + h:lane0 + h + 1], x.shape) for h in range(SSD_HEADS)]


def _expand_heads(cols):
    first = _iota2(cols[0].shape, 1) < SSD_HEAD_DIM
    return jnp.concatenate([jnp.where(first, cols[2 * p], cols[2 * p + 1]) for p in range(SSD_HEADS // 2)],
                           axis=1)


def _ssd_chunk(direction, da, tri_ref, cm, bm, xdt_fn, h_ref):
    gw = SSD_WIDTH // SSD_GROUPS
    lane0 = 0 if direction == "fwd" else SSD_HEADS
    tot_row = CHUNK - 1 if direction == "fwd" else 0
    cgs = [cm[:, g * D_STATE:(g + 1) * D_STATE] for g in range(SSD_GROUPS)]
    bgs = [bm[:, g * D_STATE:(g + 1) * D_STATE] for g in range(SSD_GROUPS)]
    hgs = [h_ref[:, g * gw:(g + 1) * gw] for g in range(SSD_GROUPS)]

    cs = _row_sums(tri_ref[...], da)
    cbs = [_dot_nt(cgs[g], bgs[g]) for g in range(SSD_GROUPS)]
    chs = [_dot(cgs[g], hgs[g].astype(BF16)) for g in range(SSD_GROUPS)]
    yield

    xdt = xdt_fn()
    cs_cols = _head_columns(cs, lane0)
    csx = _expand_heads(cs_cols)
    tot = csx[tot_row:tot_row + 1, :]
    e_in = jnp.exp(csx)
    xdec = (xdt * jnp.exp(tot - csx)).astype(BF16)
    cdec = jnp.exp(tot)
    cst = cs.T
    row = _iota2((CHUNK, CHUNK), 0)
    col = _iota2((CHUNK, CHUNK), 1)
    keep = (col <= row) if direction == "fwd" else (col > row)
    half = _iota2((CHUNK, LANES), 1) // SSD_HEAD_DIM
    xdt_b = xdt.astype(BF16)
    ys = []
    for g in range(SSD_GROUPS):
        pairs = []
        for pp in range(gw // LANES):
            p = g * (gw // LANES) + pp
            xp = xdt_b[:, p * LANES:(p + 1) * LANES]
            yp = None
            for e in range(2):
                hcol = lane0 + 2 * p + e
                seg = cs_cols[2 * p + e] - cst[hcol:hcol + 1, :]
                m = (cbs[g] * jnp.exp(jnp.where(keep, seg, NEG_BIG))).astype(BF16)
                d = _dot(m, jnp.where(half == e, xp, jnp.zeros_like(xp)))
                yp = d if yp is None else yp + d
            pairs.append(yp)
        ys.append(jnp.concatenate(pairs, axis=1) + chs[g] * e_in[:, g * gw:(g + 1) * gw])
        bgt = bgs[g].astype(F32).T.astype(BF16)
        st = _dot(bgt, xdec[:, g * gw:(g + 1) * gw])
        h_ref[:, g * gw:(g + 1) * gw] = hgs[g] * cdec[:, g * gw:(g + 1) * gw] + st
    return jnp.concatenate(ys, axis=1)


def _ssd_fwd_tile(window_fn, dt_fn, cw_ref, cbias_ref, dtb_ref, alog_ref, dskip_ref,
                  tri_ref, bc_ref, xdtb_ref, yp_ref, h_ref):
    mid = CONV_K // 2
    for j in range(SEQ_ROWS // CHUNK):
        rows = pl.ds(j * CHUNK, CHUNK)
        win = window_fn(j)
        dt, da = _dt_and_da(dt_fn(j), dtb_ref, alog_ref)
        dtx_f = _expand_heads(_head_columns(dt, 0))
        dtx_b = _expand_heads(_head_columns(dt, SSD_HEADS))
        yield
        winf = win.astype(F32)
        acc = cbias_ref[...] + winf[CONV_HALO:CONV_HALO + CHUNK] * cw_ref[mid:mid + 1, :]
        for k in [k for k in range(CONV_K) if k != mid]:
            rolled = pltpu.roll(winf, (mid - k) % winf.shape[0], axis=0)
            acc = acc + rolled[CONV_HALO:CONV_HALO + CHUNK] * cw_ref[k:k + 1, :]
        act = _silu(acc)
        xs = act[:, :SSD_WIDTH]
        bc = act[:, SSD_WIDTH:].astype(BF16)
        xdtb_ref[0, rows, :] = (xs * dtx_b).astype(BF16)
        bc_ref[0, rows, :] = bc
        y = yield from _ssd_chunk("fwd", da, tri_ref, bc[:, SSD_GROUPS * D_STATE:],
                                  bc[:, :SSD_GROUPS * D_STATE], lambda: xs * dtx_f, h_ref)
        yp_ref[0, rows, :] = y + xs * dskip_ref[...]
        yield


def _ssd_bwd_tile(bc_ref, xdtb_ref, yp_ref, dt_ref, z_ref, dtb_ref, alog_ref, nw_ref, tri_ref,
                  h_ref, reset, store):
    h_ref[...] = jnp.where(reset, 0.0, h_ref[...])
    for j in reversed(range(bc_ref.shape[1] // CHUNK)):
        rows = pl.ds(j * CHUNK, CHUNK)
        bc = bc_ref[0, rows, :]
        _, da = _dt_and_da(dt_ref[0, rows, :], dtb_ref, alog_ref)
        y = yield from _ssd_chunk("bwd", da, tri_ref, bc[:, SSD_GROUPS * D_STATE:],
                                  bc[:, :SSD_GROUPS * D_STATE],
                                  lambda: xdtb_ref[0, rows, :].astype(F32), h_ref)
        y = (y + yp_ref[0, rows, :]) * _silu(z_ref[0, rows, :])
        ms = jnp.mean(y * y, axis=-1, keepdims=True)
        store(j * CHUNK, (y * lax.rsqrt(ms + EPS) * nw_ref[...]).astype(BF16))
        yield


RING = 3


def _head_kernel(n_tiles, tiles_per_seq,
                 x_ref, mod_ref, n1_ref, n2_ref, wg_ref, wu_ref, wd_ref, win_ref,
                 cw_ref, cbias_ref, dtb_ref, alog_ref, dskip_ref, tri_ref,
                 h_ref, z_ref, q_ref, kv_ref, dt_ref, bc_ref, xdtb_ref, yp_ref,
                 xstage, dtstage, xring, dtring, hstate_ref):
    k = pl.program_id(0)

    @pl.when(k == 0)
    def _():
        xstage[...] = jnp.zeros_like(xstage)
        dtstage[...] = jnp.zeros_like(dtstage)
        xring[...] = jnp.zeros_like(xring)
        dtring[...] = jnp.zeros_like(dtring)
        hstate_ref[...] = jnp.zeros_like(hstate_ref)

    t = jnp.maximum(k - 2, 0) % tiles_per_seq
    s_new = (k + RING - 1) % RING
    s_main = (k + RING - 2) % RING
    s_prev = k % RING
    has_prev = t > 0
    has_next = t < tiles_per_seq - 1

    def advance_ring():
        xring[s_new] = xstage[...]
        dtring[s_new] = dtstage[...]
        hstate_ref[...] = jnp.where(has_prev, hstate_ref[...], 0.0)

    def window(j):
        lo = j * CHUNK - CONV_HALO
        hi = (j + 1) * CHUNK + CONV_HALO
        parts = []
        if lo < 0:
            halo = xring[s_prev, SEQ_ROWS + lo:, :]
            parts.append(jnp.where(has_prev, halo, jnp.zeros_like(halo)))
        parts.append(xring[s_main, max(lo, 0):min(hi, SEQ_ROWS), :])
        if hi > SEQ_ROWS:
            halo = xring[s_new, 0:hi - SEQ_ROWS, :]
            parts.append(jnp.where(has_next, halo, jnp.zeros_like(halo)))
        return parts[0] if len(parts) == 1 else jnp.concatenate(parts, axis=0)

    advance_ring()
    scan = _ssd_fwd_tile(window, lambda j: dtring[s_main, j * CHUNK:(j + 1) * CHUNK, :],
                         cw_ref, cbias_ref, dtb_ref, alog_ref, dskip_ref, tri_ref,
                         bc_ref, xdtb_ref, yp_ref, hstate_ref)

    x = x_ref[0]
    mod = mod_ref[0]
    u = _ada_norm(x, n1_ref[...] * (1.0 + mod[1:2]), mod[0:1]).astype(BF16)
    ff = _swiglu(u, wg_ref, wu_ref, wd_ref, side_work=(scan,), drain=False)
    h = x + (0.5 * (1.0 + mod[2:3])) * ff
    h_ref[0] = h
    u2 = _ada_norm(h, n2_ref[...] * (1.0 + mod[4:5]), mod[3:4]).astype(BF16)
    dests = {"z": (z_ref,), "xbc": (xstage,), "q": (q_ref,), "kv": (kv_ref,), "dt": (dt_ref, dtstage)}
    c0 = 0
    for name, w, _ in _INPROJ_LAYOUT:
        piece = _dot(u2, win_ref[:, c0:c0 + w])
        for ref in dests[name]:
            if len(ref.shape) == 3:
                ref[0] = piece.astype(ref.dtype)
            else:
                ref[...] = piece.astype(ref.dtype)
        c0 += w
        next(scan, None)
    for _ in scan:
        pass


def _head(x, mod, norm1, norm2, wg, wu, wd, w_in_p, conv_w8, conv_b, dtb, alog, dskip):
    b, s, d = x.shape
    d_ff = wg.shape[1]
    tps = s // SEQ_ROWS
    n_tiles = b * tps
    tri_lo, _ = _ssd_constants()
    ffn_tile = lambda k: jnp.minimum(k, n_tiles - 1)
    scan_tile = lambda k: jnp.maximum(k - 2, 0)

    def ffn_tok(w):
        return pl.BlockSpec((1, SEQ_ROWS, w), lambda k: (ffn_tile(k) // tps, ffn_tile(k) % tps, 0))

    def scan_tok(w):
        return pl.BlockSpec((1, SEQ_ROWS, w), lambda k: (scan_tile(k) // tps, scan_tile(k) % tps, 0))

    widths = {name: (w, dt) for name, w, dt in _INPROJ_LAYOUT}
    hbm_outs = ("z", "q", "kv", "dt")
    out_shape = [jax.ShapeDtypeStruct((b, s, d), F32)]
    out_shape += [jax.ShapeDtypeStruct((b, s, widths[n][0]), widths[n][1]) for n in hbm_outs]
    out_shape += [jax.ShapeDtypeStruct((b, s, BC_WIDTH), BF16), jax.ShapeDtypeStruct((b, s, SSD_WIDTH), BF16),
                  jax.ShapeDtypeStruct((b, s, SSD_WIDTH), F32)]
    out_specs = [ffn_tok(d)] + [ffn_tok(widths[n][0]) for n in hbm_outs]
    out_specs += [scan_tok(BC_WIDTH), scan_tok(SSD_WIDTH), scan_tok(SSD_WIDTH)]
    return pl.pallas_call(
        functools.partial(_head_kernel, n_tiles, tps),
        grid=(n_tiles + 2,),
        in_specs=[ffn_tok(d),
                  pl.BlockSpec((1, N_MOD, d), lambda k: (ffn_tile(k) // tps, 0, 0)),
                  _const_spec((1, d)), _const_spec((1, d)),
                  _const_spec((d, d_ff)), _const_spec((d, d_ff)), _const_spec((d_ff, d)),
                  _const_spec(w_in_p.shape),
                  _const_spec(conv_w8.shape), _const_spec((1, CONV_DIM)),
                  _const_spec((1, LANES)), _const_spec((1, LANES)), _const_spec((1, SSD_WIDTH)),
                  _const_spec(tri_lo.shape)],
        out_specs=out_specs,
        out_shape=out_shape,
        scratch_shapes=[pltpu.VMEM((SEQ_ROWS, CONV_DIM), BF16), pltpu.VMEM((SEQ_ROWS, LANES), F32),
                        pltpu.VMEM((RING, SEQ_ROWS, CONV_DIM), BF16), pltpu.VMEM((RING, SEQ_ROWS, LANES), F32),
                        pltpu.VMEM((D_STATE, SSD_WIDTH), F32)],
        compiler_params=pltpu.CompilerParams(dimension_semantics=("arbitrary",),
                                             vmem_limit_bytes=VMEM_LIMIT_BYTES),
        name="ffn1_inproj_scan",
    )(x, mod, norm1, norm2, wg, wu, wd, w_in_p, conv_w8, conv_b, dtb, alog, dskip, tri_lo)


def _head_prep_t(t, w_b, cos, sin):
    ms = jnp.sum(t * t, axis=0, keepdims=True) * (1.0 / HEAD_DIM)
    tn = t * lax.rsqrt(ms + EPS) * w_b
    half = ROPE_DIMS // 2
    t1 = tn[0:half]
    t2 = tn[half:ROPE_DIMS]
    return jnp.concatenate([t1 * cos - t2 * sin, t2 * cos + t1 * sin, tn[ROPE_DIMS:]], axis=0)


def _attn_tile(q_ref, kvp_ref, kvc_ref, kvn_ref, csp_ref, csc_ref, csn_ref, qw_ref, kw_ref, sink_ref,
               has_prev, has_next, store):
    tq = q_ref.shape[1]
    nsub = tq // BLOCK
    nw = tq + 2 * BLOCK
    kvw = ATTN_KV_HEADS * HEAD_DIM
    half = ROPE_DIMS // 2
    q_per_kv = ATTN_HEADS // ATTN_KV_HEADS

    kv = jnp.concatenate([kvp_ref[0], kvc_ref[0], kvn_ref[0]], axis=0)
    cs = jnp.concatenate([csp_ref[0], csc_ref[0], csn_ref[0]], axis=1)
    k_t = kv[:, :kvw].T
    v_t = kv[:, kvw:].T.astype(BF16)
    kw_b = jnp.tile(kw_ref[...], (1, nw // LANES))
    k_prep = jnp.concatenate(
        [_head_prep_t(k_t[g * HEAD_DIM:(g + 1) * HEAD_DIM], kw_b, cs[0:half], cs[half:])
         for g in range(ATTN_KV_HEADS)], axis=0)
    keys = k_prep.T.astype(BF16)

    q_t = q_ref[0].T
    qw_b = jnp.tile(qw_ref[...] * (HEAD_DIM ** -0.5), (1, tq // LANES))
    cs_q = csc_ref[0]
    zeros = jnp.zeros((HEAD_DIM, tq), BF16)
    q_heads = []
    for h in range(ATTN_HEADS):
        qh = _head_prep_t(q_t[h * HEAD_DIM:(h + 1) * HEAD_DIM], qw_b, cs_q[0:half], cs_q[half:]).astype(BF16)
        q_heads.append(jnp.concatenate([qh, zeros] if h // q_per_kv == 0 else [zeros, qh], axis=0))

    yield
    kr = _iota2((BLOCK, BLOCK), 0)
    qc = _iota2((BLOCK, BLOCK), 1)
    bias_prev = jnp.where(kr >= qc, 0.0, NEG_BIG)
    bias_next = jnp.where(kr <= qc, 0.0, NEG_BIG)
    sink_row = jnp.concatenate([jnp.full((1, BLOCK), sink_ref[h], F32) for h in range(ATTN_HEADS)], axis=1)
    gq = q_per_kv * BLOCK
    for j in range(nsub):
        bp = bias_prev if j > 0 else jnp.where(has_prev, bias_prev, NEG_BIG)
        bn = bias_next if j < nsub - 1 else jnp.where(has_next, bias_next, NEG_BIG)
        lo, hi = j * BLOCK, (j + 3) * BLOCK
        q_all = jnp.concatenate([qh[:, lo:lo + BLOCK] for qh in q_heads], axis=1)
        s = _dot(keys[lo:hi], q_all)
        yield
        s0 = s[0:BLOCK] + jnp.tile(bp, (1, ATTN_HEADS))
        s1 = s[BLOCK:2 * BLOCK]
        s2 = s[2 * BLOCK:] + jnp.tile(bn, (1, ATTN_HEADS))
        m = jnp.maximum(jnp.max(jnp.maximum(jnp.maximum(s0, s1), s2), axis=0, keepdims=True), sink_row)
        p0 = jnp.exp(s0 - m)
        p1 = jnp.exp(s1 - m)
        p2 = jnp.exp(s2 - m)
        denom = jnp.sum(p0 + p1 + p2, axis=0, keepdims=True) + jnp.exp(sink_row - m)
        p = jnp.concatenate([p0, p1, p2], axis=0).astype(BF16)
        inv = 1.0 / denom
        outs = []
        for g in range(ATTN_KV_HEADS):
            o = _dot(v_t[g * HEAD_DIM:(g + 1) * HEAD_DIM, lo:hi], p[:, g * gq:(g + 1) * gq])
            o = o * inv[:, g * gq:(g + 1) * gq]
            outs += [o[:, k * BLOCK:(k + 1) * BLOCK] for k in range(q_per_kv)]
        store(lo, jnp.concatenate(outs, axis=0).T.astype(BF16))
        yield


def _tail_kernel(n_tiles, tiles_per_seq,
                 h_ref, mod_ref, n3_ref, wo_ref, wg_ref, wu_ref, wd_ref,
                 q_ref, kvp_ref, kvc_ref, kvn_ref, csp_ref, csc_ref, csn_ref, qw_ref, kw_ref, sink_ref,
                 bc_ref, xdtb_ref, yp_ref, dt_ref, z_ref, dtb_ref, alog_ref, nw_ref, tri_ref,
                 o_ref, y_scr, hstate_ref):
    k = pl.program_id(0)

    @pl.when(k == 0)
    def _():
        y_scr[...] = jnp.zeros_like(y_scr)
        hstate_ref[...] = jnp.zeros_like(hstate_ref)

    rd = k % 2
    wr = 1 - rd

    t = jnp.maximum(n_tiles - 1 - k, 0) % tiles_per_seq
    is_first = t == 0
    is_last = t == tiles_per_seq - 1

    def store_ssd(row0, y):
        y_scr[wr, row0:row0 + CHUNK, 0:SSD_WIDTH] = y

    def store_attn(row0, y):
        y_scr[wr, row0:row0 + BLOCK, SSD_WIDTH:] = y

    attn = _attn_tile(q_ref, kvp_ref, kvc_ref, kvn_ref, csp_ref, csc_ref, csn_ref, qw_ref, kw_ref, sink_ref,
                      jnp.logical_not(is_first), jnp.logical_not(is_last), store_attn)
    scan = _ssd_bwd_tile(bc_ref, xdtb_ref, yp_ref, dt_ref, z_ref, dtb_ref, alog_ref, nw_ref, tri_ref,
                         hstate_ref, is_last, store_ssd)

    h1 = h_ref[0]
    mod = mod_ref[0]
    mix = _dot(y_scr[rd], wo_ref[...])
    next(scan, None)
    h2 = h1 + (1.0 + mod[5:6]) * mix
    u = _ada_norm(h2, n3_ref[...] * (1.0 + mod[7:8]), mod[6:7]).astype(BF16)
    ff = _swiglu(u, wg_ref, wu_ref, wd_ref, side_work=(attn, scan))
    o_ref[0] = h2 + (0.5 * (1.0 + mod[8:9])) * ff


def _tail(h1, mod, norm3, w_out, wg, wu, wd, q, kv, cs_tab, qw_b, kw_b, sink,
          bc, xdtb, ypart, dt, z, dtb, alog, norm_w):
    b, s, d = h1.shape
    d_ff = wg.shape[1]
    tps = s // SEQ_ROWS
    n_tiles = b * tps
    nsub = SEQ_ROWS // BLOCK
    nb = s // BLOCK
    aw = q.shape[-1]
    kvw2 = kv.shape[-1]
    cs_rows = cs_tab.shape[1]
    _, tri_up = _ssd_constants()

    ffn_tile = lambda k: jnp.clip(n_tiles - k, 0, n_tiles - 1)
    mix_tile = lambda k: jnp.maximum(n_tiles - 1 - k, 0)
    prev_blk = lambda t: jnp.maximum(t * nsub - 1, 0)
    next_blk = lambda t: jnp.minimum((t + 1) * nsub, nb - 1)

    def ffn_tok(w):
        return pl.BlockSpec((1, SEQ_ROWS, w), lambda k: (ffn_tile(k) // tps, ffn_tile(k) % tps, 0))

    def mix_tok(w):
        return pl.BlockSpec((1, SEQ_ROWS, w), lambda k: (mix_tile(k) // tps, mix_tile(k) % tps, 0))

    in_specs = [
        ffn_tok(d),
        pl.BlockSpec((1, N_MOD, d), lambda k: (ffn_tile(k) // tps, 0, 0)),
        _const_spec((1, d)), _const_spec(w_out.shape),
        _const_spec((d, d_ff)), _const_spec((d, d_ff)), _const_spec((d_ff, d)),
        mix_tok(aw),
        pl.BlockSpec((1, BLOCK, kvw2), lambda k: (mix_tile(k) // tps, prev_blk(mix_tile(k) % tps), 0)),
        mix_tok(kvw2),
        pl.BlockSpec((1, BLOCK, kvw2), lambda k: (mix_tile(k) // tps, next_blk(mix_tile(k) % tps), 0)),
        pl.BlockSpec((1, cs_rows, BLOCK), lambda k: (mix_tile(k) // tps, 0, prev_blk(mix_tile(k) % tps))),
        pl.BlockSpec((1, cs_rows, SEQ_ROWS), lambda k: (mix_tile(k) // tps, 0, mix_tile(k) % tps)),
        pl.BlockSpec((1, cs_rows, BLOCK), lambda k: (mix_tile(k) // tps, 0, next_blk(mix_tile(k) % tps))),
        _const_spec((HEAD_DIM, LANES)), _const_spec((HEAD_DIM, LANES)),
        pl.BlockSpec(memory_space=pltpu.SMEM),
        mix_tok(BC_WIDTH), mix_tok(SSD_WIDTH), mix_tok(SSD_WIDTH), mix_tok(LANES), mix_tok(SSD_WIDTH),
        _const_spec((1, LANES)), _const_spec((1, LANES)), _const_spec((1, SSD_WIDTH)),
        _const_spec(tri_up.shape),
    ]
    return pl.pallas_call(
        functools.partial(_tail_kernel, n_tiles, tps),
        grid=(n_tiles + 1,),
        in_specs=in_specs,
        out_specs=ffn_tok(d),
        out_shape=jax.ShapeDtypeStruct((b, s, d), F32),
        scratch_shapes=[pltpu.VMEM((2, SEQ_ROWS, SSD_WIDTH + aw), BF16),
                        pltpu.VMEM((D_STATE, SSD_WIDTH), F32)],
        compiler_params=pltpu.CompilerParams(dimension_semantics=("arbitrary",),
                                             vmem_limit_bytes=VMEM_LIMIT_BYTES),
        name="mixers_outproj_ffn2",
    )(h1, mod, norm3, w_out, wg, wu, wd, q, kv, kv, kv, cs_tab, cs_tab, cs_tab, qw_b, kw_b, sink,
      bc, xdtb, ypart, dt, z, dtb, alog, norm_w, tri_up)


def _pad_inproj(w_in):
    n_dt = 2 * SSD_HEADS
    s_xbc = SSD_WIDTH + CONV_DIM
    s_dt = s_xbc + n_dt
    dt_cols = jnp.pad(w_in[:, s_xbc:s_dt], ((0, 0), (0, LANES - n_dt)))
    return jnp.concatenate([w_in[:, :s_xbc], w_in[:, s_dt:], dt_cols], axis=1).astype(BF16)


def _pad_lanes(v, width=LANES):
    v = v.reshape(1, -1)
    return jnp.pad(v, ((0, 0), (0, width - v.shape[1])))


def _lane_bcast(v):
    return jnp.broadcast_to(v[:, None], (v.shape[0], LANES))


def kernel(x, c, positions, w_ada, b_ada, norm_ffn1, ffn1_wg, ffn1_wu, ffn1_wd, norm_mix, w_in, conv_w,
           conv_b, dt_bias, a_log, d_skip, ssd_norm_w, q_norm_w, k_norm_w, sink_logit, w_out, norm_ffn2,
           ffn2_wg, ffn2_wu, ffn2_wd):
    depth = w_ada.shape[0]
    b, s, d = x.shape
    h = x.astype(F32)
    c_pad = jnp.pad(c.astype(F32), ((0, -b % SUBLANES), (0, 0)))
    cs_tab = _rope_table(positions)
    for l in range(depth):
        mod = _adaln_mod(c_pad, w_ada[l], b_ada[l])[:b].reshape(b, N_MOD, d)
        conv_w8 = jnp.pad(conv_w[l], ((0, SUBLANES - CONV_K), (0, 0)))
        dtb, alog = _pad_lanes(dt_bias[l]), _pad_lanes(a_log[l])
        h1, z, q, kv, dt, bc, xdtb, ypart = _head(
            h, mod, norm_ffn1[l].reshape(1, d), norm_mix[l].reshape(1, d),
            ffn1_wg[l].astype(BF16), ffn1_wu[l].astype(BF16), ffn1_wd[l].astype(BF16), _pad_inproj(w_in[l]),
            conv_w8, conv_b[l].reshape(1, -1), dtb, alog, jnp.repeat(d_skip[l], SSD_HEAD_DIM).reshape(1, -1))
        h = _tail(h1, mod, norm_ffn2[l].reshape(1, d), w_out[l].astype(BF16),
                  ffn2_wg[l].astype(BF16), ffn2_wu[l].astype(BF16), ffn2_wd[l].astype(BF16),
                  q, kv, cs_tab, _lane_bcast(q_norm_w[l]), _lane_bcast(k_norm_w[l]), sink_logit[l],
                  bc, xdtb, ypart, dt, z, dtb, alog, ssd_norm_w[l].reshape(1, -1))
    return h.astype(x.dtype)
```

```python
import functools

import jax
import jax.numpy as jnp
import numpy as np
from jax import lax
from jax.experimental import pallas as pl
from jax.experimental.pallas import tpu as pltpu

F32 = jnp.float32
BF16 = jnp.bfloat16

SSD_HEAD_DIM = 64
SSD_HEADS = 8
SSD_GROUPS = 2
D_STATE = 128
CONV_K = 5
CHUNK = 128
HEAD_DIM = 64
ATTN_HEADS = 8
ATTN_KV_HEADS = 2
WINDOW = 128
BLOCK = 128
ROPE_DIMS = 16
ROPE_THETA = 500000.0
N_MOD = 9
EPS = 1e-6

SSD_WIDTH = SSD_HEADS * SSD_HEAD_DIM
BC_WIDTH = 2 * SSD_GROUPS * D_STATE
CONV_DIM = SSD_WIDTH + BC_WIDTH

LANES = 128
SUBLANES = 8
VMEM_LIMIT_BYTES = 56 * 1024 * 1024

SEQ_ROWS = 512
FFN_CHUNK = 256
CONV_HALO = 16

NEG_BIG = -1e30


def _dot(a, b):
    return jnp.dot(a, b, preferred_element_type=F32)


def _dot_nt(a, b):
    return lax.dot_general(a, b, (((1,), (1,)), ((), ())), preferred_element_type=F32)


def _row_sums(tri_bf16, x, terms=3):
    acc = None
    r = x
    for t in range(terms):
        h = r.astype(BF16)
        d = _dot(tri_bf16, h)
        acc = d if acc is None else acc + d
        if t + 1 < terms:
            r = r - h.astype(F32)
    return acc


def _silu(x):
    return x * jax.nn.sigmoid(x)


def _iota2(shape, dim):
    return lax.broadcasted_iota(jnp.int32, shape, dim)


def _const_spec(shape):
    nd = len(shape)
    return pl.BlockSpec(shape, lambda *_: (0,) * nd, pipeline_mode=pl.Buffered(1))


def _mod_kernel(c_ref, w_ref, b_ref, o_ref):
    cs = _silu(c_ref[...])
    o_ref[...] = _dot(cs, w_ref[...]) + b_ref[...]


def _adaln_mod(c_pad, w_ada, b_ada):
    rows, d = c_pad.shape
    n = w_ada.shape[1]
    bn = d
    return pl.pallas_call(
        _mod_kernel,
        grid=(n // bn,),
        in_specs=[pl.BlockSpec((rows, d), lambda j: (0, 0)),
                  pl.BlockSpec((d, bn), lambda j: (0, j)),
                  pl.BlockSpec((1, bn), lambda j: (0, j))],
        out_specs=pl.BlockSpec((rows, bn), lambda j: (0, j)),
        out_shape=jax.ShapeDtypeStruct((rows, n), F32),
        name="adaln_mod",
    )(c_pad, w_ada, b_ada.reshape(1, n))


def _rope_kernel(pos_ref, inv_ref, o_ref):
    half = ROPE_DIMS // 2
    s = pos_ref.shape[-1]
    p = pos_ref[0].astype(F32)
    ang = jnp.tile(inv_ref[...], (1, s // LANES)) * p
    o_ref[0, 0:half, :] = jnp.cos(ang)
    o_ref[0, half:, :] = jnp.sin(ang)


def _rope_table(positions):
    half = ROPE_DIMS // 2
    inv = ROPE_THETA ** (-jnp.arange(half, dtype=F32) * 2.0 / ROPE_DIMS)
    b, s = positions.shape
    return pl.pallas_call(
        _rope_kernel,
        grid=(b,),
        in_specs=[pl.BlockSpec((1, 1, s), lambda bi: (bi, 0, 0)), _const_spec((half, LANES))],
        out_specs=pl.BlockSpec((1, 2 * half, s), lambda bi: (bi, 0, 0)),
        out_shape=jax.ShapeDtypeStruct((b, 2 * half, s), F32),
        name="rope_table",
    )(positions.reshape(b, 1, s), jnp.broadcast_to(inv[:, None], (half, LANES)))


def _ada_norm(x, gain_scale, shift):
    ms = jnp.mean(x * x, axis=-1, keepdims=True)
    return x * lax.rsqrt(ms + EPS) * gain_scale + shift


def _swiglu(ub, wg_ref, wu_ref, wd_ref, side_work=(), drain=True):
    d_ff = wg_ref.shape[1]
    acc = None
    for c0 in range(0, d_ff, FFN_CHUNK):
        c1 = min(c0 + FFN_CHUNK, d_ff)
        g = _dot(ub, wg_ref[:, c0:c1])
        up = _dot(ub, wu_ref[:, c0:c1])
        a = (_silu(g) * up).astype(BF16)
        d = _dot(a, wd_ref[c0:c1, :])
        acc = d if acc is None else acc + d
        for gen in side_work:
            next(gen, None)
    for gen in side_work if drain else ():
        for _ in gen:
            pass
    return acc


_INPROJ_LAYOUT = (("z", SSD_WIDTH, F32), ("xbc", CONV_DIM, BF16), ("q", ATTN_HEADS * HEAD_DIM, F32),
                  ("kv", 2 * ATTN_KV_HEADS * HEAD_DIM, F32), ("dt", LANES, F32))


def _ssd_constants():
    r = np.arange(CHUNK)[:, None]
    c = np.arange(CHUNK)[None, :]
    return jnp.asarray(c <= r, dtype=BF16), jnp.asarray(c >= r, dtype=BF16)


def _dt_and_da(dt_raw, dtb_ref, alog_ref):
    lane = _iota2(dt_raw.shape, 1)
    dt = jnp.where(lane < 2 * SSD_HEADS, jax.nn.softplus(dt_raw + dtb_ref[...]), 0.0)
    a = -jnp.exp(alog_ref[...])
    return dt, dt * a


def _head_columns(x, lane0):
    return [jnp.broadcast_to(x[:, lane0 + h:lane0 + h + 1], x.shape) for h in range(SSD_HEADS)]


def _expand_heads(cols):
    first = _iota2(cols[0].shape, 1) < SSD_HEAD_DIM
    return jnp.concatenate([jnp.where(first, cols[2 * p], cols[2 * p + 1]) for p in range(SSD_HEADS // 2)],
                           axis=1)


def _ssd_chunk(direction, da, tri_ref, cm, bm, xdt_fn, h_ref):
    gw = SSD_WIDTH // SSD_GROUPS
    lane0 = 0 if direction == "fwd" else SSD_HEADS
    tot_row = CHUNK - 1 if direction == "fwd" else 0
    cgs = [cm[:, g * D_STATE:(g + 1) * D_STATE] for g in range(SSD_GROUPS)]
    bgs = [bm[:, g * D_STATE:(g + 1) * D_STATE] for g in range(SSD_GROUPS)]
    hgs = [h_ref[:, g * gw:(g + 1) * gw] for g in range(SSD_GROUPS)]

    cs = _row_sums(tri_ref[...], da)
    cbs = [_dot_nt(cgs[g], bgs[g]) for g in range(SSD_GROUPS)]
    chs = [_dot(cgs[g], hgs[g].astype(BF16)) for g in range(SSD_GROUPS)]
    yield

    xdt = xdt_fn()
    cs_cols = _head_columns(cs, lane0)
    csx = _expand_heads(cs_cols)
    tot = csx[tot_row:tot_row + 1, :]
    e_in = jnp.exp(csx)
    xdec = (xdt * jnp.exp(tot - csx)).astype(BF16)
    cdec = jnp.exp(tot)
    cst = cs.T
    row = _iota2((CHUNK, CHUNK), 0)
    col = _iota2((CHUNK, CHUNK), 1)
    keep = (col <= row) if direction == "fwd" else (col > row)
    half = _iota2((CHUNK, LANES), 1) // SSD_HEAD_DIM
    xdt_b = xdt.astype(BF16)
    ys = []
    for g in range(SSD_GROUPS):
        pairs = []
        for pp in range(gw // LANES):
            p = g * (gw // LANES) + pp
            xp = xdt_b[:, p * LANES:(p + 1) * LANES]
            yp = None
            for e in range(2):
                hcol = lane0 + 2 * p + e
                seg = cs_cols[2 * p + e] - cst[hcol:hcol + 1, :]
                m = (cbs[g] * jnp.exp(jnp.where(keep, seg, NEG_BIG))).astype(BF16)
                d = _dot(m, jnp.where(half == e, xp, jnp.zeros_like(xp)))
                yp = d if yp is None else yp + d
            pairs.append(yp)
        ys.append(jnp.concatenate(pairs, axis=1) + chs[g] * e_in[:, g * gw:(g + 1) * gw])
        bgt = bgs[g].astype(F32).T.astype(BF16)
        st = _dot(bgt, xdec[:, g * gw:(g + 1) * gw])
        h_ref[:, g * gw:(g + 1) * gw] = hgs[g] * cdec[:, g * gw:(g + 1) * gw] + st
    return jnp.concatenate(ys, axis=1)


def _ssd_fwd_tile(window_fn, dt_fn, cw_ref, cbias_ref, dtb_ref, alog_ref, dskip_ref,
                  tri_ref, bc_ref, xdtb_ref, yp_ref, h_ref):
    mid = CONV_K // 2
    for j in range(SEQ_ROWS // CHUNK):
        rows = pl.ds(j * CHUNK, CHUNK)
        win = window_fn(j)
        dt, da = _dt_and_da(dt_fn(j), dtb_ref, alog_ref)
        dtx_f = _expand_heads(_head_columns(dt, 0))
        dtx_b = _expand_heads(_head_columns(dt, SSD_HEADS))
        yield
        winf = win.astype(F32)
        acc = cbias_ref[...] + winf[CONV_HALO:CONV_HALO + CHUNK] * cw_ref[mid:mid + 1, :]
        for k in [k for k in range(CONV_K) if k != mid]:
            rolled = pltpu.roll(winf, (mid - k) % winf.shape[0], axis=0)
            acc = acc + rolled[CONV_HALO:CONV_HALO + CHUNK] * cw_ref[k:k + 1, :]
        act = _silu(acc)
        xs = act[:, :SSD_WIDTH]
        bc = act[:, SSD_WIDTH:].astype(BF16)
        xdtb_ref[0, rows, :] = (xs * dtx_b).astype(BF16)
        bc_ref[0, rows, :] = bc
        y = yield from _ssd_chunk("fwd", da, tri_ref, bc[:, SSD_GROUPS * D_STATE:],
                                  bc[:, :SSD_GROUPS * D_STATE], lambda: xs * dtx_f, h_ref)
        yp_ref[0, rows, :] = y + xs * dskip_ref[...]
        yield


def _ssd_bwd_tile(bc_ref, xdtb_ref, yp_ref, dt_ref, z_ref, dtb_ref, alog_ref, nw_ref, tri_ref,
                  h_ref, reset, store):
    h_ref[...] = jnp.where(reset, 0.0, h_ref[...])
    for j in reversed(range(bc_ref.shape[1] // CHUNK)):
        rows = pl.ds(j * CHUNK, CHUNK)
        bc = bc_ref[0, rows, :]
        _, da = _dt_and_da(dt_ref[0, rows, :], dtb_ref, alog_ref)
        y = yield from _ssd_chunk("bwd", da, tri_ref, bc[:, SSD_GROUPS * D_STATE:],
                                  bc[:, :SSD_GROUPS * D_STATE],
                                  lambda: xdtb_ref[0, rows, :].astype(F32), h_ref)
        y = (y + yp_ref[0, rows, :]) * _silu(z_ref[0, rows, :])
        ms = jnp.mean(y * y, axis=-1, keepdims=True)
        store(j * CHUNK, (y * lax.rsqrt(ms + EPS) * nw_ref[...]).astype(BF16))
        yield


RING = 3


def _head_kernel(n_tiles, tiles_per_seq,
                 x_ref, mod_ref, n1_ref, n2_ref, wg_ref, wu_ref, wd_ref, win_ref,
                 cw_ref, cbias_ref, dtb_ref, alog_ref, dskip_ref, tri_ref,
                 h_ref, z_ref, q_ref, kv_ref, dt_ref, bc_ref, xdtb_ref, yp_ref,
                 xstage, dtstage, xring, dtring, hstate_ref):
    k = pl.program_id(0)

    @pl.when(k == 0)
    def _():
        xstage[...] = jnp.zeros_like(xstage)
        dtstage[...] = jnp.zeros_like(dtstage)
        xring[...] = jnp.zeros_like(xring)
        dtring[...] = jnp.zeros_like(dtring)
        hstate_ref[...] = jnp.zeros_like(hstate_ref)

    t = jnp.maximum(k - 2, 0) % tiles_per_seq
    s_new = (k + RING - 1) % RING
    s_main = (k + RING - 2) % RING
    s_prev = k % RING
    has_prev = t > 0
    has_next = t < tiles_per_seq - 1

    def advance_ring():
        xring[s_new] = xstage[...]
        dtring[s_new] = dtstage[...]
        hstate_ref[...] = jnp.where(has_prev, hstate_ref[...], 0.0)

    def window(j):
        lo = j * CHUNK - CONV_HALO
        hi = (j + 1) * CHUNK + CONV_HALO
        parts = []
        if lo < 0:
            halo = xring[s_prev, SEQ_ROWS + lo:, :]
            parts.append(jnp.where(has_prev, halo, jnp.zeros_like(halo)))
        parts.append(xring[s_main, max(lo, 0):min(hi, SEQ_ROWS), :])
        if hi > SEQ_ROWS:
            halo = xring[s_new, 0:hi - SEQ_ROWS, :]
            parts.append(jnp.where(has_next, halo, jnp.zeros_like(halo)))
        return parts[0] if len(parts) == 1 else jnp.concatenate(parts, axis=0)

    def make_scan():
        return _ssd_fwd_tile(window, lambda j: dtring[s_main, j * CHUNK:(j + 1) * CHUNK, :],
                             cw_ref, cbias_ref, dtb_ref, alog_ref, dskip_ref, tri_ref,
                             bc_ref, xdtb_ref, yp_ref, hstate_ref)

    @pl.when(k < n_tiles)
    def _():
        advance_ring()
        scan = make_scan()
        x = x_ref[0]
        mod = mod_ref[0]
        u = _ada_norm(x, n1_ref[...] * (1.0 + mod[1:2]), mod[0:1]).astype(BF16)
        ff = _swiglu(u, wg_ref, wu_ref, wd_ref, side_work=(scan,), drain=False)
        h = x + (0.5 * (1.0 + mod[2:3])) * ff
        h_ref[0] = h
        u2 = _ada_norm(h, n2_ref[...] * (1.0 + mod[4:5]), mod[3:4]).astype(BF16)
        dests = {"z": (z_ref,), "xbc": (xstage,), "q": (q_ref,), "kv": (kv_ref,), "dt": (dt_ref, dtstage)}
        c0 = 0
        for name, w, _ in _INPROJ_LAYOUT:
            piece = _dot(u2, win_ref[:, c0:c0 + w])
            for ref in dests[name]:
                if len(ref.shape) == 3:
                    ref[0] = piece.astype(ref.dtype)
                else:
                    ref[...] = piece.astype(ref.dtype)
            c0 += w
            next(scan, None)
        for _ in scan:
            pass

    @pl.when(k >= n_tiles)
    def _():
        advance_ring()
        for _ in make_scan():
            pass


def _head(x, mod, norm1, norm2, wg, wu, wd, w_in_p, conv_w8, conv_b, dtb, alog, dskip):
    b, s, d = x.shape
    d_ff = wg.shape[1]
    tps = s // SEQ_ROWS
    n_tiles = b * tps
    tri_lo, _ = _ssd_constants()
    ffn_tile = lambda k: jnp.minimum(k, n_tiles - 1)
    scan_tile = lambda k: jnp.maximum(k - 2, 0)

    def ffn_tok(w):
        return pl.BlockSpec((1, SEQ_ROWS, w), lambda k: (ffn_tile(k) // tps, ffn_tile(k) % tps, 0))

    def scan_tok(w):
        return pl.BlockSpec((1, SEQ_ROWS, w), lambda k: (scan_tile(k) // tps, scan_tile(k) % tps, 0))

    widths = {name: (w, dt) for name, w, dt in _INPROJ_LAYOUT}
    hbm_outs = ("z", "q", "kv", "dt")
    out_shape = [jax.ShapeDtypeStruct((b, s, d), F32)]
    out_shape += [jax.ShapeDtypeStruct((b, s, widths[n][0]), widths[n][1]) for n in hbm_outs]
    out_shape += [jax.ShapeDtypeStruct((b, s, BC_WIDTH), BF16), jax.ShapeDtypeStruct((b, s, SSD_WIDTH), BF16),
                  jax.ShapeDtypeStruct((b, s, SSD_WIDTH), F32)]
    out_specs = [ffn_tok(d)] + [ffn_tok(widths[n][0]) for n in hbm_outs]
    out_specs += [scan_tok(BC_WIDTH), scan_tok(SSD_WIDTH), scan_tok(SSD_WIDTH)]
    return pl.pallas_call(
        functools.partial(_head_kernel, n_tiles, tps),
        grid=(n_tiles + 2,),
        in_specs=[ffn_tok(d),
                  pl.BlockSpec((1, N_MOD, d), lambda k: (ffn_tile(k) // tps, 0, 0)),
                  _const_spec((1, d)), _const_spec((1, d)),
                  _const_spec((d, d_ff)), _const_spec((d, d_ff)), _const_spec((d_ff, d)),
                  _const_spec(w_in_p.shape),
                  _const_spec(conv_w8.shape), _const_spec((1, CONV_DIM)),
                  _const_spec((1, LANES)), _const_spec((1, LANES)), _const_spec((1, SSD_WIDTH)),
                  _const_spec(tri_lo.shape)],
        out_specs=out_specs,
        out_shape=out_shape,
        scratch_shapes=[pltpu.VMEM((SEQ_ROWS, CONV_DIM), BF16), pltpu.VMEM((SEQ_ROWS, LANES), F32),
                        pltpu.VMEM((RING, SEQ_ROWS, CONV_DIM), BF16), pltpu.VMEM((RING, SEQ_ROWS, LANES), F32),
                        pltpu.VMEM((D_STATE, SSD_WIDTH), F32)],
        compiler_params=pltpu.CompilerParams(dimension_semantics=("arbitrary",),
                                             vmem_limit_bytes=VMEM_LIMIT_BYTES),
        name="ffn1_inproj_scan",
    )(x, mod, norm1, norm2, wg, wu, wd, w_in_p, conv_w8, conv_b, dtb, alog, dskip, tri_lo)


def _head_prep_t(t, w_b, cos, sin):
    ms = jnp.sum(t * t, axis=0, keepdims=True) * (1.0 / HEAD_DIM)
    tn = t * lax.rsqrt(ms + EPS) * w_b
    half = ROPE_DIMS // 2
    t1 = tn[0:half]
    t2 = tn[half:ROPE_DIMS]
    return jnp.concatenate([t1 * cos - t2 * sin, t2 * cos + t1 * sin, tn[ROPE_DIMS:]], axis=0)


def _attn_tile(q_ref, kvp_ref, kvc_ref, kvn_ref, csp_ref, csc_ref, csn_ref, qw_ref, kw_ref, sink_ref,
               has_prev, has_next, store):
    tq = q_ref.shape[1]
    nsub = tq // BLOCK
    nw = tq + 2 * BLOCK
    kvw = ATTN_KV_HEADS * HEAD_DIM
    half = ROPE_DIMS // 2
    q_per_kv = ATTN_HEADS // ATTN_KV_HEADS

    kv = jnp.concatenate([kvp_ref[0], kvc_ref[0], kvn_ref[0]], axis=0)
    cs = jnp.concatenate([csp_ref[0], csc_ref[0], csn_ref[0]], axis=1)
    k_t = kv[:, :kvw].T
    v_t = kv[:, kvw:].T.astype(BF16)
    kw_b = jnp.tile(kw_ref[...], (1, nw // LANES))
    k_prep = jnp.concatenate(
        [_head_prep_t(k_t[g * HEAD_DIM:(g + 1) * HEAD_DIM], kw_b, cs[0:half], cs[half:])
         for g in range(ATTN_KV_HEADS)], axis=0)
    keys = k_prep.T.astype(BF16)

    q_t = q_ref[0].T
    qw_b = jnp.tile(qw_ref[...] * (HEAD_DIM ** -0.5), (1, tq // LANES))
    cs_q = csc_ref[0]
    zeros = jnp.zeros((HEAD_DIM, tq), BF16)
    q_heads = []
    for h in range(ATTN_HEADS):
        qh = _head_prep_t(q_t[h * HEAD_DIM:(h + 1) * HEAD_DIM], qw_b, cs_q[0:half], cs_q[half:]).astype(BF16)
        q_heads.append(jnp.concatenate([qh, zeros] if h // q_per_kv == 0 else [zeros, qh], axis=0))

    yield
    kr = _iota2((BLOCK, BLOCK), 0)
    qc = _iota2((BLOCK, BLOCK), 1)
    bias_prev = jnp.where(kr >= qc, 0.0, NEG_BIG)
    bias_next = jnp.where(kr <= qc, 0.0, NEG_BIG)
    sink_row = jnp.concatenate([jnp.full((1, BLOCK), sink_ref[h], F32) for h in range(ATTN_HEADS)], axis=1)
    gq = q_per_kv * BLOCK
    for j in range(nsub):
        bp = bias_prev if j > 0 else jnp.where(has_prev, bias_prev, NEG_BIG)
        bn = bias_next if j < nsub - 1 else jnp.where(has_next, bias_next, NEG_BIG)
        lo, hi = j * BLOCK, (j + 3) * BLOCK
        q_all = jnp.concatenate([qh[:, lo:lo + BLOCK] for qh in q_heads], axis=1)
        s = _dot(keys[lo:hi], q_all)
        yield
        s0 = s[0:BLOCK] + jnp.tile(bp, (1, ATTN_HEADS))
        s1 = s[BLOCK:2 * BLOCK]
        s2 = s[2 * BLOCK:] + jnp.tile(bn, (1, ATTN_HEADS))
        m = jnp.maximum(jnp.max(jnp.maximum(jnp.maximum(s0, s1), s2), axis=0, keepdims=True), sink_row)
        p0 = jnp.exp(s0 - m)
        p1 = jnp.exp(s1 - m)
        p2 = jnp.exp(s2 - m)
        denom = jnp.sum(p0 + p1 + p2, axis=0, keepdims=True) + jnp.exp(sink_row - m)
        p = jnp.concatenate([p0, p1, p2], axis=0).astype(BF16)
        inv = 1.0 / denom
        outs = []
        for g in range(ATTN_KV_HEADS):
            o = _dot(v_t[g * HEAD_DIM:(g + 1) * HEAD_DIM, lo:hi], p[:, g * gq:(g + 1) * gq])
            o = o * inv[:, g * gq:(g + 1) * gq]
            outs += [o[:, k * BLOCK:(k + 1) * BLOCK] for k in range(q_per_kv)]
        store(lo, jnp.concatenate(outs, axis=0).T.astype(BF16))
        yield


def _tail_kernel(n_tiles, tiles_per_seq,
                 h_ref, mod_ref, n3_ref, wo_ref, wg_ref, wu_ref, wd_ref,
                 q_ref, kvp_ref, kvc_ref, kvn_ref, csp_ref, csc_ref, csn_ref, qw_ref, kw_ref, sink_ref,
                 bc_ref, xdtb_ref, yp_ref, dt_ref, z_ref, dtb_ref, alog_ref, nw_ref, tri_ref,
                 o_ref, y_scr, hstate_ref):
    k = pl.program_id(0)

    @pl.when(k == 0)
    def _():
        y_scr[...] = jnp.zeros_like(y_scr)
        hstate_ref[...] = jnp.zeros_like(hstate_ref)

    rd = k % 2
    wr = 1 - rd

    t = jnp.maximum(n_tiles - 1 - k, 0) % tiles_per_seq
    is_first = t == 0
    is_last = t == tiles_per_seq - 1

    def store_ssd(row0, y):
        y_scr[wr, row0:row0 + CHUNK, 0:SSD_WIDTH] = y

    def store_attn(row0, y):
        y_scr[wr, row0:row0 + BLOCK, SSD_WIDTH:] = y

    attn = _attn_tile(q_ref, kvp_ref, kvc_ref, kvn_ref, csp_ref, csc_ref, csn_ref, qw_ref, kw_ref, sink_ref,
                      jnp.logical_not(is_first), jnp.logical_not(is_last), store_attn)
    scan = _ssd_bwd_tile(bc_ref, xdtb_ref, yp_ref, dt_ref, z_ref, dtb_ref, alog_ref, nw_ref, tri_ref,
                         hstate_ref, is_last, store_ssd)

    h1 = h_ref[0]
    mod = mod_ref[0]
    mix = _dot(y_scr[rd], wo_ref[...])
    next(scan, None)
    h2 = h1 + (1.0 + mod[5:6]) * mix
    u = _ada_norm(h2, n3_ref[...] * (1.0 + mod[7:8]), mod[6:7]).astype(BF16)
    ff = _swiglu(u, wg_ref, wu_ref, wd_ref, side_work=(attn, scan))
    o_ref[0] = h2 + (0.5 * (1.0 + mod[8:9])) * ff


def _tail(h1, mod, norm3, w_out, wg, wu, wd, q, kv, cs_tab, qw_b, kw_b, sink,
          bc, xdtb, ypart, dt, z, dtb, alog, norm_w):
    b, s, d = h1.shape
    d_ff = wg.shape[1]
    tps = s // SEQ_ROWS
    n_tiles = b * tps
    nsub = SEQ_ROWS // BLOCK
    nb = s // BLOCK
    aw = q.shape[-1]
    kvw2 = kv.shape[-1]
    cs_rows = cs_tab.shape[1]
    _, tri_up = _ssd_constants()

    ffn_tile = lambda k: jnp.clip(n_tiles - k, 0, n_tiles - 1)
    mix_tile = lambda k: jnp.maximum(n_tiles - 1 - k, 0)
    prev_blk = lambda t: jnp.maximum(t * nsub - 1, 0)
    next_blk = lambda t: jnp.minimum((t + 1) * nsub, nb - 1)

    def ffn_tok(w):
        return pl.BlockSpec((1, SEQ_ROWS, w), lambda k: (ffn_tile(k) // tps, ffn_tile(k) % tps, 0))

    def mix_tok(w):
        return pl.BlockSpec((1, SEQ_ROWS, w), lambda k: (mix_tile(k) // tps, mix_tile(k) % tps, 0))

    in_specs = [
        ffn_tok(d),
        pl.BlockSpec((1, N_MOD, d), lambda k: (ffn_tile(k) // tps, 0, 0)),
        _const_spec((1, d)), _const_spec(w_out.shape),
        _const_spec((d, d_ff)), _const_spec((d, d_ff)), _const_spec((d_ff, d)),
        mix_tok(aw),
        pl.BlockSpec((1, BLOCK, kvw2), lambda k: (mix_tile(k) // tps, prev_blk(mix_tile(k) % tps), 0)),
        mix_tok(kvw2),
        pl.BlockSpec((1, BLOCK, kvw2), lambda k: (mix_tile(k) // tps, next_blk(mix_tile(k) % tps), 0)),
        pl.BlockSpec((1, cs_rows, BLOCK), lambda k: (mix_tile(k) // tps, 0, prev_blk(mix_tile(k) % tps))),
        pl.BlockSpec((1, cs_rows, SEQ_ROWS), lambda k: (mix_tile(k) // tps, 0, mix_tile(k) % tps)),
        pl.BlockSpec((1, cs_rows, BLOCK), lambda k: (mix_tile(k) // tps, 0, next_blk(mix_tile(k) % tps))),
        _const_spec((HEAD_DIM, LANES)), _const_spec((HEAD_DIM, LANES)),
        pl.BlockSpec(memory_space=pltpu.SMEM),
        mix_tok(BC_WIDTH), mix_tok(SSD_WIDTH), mix_tok(SSD_WIDTH), mix_tok(LANES), mix_tok(SSD_WIDTH),
        _const_spec((1, LANES)), _const_spec((1, LANES)), _const_spec((1, SSD_WIDTH)),
        _const_spec(tri_up.shape),
    ]
    return pl.pallas_call(
        functools.partial(_tail_kernel, n_tiles, tps),
        grid=(n_tiles + 1,),
        in_specs=in_specs,
        out_specs=ffn_tok(d),
        out_shape=jax.ShapeDtypeStruct((b, s, d), F32),
        scratch_shapes=[pltpu.VMEM((2, SEQ_ROWS, SSD_WIDTH + aw), BF16),
                        pltpu.VMEM((D_STATE, SSD_WIDTH), F32)],
        compiler_params=pltpu.CompilerParams(dimension_semantics=("arbitrary",),
                                             vmem_limit_bytes=VMEM_LIMIT_BYTES),
        name="mixers_outproj_ffn2",
    )(h1, mod, norm3, w_out, wg, wu, wd, q, kv, kv, kv, cs_tab, cs_tab, cs_tab, qw_b, kw_b, sink,
      bc, xdtb, ypart, dt, z, dtb, alog, norm_w, tri_up)


def _pad_inproj(w_in):
    n_dt = 2 * SSD_HEADS
    s_xbc = SSD_WIDTH + CONV_DIM
    s_dt = s_xbc + n_dt
    dt_cols = jnp.pad(w_in[:, s_xbc:s_dt], ((0, 0), (0, LANES - n_dt)))
    return jnp.concatenate([w_in[:, :s_xbc], w_in[:, s_dt:], dt_cols], axis=1).astype(BF16)


def _pad_lanes(v, width=LANES):
    v = v.reshape(1, -1)
    return jnp.pad(v, ((0, 0), (0, width - v.shape[1])))


def _lane_bcast(v):
    return jnp.broadcast_to(v[:, None], (v.shape[0], LANES))


def kernel(x, c, positions, w_ada, b_ada, norm_ffn1, ffn1_wg, ffn1_wu, ffn1_wd, norm_mix, w_in, conv_w,
           conv_b, dt_bias, a_log, d_skip, ssd_norm_w, q_norm_w, k_norm_w, sink_logit, w_out, norm_ffn2,
           ffn2_wg, ffn2_wu, ffn2_wd):
    depth = w_ada.shape[0]
    b, s, d = x.shape
    h = x.astype(F32)
    c_pad = jnp.pad(c.astype(F32), ((0, -b % SUBLANES), (0, 0)))
    cs_tab = _rope_table(positions)
    for l in range(depth):
        mod = _adaln_mod(c_pad, w_ada[l], b_ada[l])[:b].reshape(b, N_MOD, d)
        conv_w8 = jnp.pad(conv_w[l], ((0, SUBLANES - CONV_K), (0, 0)))
        dtb, alog = _pad_lanes(dt_bias[l]), _pad_lanes(a_log[l])
        h1, z, q, kv, dt, bc, xdtb, ypart = _head(
            h, mod, norm_ffn1[l].reshape(1, d), norm_mix[l].reshape(1, d),
            ffn1_wg[l].astype(BF16), ffn1_wu[l].astype(BF16), ffn1_wd[l].astype(BF16), _pad_inproj(w_in[l]),
            conv_w8, conv_b[l].reshape(1, -1), dtb, alog, jnp.repeat(d_skip[l], SSD_HEAD_DIM).reshape(1, -1))
        h = _tail(h1, mod, norm_ffn2[l].reshape(1, d), w_out[l].astype(BF16),
                  ffn2_wg[l].astype(BF16), ffn2_wu[l].astype(BF16), ffn2_wd[l].astype(BF16),
                  q, kv, cs_tab, _lane_bcast(q_norm_w[l]), _lane_bcast(k_norm_w[l]), sink_logit[l],
                  bc, xdtb, ypart, dt, z, dtb, alog, ssd_norm_w[l].reshape(1, -1))
    return h.astype(x.dtype)
```

```python
import functools

import jax
import jax.numpy as jnp
import numpy as np
from jax import lax
from jax.experimental import pallas as pl
from jax.experimental.pallas import tpu as pltpu

F32 = jnp.float32
BF16 = jnp.bfloat16

SSD_HEAD_DIM = 64
SSD_HEADS = 8
SSD_GROUPS = 2
D_STATE = 128
CONV_K = 5
CHUNK = 128
HEAD_DIM = 64
ATTN_HEADS = 8
ATTN_KV_HEADS = 2
WINDOW = 128
BLOCK = 128
ROPE_DIMS = 16
ROPE_THETA = 500000.0
N_MOD = 9
EPS = 1e-6

SSD_WIDTH = SSD_HEADS * SSD_HEAD_DIM
BC_WIDTH = 2 * SSD_GROUPS * D_STATE
CONV_DIM = SSD_WIDTH + BC_WIDTH

LANES = 128
SUBLANES = 8
VMEM_LIMIT_BYTES = 56 * 1024 * 1024

SEQ_ROWS = 512
FFN_CHUNK = 256
CONV_HALO = 16

NEG_BIG = -1e30


def _dot(a, b):
    return jnp.dot(a, b, preferred_element_type=F32)


def _dot_nt(a, b):
    return lax.dot_general(a, b, (((1,), (1,)), ((), ())), preferred_element_type=F32)


def _row_sums(tri_bf16, x, terms=3):
    acc = None
    r = x
    for t in range(terms):
        h = r.astype(BF16)
        d = _dot(tri_bf16, h)
        acc = d if acc is None else acc + d
        if t + 1 < terms:
            r = r - h.astype(F32)
    return acc


def _silu(x):
    return x * jax.nn.sigmoid(x)


def _iota2(shape, dim):
    return lax.broadcasted_iota(jnp.int32, shape, dim)


def _const_spec(shape):
    nd = len(shape)
    return pl.BlockSpec(shape, lambda *_: (0,) * nd, pipeline_mode=pl.Buffered(1))


def _mod_kernel(c_ref, w_ref, b_ref, o_ref):
    cs = _silu(c_ref[...])
    o_ref[...] = _dot(cs, w_ref[...]) + b_ref[...]


def _adaln_mod(c_pad, w_ada, b_ada):
    rows, d = c_pad.shape
    n = w_ada.shape[1]
    bn = d
    return pl.pallas_call(
        _mod_kernel,
        grid=(n // bn,),
        in_specs=[pl.BlockSpec((rows, d), lambda j: (0, 0)),
                  pl.BlockSpec((d, bn), lambda j: (0, j)),
                  pl.BlockSpec((1, bn), lambda j: (0, j))],
        out_specs=pl.BlockSpec((rows, bn), lambda j: (0, j)),
        out_shape=jax.ShapeDtypeStruct((rows, n), F32),
        name="adaln_mod",
    )(c_pad, w_ada, b_ada.reshape(1, n))


def _rope_kernel(pos_ref, inv_ref, o_ref):
    half = ROPE_DIMS // 2
    s = pos_ref.shape[-1]
    p = pos_ref[0].astype(F32)
    ang = jnp.tile(inv_ref[...], (1, s // LANES)) * p
    o_ref[0, 0:half, :] = jnp.cos(ang)
    o_ref[0, half:, :] = jnp.sin(ang)


def _rope_table(positions):
    half = ROPE_DIMS // 2
    inv = ROPE_THETA ** (-jnp.arange(half, dtype=F32) * 2.0 / ROPE_DIMS)
    b, s = positions.shape
    return pl.pallas_call(
        _rope_kernel,
        grid=(b,),
        in_specs=[pl.BlockSpec((1, 1, s), lambda bi: (bi, 0, 0)), _const_spec((half, LANES))],
        out_specs=pl.BlockSpec((1, 2 * half, s), lambda bi: (bi, 0, 0)),
        out_shape=jax.ShapeDtypeStruct((b, 2 * half, s), F32),
        name="rope_table",
    )(positions.reshape(b, 1, s), jnp.broadcast_to(inv[:, None], (half, LANES)))


def _ada_norm(x, gain_scale, shift):
    ms = jnp.mean(x * x, axis=-1, keepdims=True)
    return x * lax.rsqrt(ms + EPS) * gain_scale + shift


def _swiglu(ub, wg_ref, wu_ref, wd_ref, side_work=(), drain=True):
    d_ff = wg_ref.shape[1]
    acc = None
    for c0 in range(0, d_ff, FFN_CHUNK):
        c1 = min(c0 + FFN_CHUNK, d_ff)
        g = _dot(ub, wg_ref[:, c0:c1])
        up = _dot(ub, wu_ref[:, c0:c1])
        a = (_silu(g) * up).astype(BF16)
        d = _dot(a, wd_ref[c0:c1, :])
        acc = d if acc is None else acc + d
        for gen in side_work:
            next(gen, None)
    for gen in side_work if drain else ():
        for _ in gen:
            pass
    return acc


_INPROJ_LAYOUT = (("z", SSD_WIDTH, F32), ("xbc", CONV_DIM, BF16), ("q", ATTN_HEADS * HEAD_DIM, F32),
                  ("kv", 2 * ATTN_KV_HEADS * HEAD_DIM, F32), ("dt", LANES, F32))


def _ssd_constants():
    r = np.arange(CHUNK)[:, None]
    c = np.arange(CHUNK)[None, :]
    return jnp.asarray(c <= r, dtype=BF16), jnp.asarray(c >= r, dtype=BF16)


def _dt_and_da(dt_raw, dtb_ref, alog_ref):
    lane = _iota2(dt_raw.shape, 1)
    dt = jnp.where(lane < 2 * SSD_HEADS, jax.nn.softplus(dt_raw + dtb_ref[...]), 0.0)
    a = -jnp.exp(alog_ref[...])
    return dt, dt * a


def _head_columns(x, lane0):
    return [jnp.broadcast_to(x[:, lane0 + h:lane0 + h + 1], x.shape) for h in range(SSD_HEADS)]


def _expand_heads(cols):
    first = _iota2(cols[0].shape, 1) < SSD_HEAD_DIM
    return jnp.concatenate([jnp.where(first, cols[2 * p], cols[2 * p + 1]) for p in range(SSD_HEADS // 2)],
                           axis=1)


def _ssd_chunk(direction, da, tri_ref, cm, bm, xdt_fn, h_ref):
    gw = SSD_WIDTH // SSD_GROUPS
    lane0 = 0 if direction == "fwd" else SSD_HEADS
    tot_row = CHUNK - 1 if direction == "fwd" else 0
    cgs = [cm[:, g * D_STATE:(g + 1) * D_STATE] for g in range(SSD_GROUPS)]
    bgs = [bm[:, g * D_STATE:(g + 1) * D_STATE] for g in range(SSD_GROUPS)]
    hgs = [h_ref[:, g * gw:(g + 1) * gw] for g in range(SSD_GROUPS)]

    cs = _row_sums(tri_ref[...], da)
    cbs = [_dot_nt(cgs[g], bgs[g]) for g in range(SSD_GROUPS)]
    chs = [_dot(cgs[g], hgs[g].astype(BF16)) for g in range(SSD_GROUPS)]
    yield

    xdt = xdt_fn()
    cs_cols = _head_columns(cs, lane0)
    csx = _expand_heads(cs_cols)
    tot = csx[tot_row:tot_row + 1, :]
    e_in = jnp.exp(csx)
    xdec = (xdt * jnp.exp(tot - csx)).astype(BF16)
    cdec = jnp.exp(tot)
    cst = cs.T
    row = _iota2((CHUNK, CHUNK), 0)
    col = _iota2((CHUNK, CHUNK), 1)
    keep = (col <= row) if direction == "fwd" else (col > row)
    half = _iota2((CHUNK, LANES), 1) // SSD_HEAD_DIM
    xdt_b = xdt.astype(BF16)
    ys = []
    for g in range(SSD_GROUPS):
        pairs = []
        for pp in range(gw // LANES):
            p = g * (gw // LANES) + pp
            xp = xdt_b[:, p * LANES:(p + 1) * LANES]
            yp = None
            for e in range(2):
                hcol = lane0 + 2 * p + e
                seg = cs_cols[2 * p + e] - cst[hcol:hcol + 1, :]
                m = (cbs[g] * jnp.exp(jnp.where(keep, seg, NEG_BIG))).astype(BF16)
                d = _dot(m, jnp.where(half == e, xp, jnp.zeros_like(xp)))
                yp = d if yp is None else yp + d
            pairs.append(yp)
        ys.append(jnp.concatenate(pairs, axis=1) + chs[g] * e_in[:, g * gw:(g + 1) * gw])
        bgt = bgs[g].astype(F32).T.astype(BF16)
        st = _dot(bgt, xdec[:, g * gw:(g + 1) * gw])
        h_ref[:, g * gw:(g + 1) * gw] = hgs[g] * cdec[:, g * gw:(g + 1) * gw] + st
    return jnp.concatenate(ys, axis=1)


def _ssd_fwd_tile(window_fn, dt_fn, cw_ref, cbias_ref, dtb_ref, alog_ref, dskip_ref,
                  tri_ref, bc_ref, xdtb_ref, yp_ref, h_ref):
    mid = CONV_K // 2
    for j in range(SEQ_ROWS // CHUNK):
        rows = pl.ds(j * CHUNK, CHUNK)
        win = window_fn(j)
        dt, da = _dt_and_da(dt_fn(j), dtb_ref, alog_ref)
        dtx_f = _expand_heads(_head_columns(dt, 0))
        dtx_b = _expand_heads(_head_columns(dt, SSD_HEADS))
        yield
        winf = win.astype(F32)
        acc = cbias_ref[...] + winf[CONV_HALO:CONV_HALO + CHUNK] * cw_ref[mid:mid + 1, :]
        for k in [k for k in range(CONV_K) if k != mid]:
            rolled = pltpu.roll(winf, (mid - k) % winf.shape[0], axis=0)
            acc = acc + rolled[CONV_HALO:CONV_HALO + CHUNK] * cw_ref[k:k + 1, :]
        act = _silu(acc)
        xs = act[:, :SSD_WIDTH]
        bc = act[:, SSD_WIDTH:].astype(BF16)
        xdtb_ref[0, rows, :] = (xs * dtx_b).astype(BF16)
        bc_ref[0, rows, :] = bc
        y = yield from _ssd_chunk("fwd", da, tri_ref, bc[:, SSD_GROUPS * D_STATE:],
                                  bc[:, :SSD_GROUPS * D_STATE], lambda: xs * dtx_f, h_ref)
        yp_ref[0, rows, :] = y + xs * dskip_ref[...]
        yield


def _ssd_bwd_tile(bc_ref, xdtb_ref, yp_ref, dt_ref, z_ref, dtb_ref, alog_ref, nw_ref, tri_ref,
                  h_ref, reset, store):
    h_ref[...] = jnp.where(reset, 0.0, h_ref[...])
    for j in reversed(range(bc_ref.shape[1] // CHUNK)):
        rows = pl.ds(j * CHUNK, CHUNK)
        bc = bc_ref[0, rows, :]
        _, da = _dt_and_da(dt_ref[0, rows, :], dtb_ref, alog_ref)
        y = yield from _ssd_chunk("bwd", da, tri_ref, bc[:, SSD_GROUPS * D_STATE:],
                                  bc[:, :SSD_GROUPS * D_STATE],
                                  lambda: xdtb_ref[0, rows, :].astype(F32), h_ref)
        y = (y + yp_ref[0, rows, :]) * _silu(z_ref[0, rows, :])
        ms = jnp.mean(y * y, axis=-1, keepdims=True)
        store(j * CHUNK, (y * lax.rsqrt(ms + EPS) * nw_ref[...]).astype(BF16))
        yield


RING = 3


def _head_kernel(n_tiles, tiles_per_seq,
                 x_ref, mod_ref, n1_ref, n2_ref, wg_ref, wu_ref, wd_ref, win_ref,
                 cw_ref, cbias_ref, dtb_ref, alog_ref, dskip_ref, tri_ref,
                 *rest):
    n_cast = (len(rest) - 13) // 2
    cast_src, rest = rest[:n_cast], rest[n_cast:]
    h_ref, z_ref, q_ref, kv_ref, dt_ref, bc_ref, xdtb_ref, yp_ref = rest[:8]
    cast_dst = rest[8:8 + n_cast]
    xstage, dtstage, xring, dtring, hstate_ref = rest[8 + n_cast:]
    k = pl.program_id(0)

    for src, dst in zip(cast_src, cast_dst):
        dst[...] = src[...].astype(BF16)

    @pl.when(k == 0)
    def _():
        xstage[...] = jnp.zeros_like(xstage)
        dtstage[...] = jnp.zeros_like(dtstage)
        xring[...] = jnp.zeros_like(xring)
        dtring[...] = jnp.zeros_like(dtring)
        hstate_ref[...] = jnp.zeros_like(hstate_ref)

    t = jnp.maximum(k - 2, 0) % tiles_per_seq
    s_new = (k + RING - 1) % RING
    s_main = (k + RING - 2) % RING
    s_prev = k % RING
    has_prev = t > 0
    has_next = t < tiles_per_seq - 1

    def advance_ring():
        xring[s_new] = xstage[...]
        dtring[s_new] = dtstage[...]
        hstate_ref[...] = jnp.where(has_prev, hstate_ref[...], 0.0)

    def window(j):
        lo = j * CHUNK - CONV_HALO
        hi = (j + 1) * CHUNK + CONV_HALO
        parts = []
        if lo < 0:
            halo = xring[s_prev, SEQ_ROWS + lo:, :]
            parts.append(jnp.where(has_prev, halo, jnp.zeros_like(halo)))
        parts.append(xring[s_main, max(lo, 0):min(hi, SEQ_ROWS), :])
        if hi > SEQ_ROWS:
            halo = xring[s_new, 0:hi - SEQ_ROWS, :]
            parts.append(jnp.where(has_next, halo, jnp.zeros_like(halo)))
        return parts[0] if len(parts) == 1 else jnp.concatenate(parts, axis=0)

    advance_ring()
    scan = _ssd_fwd_tile(window, lambda j: dtring[s_main, j * CHUNK:(j + 1) * CHUNK, :],
                         cw_ref, cbias_ref, dtb_ref, alog_ref, dskip_ref, tri_ref,
                         bc_ref, xdtb_ref, yp_ref, hstate_ref)

    x = x_ref[0]
    mod = mod_ref[0]
    u = _ada_norm(x, n1_ref[...] * (1.0 + mod[1:2]), mod[0:1]).astype(BF16)
    ff = _swiglu(u, wg_ref, wu_ref, wd_ref, side_work=(scan,), drain=False)
    h = x + (0.5 * (1.0 + mod[2:3])) * ff
    h_ref[0] = h
    u2 = _ada_norm(h, n2_ref[...] * (1.0 + mod[4:5]), mod[3:4]).astype(BF16)
    dests = {"z": (z_ref,), "xbc": (xstage,), "q": (q_ref,), "kv": (kv_ref,), "dt": (dt_ref, dtstage)}
    c0 = 0
    for name, w, _ in _INPROJ_LAYOUT:
        piece = _dot(u2, win_ref[:, c0:c0 + w])
        for ref in dests[name]:
            if len(ref.shape) == 3:
                ref[0] = piece.astype(ref.dtype)
            else:
                ref[...] = piece.astype(ref.dtype)
        c0 += w
        next(scan, None)
    for _ in scan:
        pass


def _row_block(rows, max_blocks):
    tile = 2 * SUBLANES
    return next(r for r in range(tile, rows + 1, tile) if rows % r == 0 and rows // r <= max_blocks)


def _head(x, mod, norm1, norm2, wg, wu, wd, w_in_p, conv_w8, conv_b, dtb, alog, dskip, to_bf16):
    b, s, d = x.shape
    d_ff = wg.shape[1]
    tps = s // SEQ_ROWS
    n_tiles = b * tps
    tri_lo, _ = _ssd_constants()
    ffn_tile = lambda k: jnp.minimum(k, n_tiles - 1)
    scan_tile = lambda k: jnp.maximum(k - 2, 0)

    def ffn_tok(w):
        return pl.BlockSpec((1, SEQ_ROWS, w), lambda k: (ffn_tile(k) // tps, ffn_tile(k) % tps, 0))

    def scan_tok(w):
        return pl.BlockSpec((1, SEQ_ROWS, w), lambda k: (scan_tile(k) // tps, scan_tile(k) % tps, 0))

    widths = {name: (w, dt) for name, w, dt in _INPROJ_LAYOUT}
    hbm_outs = ("z", "q", "kv", "dt")
    out_shape = [jax.ShapeDtypeStruct((b, s, d), F32)]
    out_shape += [jax.ShapeDtypeStruct((b, s, widths[n][0]), widths[n][1]) for n in hbm_outs]
    out_shape += [jax.ShapeDtypeStruct((b, s, BC_WIDTH), BF16), jax.ShapeDtypeStruct((b, s, SSD_WIDTH), BF16),
                  jax.ShapeDtypeStruct((b, s, SSD_WIDTH), F32)]
    out_specs = [ffn_tok(d)] + [ffn_tok(widths[n][0]) for n in hbm_outs]
    out_specs += [scan_tok(BC_WIDTH), scan_tok(SSD_WIDTH), scan_tok(SSD_WIDTH)]
    cast_specs = []
    for w in to_bf16:
        rb = _row_block(w.shape[0], n_tiles)
        cast_specs.append(pl.BlockSpec((rb, w.shape[1]),
                                       lambda k, last=w.shape[0] // rb - 1: (jnp.minimum(k, last), 0)))
    out_shape += [jax.ShapeDtypeStruct(w.shape, BF16) for w in to_bf16]
    out_specs += cast_specs
    return pl.pallas_call(
        functools.partial(_head_kernel, n_tiles, tps),
        grid=(n_tiles + 2,),
        in_specs=[ffn_tok(d),
                  pl.BlockSpec((1, N_MOD, d), lambda k: (ffn_tile(k) // tps, 0, 0)),
                  _const_spec((1, d)), _const_spec((1, d)),
                  _const_spec((d, d_ff)), _const_spec((d, d_ff)), _const_spec((d_ff, d)),
                  _const_spec(w_in_p.shape),
                  _const_spec(conv_w8.shape), _const_spec((1, CONV_DIM)),
                  _const_spec((1, LANES)), _const_spec((1, LANES)), _const_spec((1, SSD_WIDTH)),
                  _const_spec(tri_lo.shape)] + cast_specs,
        out_specs=out_specs,
        out_shape=out_shape,
        scratch_shapes=[pltpu.VMEM((SEQ_ROWS, CONV_DIM), BF16), pltpu.VMEM((SEQ_ROWS, LANES), F32),
                        pltpu.VMEM((RING, SEQ_ROWS, CONV_DIM), BF16), pltpu.VMEM((RING, SEQ_ROWS, LANES), F32),
                        pltpu.VMEM((D_STATE, SSD_WIDTH), F32)],
        compiler_params=pltpu.CompilerParams(dimension_semantics=("arbitrary",),
                                             vmem_limit_bytes=VMEM_LIMIT_BYTES),
        name="ffn1_inproj_scan",
    )(x, mod, norm1, norm2, wg, wu, wd, w_in_p, conv_w8, conv_b, dtb, alog, dskip, tri_lo, *to_bf16)


def _head_prep_t(t, w_b, cos, sin):
    ms = jnp.sum(t * t, axis=0, keepdims=True) * (1.0 / HEAD_DIM)
    tn = t * lax.rsqrt(ms + EPS) * w_b
    half = ROPE_DIMS // 2
    t1 = tn[0:half]
    t2 = tn[half:ROPE_DIMS]
    return jnp.concatenate([t1 * cos - t2 * sin, t2 * cos + t1 * sin, tn[ROPE_DIMS:]], axis=0)


def _attn_tile(q_ref, kvp_ref, kvc_ref, kvn_ref, csp_ref, csc_ref, csn_ref, qw_ref, kw_ref, sink_ref,
               has_prev, has_next, store):
    tq = q_ref.shape[1]
    nsub = tq // BLOCK
    nw = tq + 2 * BLOCK
    kvw = ATTN_KV_HEADS * HEAD_DIM
    half = ROPE_DIMS // 2
    q_per_kv = ATTN_HEADS // ATTN_KV_HEADS

    kv = jnp.concatenate([kvp_ref[0], kvc_ref[0], kvn_ref[0]], axis=0)
    cs = jnp.concatenate([csp_ref[0], csc_ref[0], csn_ref[0]], axis=1)
    k_t = kv[:, :kvw].T
    v_t = kv[:, kvw:].T.astype(BF16)
    kw_b = jnp.tile(kw_ref[...], (1, nw // LANES))
    k_prep = jnp.concatenate(
        [_head_prep_t(k_t[g * HEAD_DIM:(g + 1) * HEAD_DIM], kw_b, cs[0:half], cs[half:])
         for g in range(ATTN_KV_HEADS)], axis=0)
    keys = k_prep.T.astype(BF16)

    q_t = q_ref[0].T
    qw_b = jnp.tile(qw_ref[...] * (HEAD_DIM ** -0.5), (1, tq // LANES))
    cs_q = csc_ref[0]
    zeros = jnp.zeros((HEAD_DIM, tq), BF16)
    q_heads = []
    for h in range(ATTN_HEADS):
        qh = _head_prep_t(q_t[h * HEAD_DIM:(h + 1) * HEAD_DIM], qw_b, cs_q[0:half], cs_q[half:]).astype(BF16)
        q_heads.append(jnp.concatenate([qh, zeros] if h // q_per_kv == 0 else [zeros, qh], axis=0))

    yield
    kr = _iota2((BLOCK, BLOCK), 0)
    qc = _iota2((BLOCK, BLOCK), 1)
    bias_prev = jnp.where(kr >= qc, 0.0, NEG_BIG)
    bias_next = jnp.where(kr <= qc, 0.0, NEG_BIG)
    sink_row = jnp.concatenate([jnp.full((1, BLOCK), sink_ref[h], F32) for h in range(ATTN_HEADS)], axis=1)
    gq = q_per_kv * BLOCK
    for j in range(nsub):
        bp = bias_prev if j > 0 else jnp.where(has_prev, bias_prev, NEG_BIG)
        bn = bias_next if j < nsub - 1 else jnp.where(has_next, bias_next, NEG_BIG)
        lo, hi = j * BLOCK, (j + 3) * BLOCK
        q_all = jnp.concatenate([qh[:, lo:lo + BLOCK] for qh in q_heads], axis=1)
        s = _dot(keys[lo:hi], q_all)
        yield
        s0 = s[0:BLOCK] + jnp.tile(bp, (1, ATTN_HEADS))
        s1 = s[BLOCK:2 * BLOCK]
        s2 = s[2 * BLOCK:] + jnp.tile(bn, (1, ATTN_HEADS))
        m = jnp.maximum(jnp.max(jnp.maximum(jnp.maximum(s0, s1), s2), axis=0, keepdims=True), sink_row)
        p0 = jnp.exp(s0 - m)
        p1 = jnp.exp(s1 - m)
        p2 = jnp.exp(s2 - m)
        denom = jnp.sum(p0 + p1 + p2, axis=0, keepdims=True) + jnp.exp(sink_row - m)
        p = jnp.concatenate([p0, p1, p2], axis=0).astype(BF16)
        inv = 1.0 / denom
        outs = []
        for g in range(ATTN_KV_HEADS):
            o = _dot(v_t[g * HEAD_DIM:(g + 1) * HEAD_DIM, lo:hi], p[:, g * gq:(g + 1) * gq])
            o = o * inv[:, g * gq:(g + 1) * gq]
            outs += [o[:, k * BLOCK:(k + 1) * BLOCK] for k in range(q_per_kv)]
        store(lo, jnp.concatenate(outs, axis=0).T.astype(BF16))
        yield


def _tail_kernel(n_tiles, tiles_per_seq,
                 h_ref, mod_ref, n3_ref, wo_ref, wg_ref, wu_ref, wd_ref,
                 q_ref, kvp_ref, kvc_ref, kvn_ref, csp_ref, csc_ref, csn_ref, qw_ref, kw_ref, sink_ref,
                 bc_ref, xdtb_ref, yp_ref, dt_ref, z_ref, dtb_ref, alog_ref, nw_ref, tri_ref,
                 o_ref, y_scr, hstate_ref):
    k = pl.program_id(0)

    @pl.when(k == 0)
    def _():
        y_scr[...] = jnp.zeros_like(y_scr)
        hstate_ref[...] = jnp.zeros_like(hstate_ref)

    rd = k % 2
    wr = 1 - rd

    t = jnp.maximum(n_tiles - 1 - k, 0) % tiles_per_seq
    is_first = t == 0
    is_last = t == tiles_per_seq - 1

    def store_ssd(row0, y):
        y_scr[wr, row0:row0 + CHUNK, 0:SSD_WIDTH] = y

    def store_attn(row0, y):
        y_scr[wr, row0:row0 + BLOCK, SSD_WIDTH:] = y

    attn = _attn_tile(q_ref, kvp_ref, kvc_ref, kvn_ref, csp_ref, csc_ref, csn_ref, qw_ref, kw_ref, sink_ref,
                      jnp.logical_not(is_first), jnp.logical_not(is_last), store_attn)
    scan = _ssd_bwd_tile(bc_ref, xdtb_ref, yp_ref, dt_ref, z_ref, dtb_ref, alog_ref, nw_ref, tri_ref,
                         hstate_ref, is_last, store_ssd)

    h1 = h_ref[0]
    mod = mod_ref[0]
    mix = _dot(y_scr[rd], wo_ref[...])
    next(scan, None)
    h2 = h1 + (1.0 + mod[5:6]) * mix
    u = _ada_norm(h2, n3_ref[...] * (1.0 + mod[7:8]), mod[6:7]).astype(BF16)
    ff = _swiglu(u, wg_ref, wu_ref, wd_ref, side_work=(attn, scan))
    o_ref[0] = h2 + (0.5 * (1.0 + mod[8:9])) * ff


def _tail(h1, mod, norm3, w_out, wg, wu, wd, q, kv, cs_tab, qw_b, kw_b, sink,
          bc, xdtb, ypart, dt, z, dtb, alog, norm_w):
    b, s, d = h1.shape
    d_ff = wg.shape[1]
    tps = s // SEQ_ROWS
    n_tiles = b * tps
    nsub = SEQ_ROWS // BLOCK
    nb = s // BLOCK
    aw = q.shape[-1]
    kvw2 = kv.shape[-1]
    cs_rows = cs_tab.shape[1]
    _, tri_up = _ssd_constants()

    ffn_tile = lambda k: jnp.clip(n_tiles - k, 0, n_tiles - 1)
    mix_tile = lambda k: jnp.maximum(n_tiles - 1 - k, 0)
    prev_blk = lambda t: jnp.maximum(t * nsub - 1, 0)
    next_blk = lambda t: jnp.minimum((t + 1) * nsub, nb - 1)

    def ffn_tok(w):
        return pl.BlockSpec((1, SEQ_ROWS, w), lambda k: (ffn_tile(k) // tps, ffn_tile(k) % tps, 0))

    def mix_tok(w):
        return pl.BlockSpec((1, SEQ_ROWS, w), lambda k: (mix_tile(k) // tps, mix_tile(k) % tps, 0))

    in_specs = [
        ffn_tok(d),
        pl.BlockSpec((1, N_MOD, d), lambda k: (ffn_tile(k) // tps, 0, 0)),
        _const_spec((1, d)), _const_spec(w_out.shape),
        _const_spec((d, d_ff)), _const_spec((d, d_ff)), _const_spec((d_ff, d)),
        mix_tok(aw),
        pl.BlockSpec((1, BLOCK, kvw2), lambda k: (mix_tile(k) // tps, prev_blk(mix_tile(k) % tps), 0)),
        mix_tok(kvw2),
        pl.BlockSpec((1, BLOCK, kvw2), lambda k: (mix_tile(k) // tps, next_blk(mix_tile(k) % tps), 0)),
        pl.BlockSpec((1, cs_rows, BLOCK), lambda k: (mix_tile(k) // tps, 0, prev_blk(mix_tile(k) % tps))),
        pl.BlockSpec((1, cs_rows, SEQ_ROWS), lambda k: (mix_tile(k) // tps, 0, mix_tile(k) % tps)),
        pl.BlockSpec((1, cs_rows, BLOCK), lambda k: (mix_tile(k) // tps, 0, next_blk(mix_tile(k) % tps))),
        _const_spec((HEAD_DIM, LANES)), _const_spec((HEAD_DIM, LANES)),
        pl.BlockSpec(memory_space=pltpu.SMEM),
        mix_tok(BC_WIDTH), mix_tok(SSD_WIDTH), mix_tok(SSD_WIDTH), mix_tok(LANES), mix_tok(SSD_WIDTH),
        _const_spec((1, LANES)), _const_spec((1, LANES)), _const_spec((1, SSD_WIDTH)),
        _const_spec(tri_up.shape),
    ]
    return pl.pallas_call(
        functools.partial(_tail_kernel, n_tiles, tps),
        grid=(n_tiles + 1,),
        in_specs=in_specs,
        out_specs=ffn_tok(d),
        out_shape=jax.ShapeDtypeStruct((b, s, d), F32),
        scratch_shapes=[pltpu.VMEM((2, SEQ_ROWS, SSD_WIDTH + aw), BF16),
                        pltpu.VMEM((D_STATE, SSD_WIDTH), F32)],
        compiler_params=pltpu.CompilerParams(dimension_semantics=("arbitrary",),
                                             vmem_limit_bytes=VMEM_LIMIT_BYTES),
        name="mixers_outproj_ffn2",
    )(h1, mod, norm3, w_out, wg, wu, wd, q, kv, kv, kv, cs_tab, cs_tab, cs_tab, qw_b, kw_b, sink,
      bc, xdtb, ypart, dt, z, dtb, alog, norm_w, tri_up)


def _pad_inproj(w_in):
    n_dt = 2 * SSD_HEADS
    s_xbc = SSD_WIDTH + CONV_DIM
    s_dt = s_xbc + n_dt
    dt_cols = jnp.pad(w_in[:, s_xbc:s_dt], ((0, 0), (0, LANES - n_dt)))
    return jnp.concatenate([w_in[:, :s_xbc], w_in[:, s_dt:], dt_cols], axis=1).astype(BF16)


def _pad_lanes(v, width=LANES):
    v = v.reshape(1, -1)
    return jnp.pad(v, ((0, 0), (0, width - v.shape[1])))


def _lane_bcast(v):
    return jnp.broadcast_to(v[:, None], (v.shape[0], LANES))


def kernel(x, c, positions, w_ada, b_ada, norm_ffn1, ffn1_wg, ffn1_wu, ffn1_wd, norm_mix, w_in, conv_w,
           conv_b, dt_bias, a_log, d_skip, ssd_norm_w, q_norm_w, k_norm_w, sink_logit, w_out, norm_ffn2,
           ffn2_wg, ffn2_wu, ffn2_wd):
    depth = w_ada.shape[0]
    b, s, d = x.shape
    h = x.astype(F32)
    c_pad = jnp.pad(c.astype(F32), ((0, -b % SUBLANES), (0, 0)))
    cs_tab = _rope_table(positions)
    for l in range(depth):
        mod = _adaln_mod(c_pad, w_ada[l], b_ada[l])[:b].reshape(b, N_MOD, d)
        conv_w8 = jnp.pad(conv_w[l], ((0, SUBLANES - CONV_K), (0, 0)))
        dtb, alog = _pad_lanes(dt_bias[l]), _pad_lanes(a_log[l])
        h1, z, q, kv, dt, bc, xdtb, ypart, wo_b, wg2_b, wu2_b, wd2_b = _head(
            h, mod, norm_ffn1[l].reshape(1, d), norm_mix[l].reshape(1, d),
            ffn1_wg[l].astype(BF16), ffn1_wu[l].astype(BF16), ffn1_wd[l].astype(BF16), _pad_inproj(w_in[l]),
            conv_w8, conv_b[l].reshape(1, -1), dtb, alog, jnp.repeat(d_skip[l], SSD_HEAD_DIM).reshape(1, -1),
            to_bf16=(w_out[l], ffn2_wg[l], ffn2_wu[l], ffn2_wd[l]))
        h = _tail(h1, mod, norm_ffn2[l].reshape(1, d), wo_b, wg2_b, wu2_b, wd2_b,
                  q, kv, cs_tab, _lane_bcast(q_norm_w[l]), _lane_bcast(k_norm_w[l]), sink_logit[l],
                  bc, xdtb, ypart, dt, z, dtb, alog, ssd_norm_w[l].reshape(1, -1))
    return h.astype(x.dtype)
```

```python
import functools

import jax
import jax.numpy as jnp
import numpy as np
from jax import lax
from jax.experimental import pallas as pl
from jax.experimental.pallas import tpu as pltpu

F32 = jnp.float32
BF16 = jnp.bfloat16

SSD_HEAD_DIM = 64
SSD_HEADS = 8
SSD_GROUPS = 2
D_STATE = 128
CONV_K = 5
CHUNK = 128
HEAD_DIM = 64
ATTN_HEADS = 8
ATTN_KV_HEADS = 2
WINDOW = 128
BLOCK = 128
ROPE_DIMS = 16
ROPE_THETA = 500000.0
N_MOD = 9
EPS = 1e-6

SSD_WIDTH = SSD_HEADS * SSD_HEAD_DIM
BC_WIDTH = 2 * SSD_GROUPS * D_STATE
CONV_DIM = SSD_WIDTH + BC_WIDTH

LANES = 128
SUBLANES = 8
VMEM_LIMIT_BYTES = 56 * 1024 * 1024

SEQ_ROWS = 512
FFN_CHUNK = 256
CONV_HALO = 16

NEG_BIG = -1e30


def _dot(a, b):
    return jnp.dot(a, b, preferred_element_type=F32)


def _dot_nt(a, b):
    return lax.dot_general(a, b, (((1,), (1,)), ((), ())), preferred_element_type=F32)


def _row_sums(tri_bf16, x, terms=3):
    acc = None
    r = x
    for t in range(terms):
        h = r.astype(BF16)
        d = _dot(tri_bf16, h)
        acc = d if acc is None else acc + d
        if t + 1 < terms:
            r = r - h.astype(F32)
    return acc


def _silu(x):
    return x * jax.nn.sigmoid(x)


def _iota2(shape, dim):
    return lax.broadcasted_iota(jnp.int32, shape, dim)


def _const_spec(shape):
    nd = len(shape)
    return pl.BlockSpec(shape, lambda *_: (0,) * nd, pipeline_mode=pl.Buffered(1))


def _mod_kernel(c_ref, w_ref, b_ref, o_ref):
    cs = _silu(c_ref[...])
    o_ref[...] = _dot(cs, w_ref[...]) + b_ref[...]


def _adaln_mod(c_pad, w_ada, b_ada):
    rows, d = c_pad.shape
    n = w_ada.shape[1]
    bn = 3 * d
    return pl.pallas_call(
        _mod_kernel,
        grid=(n // bn,),
        in_specs=[pl.BlockSpec((rows, d), lambda j: (0, 0)),
                  pl.BlockSpec((d, bn), lambda j: (0, j)),
                  pl.BlockSpec((1, bn), lambda j: (0, j))],
        out_specs=pl.BlockSpec((rows, bn), lambda j: (0, j)),
        out_shape=jax.ShapeDtypeStruct((rows, n), F32),
        name="adaln_mod",
    )(c_pad, w_ada, b_ada.reshape(1, n))


def _rope_kernel(pos_ref, inv_ref, o_ref):
    half = ROPE_DIMS // 2
    s = pos_ref.shape[-1]
    p = pos_ref[0].astype(F32)
    ang = jnp.tile(inv_ref[...], (1, s // LANES)) * p
    o_ref[0, 0:half, :] = jnp.cos(ang)
    o_ref[0, half:, :] = jnp.sin(ang)


def _rope_table(positions):
    half = ROPE_DIMS // 2
    inv = ROPE_THETA ** (-jnp.arange(half, dtype=F32) * 2.0 / ROPE_DIMS)
    b, s = positions.shape
    return pl.pallas_call(
        _rope_kernel,
        grid=(b,),
        in_specs=[pl.BlockSpec((1, 1, s), lambda bi: (bi, 0, 0)), _const_spec((half, LANES))],
        out_specs=pl.BlockSpec((1, 2 * half, s), lambda bi: (bi, 0, 0)),
        out_shape=jax.ShapeDtypeStruct((b, 2 * half, s), F32),
        name="rope_table",
    )(positions.reshape(b, 1, s), jnp.broadcast_to(inv[:, None], (half, LANES)))


def _ada_norm(x, gain_scale, shift):
    ms = jnp.mean(x * x, axis=-1, keepdims=True)
    return x * lax.rsqrt(ms + EPS) * gain_scale + shift


def _swiglu(ub, wg_ref, wu_ref, wd_ref, side_work=(), drain=True):
    d_ff = wg_ref.shape[1]
    acc = None
    for c0 in range(0, d_ff, FFN_CHUNK):
        c1 = min(c0 + FFN_CHUNK, d_ff)
        g = _dot(ub, wg_ref[:, c0:c1])
        up = _dot(ub, wu_ref[:, c0:c1])
        a = (_silu(g) * up).astype(BF16)
        d = _dot(a, wd_ref[c0:c1, :])
        acc = d if acc is None else acc + d
        for gen in side_work:
            next(gen, None)
    for gen in side_work if drain else ():
        for _ in gen:
            pass
    return acc


_INPROJ_LAYOUT = (("z", SSD_WIDTH, F32), ("xbc", CONV_DIM, BF16), ("q", ATTN_HEADS * HEAD_DIM, F32),
                  ("kv", 2 * ATTN_KV_HEADS * HEAD_DIM, F32), ("dt", LANES, F32))


def _ssd_constants():
    r = np.arange(CHUNK)[:, None]
    c = np.arange(CHUNK)[None, :]
    return jnp.asarray(c <= r, dtype=BF16), jnp.asarray(c >= r, dtype=BF16)


def _dt_and_da(dt_raw, dtb_ref, alog_ref):
    lane = _iota2(dt_raw.shape, 1)
    dt = jnp.where(lane < 2 * SSD_HEADS, jax.nn.softplus(dt_raw + dtb_ref[...]), 0.0)
    a = -jnp.exp(alog_ref[...])
    return dt, dt * a


def _head_columns(x, lane0):
    return [jnp.broadcast_to(x[:, lane0 + h:lane0 + h + 1], x.shape) for h in range(SSD_HEADS)]


def _expand_heads(cols):
    first = _iota2(cols[0].shape, 1) < SSD_HEAD_DIM
    return jnp.concatenate([jnp.where(first, cols[2 * p], cols[2 * p + 1]) for p in range(SSD_HEADS // 2)],
                           axis=1)


def _ssd_chunk(direction, da, tri_ref, cm, bm, xdt_fn, h_ref):
    gw = SSD_WIDTH // SSD_GROUPS
    lane0 = 0 if direction == "fwd" else SSD_HEADS
    tot_row = CHUNK - 1 if direction == "fwd" else 0
    cgs = [cm[:, g * D_STATE:(g + 1) * D_STATE] for g in range(SSD_GROUPS)]
    bgs = [bm[:, g * D_STATE:(g + 1) * D_STATE] for g in range(SSD_GROUPS)]
    hgs = [h_ref[:, g * gw:(g + 1) * gw] for g in range(SSD_GROUPS)]

    cs = _row_sums(tri_ref[...], da)
    cbs = [_dot_nt(cgs[g], bgs[g]) for g in range(SSD_GROUPS)]
    chs = [_dot(cgs[g], hgs[g].astype(BF16)) for g in range(SSD_GROUPS)]
    yield

    xdt = xdt_fn()
    cs_cols = _head_columns(cs, lane0)
    csx = _expand_heads(cs_cols)
    tot = csx[tot_row:tot_row + 1, :]
    e_in = jnp.exp(csx)
    xdec = (xdt * jnp.exp(tot - csx)).astype(BF16)
    cdec = jnp.exp(tot)
    cst = cs.T
    row = _iota2((CHUNK, CHUNK), 0)
    col = _iota2((CHUNK, CHUNK), 1)
    keep = (col <= row) if direction == "fwd" else (col > row)
    half = _iota2((CHUNK, LANES), 1) // SSD_HEAD_DIM
    xdt_b = xdt.astype(BF16)
    ys = []
    for g in range(SSD_GROUPS):
        pairs = []
        for pp in range(gw // LANES):
            p = g * (gw // LANES) + pp
            xp = xdt_b[:, p * LANES:(p + 1) * LANES]
            yp = None
            for e in range(2):
                hcol = lane0 + 2 * p + e
                seg = cs_cols[2 * p + e] - cst[hcol:hcol + 1, :]
                m = (cbs[g] * jnp.exp(jnp.where(keep, seg, NEG_BIG))).astype(BF16)
                d = _dot(m, jnp.where(half == e, xp, jnp.zeros_like(xp)))
                yp = d if yp is None else yp + d
            pairs.append(yp)
        ys.append(jnp.concatenate(pairs, axis=1) + chs[g] * e_in[:, g * gw:(g + 1) * gw])
        bgt = bgs[g].astype(F32).T.astype(BF16)
        st = _dot(bgt, xdec[:, g * gw:(g + 1) * gw])
        h_ref[:, g * gw:(g + 1) * gw] = hgs[g] * cdec[:, g * gw:(g + 1) * gw] + st
    return jnp.concatenate(ys, axis=1)


def _ssd_fwd_tile(window_fn, dt_fn, cw_ref, cbias_ref, dtb_ref, alog_ref, dskip_ref,
                  tri_ref, bc_ref, xdtb_ref, yp_ref, h_ref):
    mid = CONV_K // 2
    for j in range(SEQ_ROWS // CHUNK):
        rows = pl.ds(j * CHUNK, CHUNK)
        win = window_fn(j)
        dt, da = _dt_and_da(dt_fn(j), dtb_ref, alog_ref)
        dtx_f = _expand_heads(_head_columns(dt, 0))
        dtx_b = _expand_heads(_head_columns(dt, SSD_HEADS))
        yield
        winf = win.astype(F32)
        acc = cbias_ref[...] + winf[CONV_HALO:CONV_HALO + CHUNK] * cw_ref[mid:mid + 1, :]
        for k in [k for k in range(CONV_K) if k != mid]:
            rolled = pltpu.roll(winf, (mid - k) % winf.shape[0], axis=0)
            acc = acc + rolled[CONV_HALO:CONV_HALO + CHUNK] * cw_ref[k:k + 1, :]
        act = _silu(acc)
        xs = act[:, :SSD_WIDTH]
        bc = act[:, SSD_WIDTH:].astype(BF16)
        xdtb_ref[0, rows, :] = (xs * dtx_b).astype(BF16)
        bc_ref[0, rows, :] = bc
        y = yield from _ssd_chunk("fwd", da, tri_ref, bc[:, SSD_GROUPS * D_STATE:],
                                  bc[:, :SSD_GROUPS * D_STATE], lambda: xs * dtx_f, h_ref)
        yp_ref[0, rows, :] = y + xs * dskip_ref[...]
        yield


def _ssd_bwd_tile(bc_ref, xdtb_ref, yp_ref, dt_ref, z_ref, dtb_ref, alog_ref, nw_ref, tri_ref,
                  h_ref, reset, store):
    h_ref[...] = jnp.where(reset, 0.0, h_ref[...])
    for j in reversed(range(bc_ref.shape[1] // CHUNK)):
        rows = pl.ds(j * CHUNK, CHUNK)
        bc = bc_ref[0, rows, :]
        _, da = _dt_and_da(dt_ref[0, rows, :], dtb_ref, alog_ref)
        y = yield from _ssd_chunk("bwd", da, tri_ref, bc[:, SSD_GROUPS * D_STATE:],
                                  bc[:, :SSD_GROUPS * D_STATE],
                                  lambda: xdtb_ref[0, rows, :].astype(F32), h_ref)
        y = (y + yp_ref[0, rows, :]) * _silu(z_ref[0, rows, :])
        ms = jnp.mean(y * y, axis=-1, keepdims=True)
        store(j * CHUNK, (y * lax.rsqrt(ms + EPS) * nw_ref[...]).astype(BF16))
        yield


RING = 3


def _head_kernel(n_tiles, tiles_per_seq,
                 x_ref, mod_ref, n1_ref, n2_ref, wg_ref, wu_ref, wd_ref, win_ref,
                 cw_ref, cbias_ref, dtb_ref, alog_ref, dskip_ref, tri_ref,
                 *rest):
    n_cast = (len(rest) - 13) // 2
    cast_src, rest = rest[:n_cast], rest[n_cast:]
    h_ref, z_ref, q_ref, kv_ref, dt_ref, bc_ref, xdtb_ref, yp_ref = rest[:8]
    cast_dst = rest[8:8 + n_cast]
    xstage, dtstage, xring, dtring, hstate_ref = rest[8 + n_cast:]
    k = pl.program_id(0)

    for src, dst in zip(cast_src, cast_dst):
        dst[...] = src[...].astype(BF16)

    @pl.when(k == 0)
    def _():
        xstage[...] = jnp.zeros_like(xstage)
        dtstage[...] = jnp.zeros_like(dtstage)
        xring[...] = jnp.zeros_like(xring)
        dtring[...] = jnp.zeros_like(dtring)
        hstate_ref[...] = jnp.zeros_like(hstate_ref)

    t = jnp.maximum(k - 2, 0) % tiles_per_seq
    s_new = (k + RING - 1) % RING
    s_main = (k + RING - 2) % RING
    s_prev = k % RING
    has_prev = t > 0
    has_next = t < tiles_per_seq - 1

    def advance_ring():
        xring[s_new] = xstage[...]
        dtring[s_new] = dtstage[...]
        hstate_ref[...] = jnp.where(has_prev, hstate_ref[...], 0.0)

    def window(j):
        lo = j * CHUNK - CONV_HALO
        hi = (j + 1) * CHUNK + CONV_HALO
        parts = []
        if lo < 0:
            halo = xring[s_prev, SEQ_ROWS + lo:, :]
            parts.append(jnp.where(has_prev, halo, jnp.zeros_like(halo)))
        parts.append(xring[s_main, max(lo, 0):min(hi, SEQ_ROWS), :])
        if hi > SEQ_ROWS:
            halo = xring[s_new, 0:hi - SEQ_ROWS, :]
            parts.append(jnp.where(has_next, halo, jnp.zeros_like(halo)))
        return parts[0] if len(parts) == 1 else jnp.concatenate(parts, axis=0)

    advance_ring()
    scan = _ssd_fwd_tile(window, lambda j: dtring[s_main, j * CHUNK:(j + 1) * CHUNK, :],
                         cw_ref, cbias_ref, dtb_ref, alog_ref, dskip_ref, tri_ref,
                         bc_ref, xdtb_ref, yp_ref, hstate_ref)

    x = x_ref[0]
    mod = mod_ref[0]
    u = _ada_norm(x, n1_ref[...] * (1.0 + mod[1:2]), mod[0:1]).astype(BF16)
    ff = _swiglu(u, wg_ref, wu_ref, wd_ref, side_work=(scan,), drain=False)
    h = x + (0.5 * (1.0 + mod[2:3])) * ff
    h_ref[0] = h
    u2 = _ada_norm(h, n2_ref[...] * (1.0 + mod[4:5]), mod[3:4]).astype(BF16)
    dests = {"z": (z_ref,), "xbc": (xstage,), "q": (q_ref,), "kv": (kv_ref,), "dt": (dt_ref, dtstage)}
    c0 = 0
    for name, w, _ in _INPROJ_LAYOUT:
        piece = _dot(u2, win_ref[:, c0:c0 + w])
        for ref in dests[name]:
            if len(ref.shape) == 3:
                ref[0] = piece.astype(ref.dtype)
            else:
                ref[...] = piece.astype(ref.dtype)
        c0 += w
        next(scan, None)
    for _ in scan:
        pass


def _row_block(rows, max_blocks):
    tile = 2 * SUBLANES
    return next(r for r in range(tile, rows + 1, tile) if rows % r == 0 and rows // r <= max_blocks)


def _head(x, mod, norm1, norm2, wg, wu, wd, w_in_p, conv_w8, conv_b, dtb, alog, dskip, to_bf16):
    b, s, d = x.shape
    d_ff = wg.shape[1]
    tps = s // SEQ_ROWS
    n_tiles = b * tps
    tri_lo, _ = _ssd_constants()
    ffn_tile = lambda k: jnp.minimum(k, n_tiles - 1)
    scan_tile = lambda k: jnp.maximum(k - 2, 0)

    def ffn_tok(w):
        return pl.BlockSpec((1, SEQ_ROWS, w), lambda k: (ffn_tile(k) // tps, ffn_tile(k) % tps, 0))

    def scan_tok(w):
        return pl.BlockSpec((1, SEQ_ROWS, w), lambda k: (scan_tile(k) // tps, scan_tile(k) % tps, 0))

    widths = {name: (w, dt) for name, w, dt in _INPROJ_LAYOUT}
    hbm_outs = ("z", "q", "kv", "dt")
    out_shape = [jax.ShapeDtypeStruct((b, s, d), F32)]
    out_shape += [jax.ShapeDtypeStruct((b, s, widths[n][0]), widths[n][1]) for n in hbm_outs]
    out_shape += [jax.ShapeDtypeStruct((b, s, BC_WIDTH), BF16), jax.ShapeDtypeStruct((b, s, SSD_WIDTH), BF16),
                  jax.ShapeDtypeStruct((b, s, SSD_WIDTH), F32)]
    out_specs = [ffn_tok(d)] + [ffn_tok(widths[n][0]) for n in hbm_outs]
    out_specs += [scan_tok(BC_WIDTH), scan_tok(SSD_WIDTH), scan_tok(SSD_WIDTH)]
    cast_specs = []
    for w in to_bf16:
        rb = _row_block(w.shape[0], n_tiles)
        cast_specs.append(pl.BlockSpec((rb, w.shape[1]),
                                       lambda k, last=w.shape[0] // rb - 1: (jnp.minimum(k, last), 0)))
    out_shape += [jax.ShapeDtypeStruct(w.shape, BF16) for w in to_bf16]
    out_specs += cast_specs
    return pl.pallas_call(
        functools.partial(_head_kernel, n_tiles, tps),
        grid=(n_tiles + 2,),
        in_specs=[ffn_tok(d),
                  pl.BlockSpec((1, N_MOD, d), lambda k: (ffn_tile(k) // tps, 0, 0)),
                  _const_spec((1, d)), _const_spec((1, d)),
                  _const_spec((d, d_ff)), _const_spec((d, d_ff)), _const_spec((d_ff, d)),
                  _const_spec(w_in_p.shape),
                  _const_spec(conv_w8.shape), _const_spec((1, CONV_DIM)),
                  _const_spec((1, LANES)), _const_spec((1, LANES)), _const_spec((1, SSD_WIDTH)),
                  _const_spec(tri_lo.shape)] + cast_specs,
        out_specs=out_specs,
        out_shape=out_shape,
        scratch_shapes=[pltpu.VMEM((SEQ_ROWS, CONV_DIM), BF16), pltpu.VMEM((SEQ_ROWS, LANES), F32),
                        pltpu.VMEM((RING, SEQ_ROWS, CONV_DIM), BF16), pltpu.VMEM((RING, SEQ_ROWS, LANES), F32),
                        pltpu.VMEM((D_STATE, SSD_WIDTH), F32)],
        compiler_params=pltpu.CompilerParams(dimension_semantics=("arbitrary",),
                                             vmem_limit_bytes=VMEM_LIMIT_BYTES),
        name="ffn1_inproj_scan",
    )(x, mod, norm1, norm2, wg, wu, wd, w_in_p, conv_w8, conv_b, dtb, alog, dskip, tri_lo, *to_bf16)


def _head_prep_t(t, w_b, cos, sin):
    ms = jnp.sum(t * t, axis=0, keepdims=True) * (1.0 / HEAD_DIM)
    tn = t * lax.rsqrt(ms + EPS) * w_b
    half = ROPE_DIMS // 2
    t1 = tn[0:half]
    t2 = tn[half:ROPE_DIMS]
    return jnp.concatenate([t1 * cos - t2 * sin, t2 * cos + t1 * sin, tn[ROPE_DIMS:]], axis=0)


def _attn_tile(q_ref, kvp_ref, kvc_ref, kvn_ref, csp_ref, csc_ref, csn_ref, qw_ref, kw_ref, sink_ref,
               has_prev, has_next, store):
    tq = q_ref.shape[1]
    nsub = tq // BLOCK
    nw = tq + 2 * BLOCK
    kvw = ATTN_KV_HEADS * HEAD_DIM
    half = ROPE_DIMS // 2
    q_per_kv = ATTN_HEADS // ATTN_KV_HEADS

    kv = jnp.concatenate([kvp_ref[0], kvc_ref[0], kvn_ref[0]], axis=0)
    cs = jnp.concatenate([csp_ref[0], csc_ref[0], csn_ref[0]], axis=1)
    k_t = kv[:, :kvw].T
    v_t = kv[:, kvw:].T.astype(BF16)
    kw_b = jnp.tile(kw_ref[...], (1, nw // LANES))
    k_prep = jnp.concatenate(
        [_head_prep_t(k_t[g * HEAD_DIM:(g + 1) * HEAD_DIM], kw_b, cs[0:half], cs[half:])
         for g in range(ATTN_KV_HEADS)], axis=0)
    keys = k_prep.T.astype(BF16)

    q_t = q_ref[0].T
    qw_b = jnp.tile(qw_ref[...] * (HEAD_DIM ** -0.5), (1, tq // LANES))
    cs_q = csc_ref[0]
    zeros = jnp.zeros((HEAD_DIM, tq), BF16)
    q_heads = []
    for h in range(ATTN_HEADS):
        qh = _head_prep_t(q_t[h * HEAD_DIM:(h + 1) * HEAD_DIM], qw_b, cs_q[0:half], cs_q[half:]).astype(BF16)
        q_heads.append(jnp.concatenate([qh, zeros] if h // q_per_kv == 0 else [zeros, qh], axis=0))

    yield
    kr = _iota2((BLOCK, BLOCK), 0)
    qc = _iota2((BLOCK, BLOCK), 1)
    bias_prev = jnp.where(kr >= qc, 0.0, NEG_BIG)
    bias_next = jnp.where(kr <= qc, 0.0, NEG_BIG)
    sink_row = jnp.concatenate([jnp.full((1, BLOCK), sink_ref[h], F32) for h in range(ATTN_HEADS)], axis=1)
    gq = q_per_kv * BLOCK
    for j in range(nsub):
        bp = bias_prev if j > 0 else jnp.where(has_prev, bias_prev, NEG_BIG)
        bn = bias_next if j < nsub - 1 else jnp.where(has_next, bias_next, NEG_BIG)
        lo, hi = j * BLOCK, (j + 3) * BLOCK
        q_all = jnp.concatenate([qh[:, lo:lo + BLOCK] for qh in q_heads], axis=1)
        s = _dot(keys[lo:hi], q_all)
        yield
        s0 = s[0:BLOCK] + jnp.tile(bp, (1, ATTN_HEADS))
        s1 = s[BLOCK:2 * BLOCK]
        s2 = s[2 * BLOCK:] + jnp.tile(bn, (1, ATTN_HEADS))
        m = jnp.maximum(jnp.max(jnp.maximum(jnp.maximum(s0, s1), s2), axis=0, keepdims=True), sink_row)
        p0 = jnp.exp(s0 - m)
        p1 = jnp.exp(s1 - m)
        p2 = jnp.exp(s2 - m)
        denom = jnp.sum(p0 + p1 + p2, axis=0, keepdims=True) + jnp.exp(sink_row - m)
        p = jnp.concatenate([p0, p1, p2], axis=0).astype(BF16)
        inv = 1.0 / denom
        outs = []
        for g in range(ATTN_KV_HEADS):
            o = _dot(v_t[g * HEAD_DIM:(g + 1) * HEAD_DIM, lo:hi], p[:, g * gq:(g + 1) * gq])
            o = o * inv[:, g * gq:(g + 1) * gq]
            outs += [o[:, k * BLOCK:(k + 1) * BLOCK] for k in range(q_per_kv)]
        store(lo, jnp.concatenate(outs, axis=0).T.astype(BF16))
        yield


def _tail_kernel(n_tiles, tiles_per_seq,
                 h_ref, mod_ref, n3_ref, wo_ref, wg_ref, wu_ref, wd_ref,
                 q_ref, kvp_ref, kvc_ref, kvn_ref, csp_ref, csc_ref, csn_ref, qw_ref, kw_ref, sink_ref,
                 bc_ref, xdtb_ref, yp_ref, dt_ref, z_ref, dtb_ref, alog_ref, nw_ref, tri_ref,
                 o_ref, y_scr, hstate_ref):
    k = pl.program_id(0)

    @pl.when(k == 0)
    def _():
        y_scr[...] = jnp.zeros_like(y_scr)
        hstate_ref[...] = jnp.zeros_like(hstate_ref)

    rd = k % 2
    wr = 1 - rd

    t = jnp.maximum(n_tiles - 1 - k, 0) % tiles_per_seq
    is_first = t == 0
    is_last = t == tiles_per_seq - 1

    def store_ssd(row0, y):
        y_scr[wr, row0:row0 + CHUNK, 0:SSD_WIDTH] = y

    def store_attn(row0, y):
        y_scr[wr, row0:row0 + BLOCK, SSD_WIDTH:] = y

    attn = _attn_tile(q_ref, kvp_ref, kvc_ref, kvn_ref, csp_ref, csc_ref, csn_ref, qw_ref, kw_ref, sink_ref,
                      jnp.logical_not(is_first), jnp.logical_not(is_last), store_attn)
    scan = _ssd_bwd_tile(bc_ref, xdtb_ref, yp_ref, dt_ref, z_ref, dtb_ref, alog_ref, nw_ref, tri_ref,
                         hstate_ref, is_last, store_ssd)

    h1 = h_ref[0]
    mod = mod_ref[0]
    mix = _dot(y_scr[rd], wo_ref[...])
    next(scan, None)
    h2 = h1 + (1.0 + mod[5:6]) * mix
    u = _ada_norm(h2, n3_ref[...] * (1.0 + mod[7:8]), mod[6:7]).astype(BF16)
    ff = _swiglu(u, wg_ref, wu_ref, wd_ref, side_work=(attn, scan))
    o_ref[0] = h2 + (0.5 * (1.0 + mod[8:9])) * ff


def _tail(h1, mod, norm3, w_out, wg, wu, wd, q, kv, cs_tab, qw_b, kw_b, sink,
          bc, xdtb, ypart, dt, z, dtb, alog, norm_w):
    b, s, d = h1.shape
    d_ff = wg.shape[1]
    tps = s // SEQ_ROWS
    n_tiles = b * tps
    nsub = SEQ_ROWS // BLOCK
    nb = s // BLOCK
    aw = q.shape[-1]
    kvw2 = kv.shape[-1]
    cs_rows = cs_tab.shape[1]
    _, tri_up = _ssd_constants()

    ffn_tile = lambda k: jnp.clip(n_tiles - k, 0, n_tiles - 1)
    mix_tile = lambda k: jnp.maximum(n_tiles - 1 - k, 0)
    prev_blk = lambda t: jnp.maximum(t * nsub - 1, 0)
    next_blk = lambda t: jnp.minimum((t + 1) * nsub, nb - 1)

    def ffn_tok(w):
        return pl.BlockSpec((1, SEQ_ROWS, w), lambda k: (ffn_tile(k) // tps, ffn_tile(k) % tps, 0))

    def mix_tok(w):
        return pl.BlockSpec((1, SEQ_ROWS, w), lambda k: (mix_tile(k) // tps, mix_tile(k) % tps, 0))

    in_specs = [
        ffn_tok(d),
        pl.BlockSpec((1, N_MOD, d), lambda k: (ffn_tile(k) // tps, 0, 0)),
        _const_spec((1, d)), _const_spec(w_out.shape),
        _const_spec((d, d_ff)), _const_spec((d, d_ff)), _const_spec((d_ff, d)),
        mix_tok(aw),
        pl.BlockSpec((1, BLOCK, kvw2), lambda k: (mix_tile(k) // tps, prev_blk(mix_tile(k) % tps), 0)),
        mix_tok(kvw2),
        pl.BlockSpec((1, BLOCK, kvw2), lambda k: (mix_tile(k) // tps, next_blk(mix_tile(k) % tps), 0)),
        pl.BlockSpec((1, cs_rows, BLOCK), lambda k: (mix_tile(k) // tps, 0, prev_blk(mix_tile(k) % tps))),
        pl.BlockSpec((1, cs_rows, SEQ_ROWS), lambda k: (mix_tile(k) // tps, 0, mix_tile(k) % tps)),
        pl.BlockSpec((1, cs_rows, BLOCK), lambda k: (mix_tile(k) // tps, 0, next_blk(mix_tile(k) % tps))),
        _const_spec((HEAD_DIM, LANES)), _const_spec((HEAD_DIM, LANES)),
        pl.BlockSpec(memory_space=pltpu.SMEM),
        mix_tok(BC_WIDTH), mix_tok(SSD_WIDTH), mix_tok(SSD_WIDTH), mix_tok(LANES), mix_tok(SSD_WIDTH),
        _const_spec((1, LANES)), _const_spec((1, LANES)), _const_spec((1, SSD_WIDTH)),
        _const_spec(tri_up.shape),
    ]
    return pl.pallas_call(
        functools.partial(_tail_kernel, n_tiles, tps),
        grid=(n_tiles + 1,),
        in_specs=in_specs,
        out_specs=ffn_tok(d),
        out_shape=jax.ShapeDtypeStruct((b, s, d), F32),
        scratch_shapes=[pltpu.VMEM((2, SEQ_ROWS, SSD_WIDTH + aw), BF16),
                        pltpu.VMEM((D_STATE, SSD_WIDTH), F32)],
        compiler_params=pltpu.CompilerParams(dimension_semantics=("arbitrary",),
                                             vmem_limit_bytes=VMEM_LIMIT_BYTES),
        name="mixers_outproj_ffn2",
    )(h1, mod, norm3, w_out, wg, wu, wd, q, kv, kv, kv, cs_tab, cs_tab, cs_tab, qw_b, kw_b, sink,
      bc, xdtb, ypart, dt, z, dtb, alog, norm_w, tri_up)


def _pad_inproj(w_in):
    n_dt = 2 * SSD_HEADS
    s_xbc = SSD_WIDTH + CONV_DIM
    s_dt = s_xbc + n_dt
    dt_cols = jnp.pad(w_in[:, s_xbc:s_dt], ((0, 0), (0, LANES - n_dt)))
    return jnp.concatenate([w_in[:, :s_xbc], w_in[:, s_dt:], dt_cols], axis=1).astype(BF16)


def _pad_lanes(v, width=LANES):
    v = v.reshape(1, -1)
    return jnp.pad(v, ((0, 0), (0, width - v.shape[1])))


def _lane_bcast(v):
    return jnp.broadcast_to(v[:, None], (v.shape[0], LANES))


def kernel(x, c, positions, w_ada, b_ada, norm_ffn1, ffn1_wg, ffn1_wu, ffn1_wd, norm_mix, w_in, conv_w,
           conv_b, dt_bias, a_log, d_skip, ssd_norm_w, q_norm_w, k_norm_w, sink_logit, w_out, norm_ffn2,
           ffn2_wg, ffn2_wu, ffn2_wd):
    depth = w_ada.shape[0]
    b, s, d = x.shape
    h = x.astype(F32)
    c_pad = jnp.pad(c.astype(F32), ((0, -b % SUBLANES), (0, 0)))
    cs_tab = _rope_table(positions)
    for l in range(depth):
        mod = _adaln_mod(c_pad, w_ada[l], b_ada[l])[:b].reshape(b, N_MOD, d)
        conv_w8 = jnp.pad(conv_w[l], ((0, SUBLANES - CONV_K), (0, 0)))
        dtb, alog = _pad_lanes(dt_bias[l]), _pad_lanes(a_log[l])
        h1, z, q, kv, dt, bc, xdtb, ypart, wo_b, wg2_b, wu2_b, wd2_b = _head(
            h, mod, norm_ffn1[l].reshape(1, d), norm_mix[l].reshape(1, d),
            ffn1_wg[l].astype(BF16), ffn1_wu[l].astype(BF16), ffn1_wd[l].astype(BF16), _pad_inproj(w_in[l]),
            conv_w8, conv_b[l].reshape(1, -1), dtb, alog, jnp.repeat(d_skip[l], SSD_HEAD_DIM).reshape(1, -1),
            to_bf16=(w_out[l], ffn2_wg[l], ffn2_wu[l], ffn2_wd[l]))
        h = _tail(h1, mod, norm_ffn2[l].reshape(1, d), wo_b, wg2_b, wu2_b, wd2_b,
                  q, kv, cs_tab, _lane_bcast(q_norm_w[l]), _lane_bcast(k_norm_w[l]), sink_logit[l],
                  bc, xdtb, ypart, dt, z, dtb, alog, ssd_norm_w[l].reshape(1, -1))
    return h.astype(x.dtype)
```

```python
import functools

import jax
import jax.numpy as jnp
import numpy as np
from jax import lax
from jax.experimental import pallas as pl
from jax.experimental.pallas import tpu as pltpu

F32 = jnp.float32
BF16 = jnp.bfloat16

SSD_HEAD_DIM = 64
SSD_HEADS = 8
SSD_GROUPS = 2
D_STATE = 128
CONV_K = 5
CHUNK = 128
HEAD_DIM = 64
ATTN_HEADS = 8
ATTN_KV_HEADS = 2
WINDOW = 128
BLOCK = 128
ROPE_DIMS = 16
ROPE_THETA = 500000.0
N_MOD = 9
EPS = 1e-6

SSD_WIDTH = SSD_HEADS * SSD_HEAD_DIM
BC_WIDTH = 2 * SSD_GROUPS * D_STATE
CONV_DIM = SSD_WIDTH + BC_WIDTH

LANES = 128
SUBLANES = 8
VMEM_LIMIT_BYTES = 56 * 1024 * 1024

SEQ_ROWS = 512
FFN_CHUNK = 256
CONV_HALO = 16
FETCH_ROWS_WIDE = 32
FETCH_ROWS_NARROW = 64

NEG_BIG = -1e30


def _dot(a, b):
    return jnp.dot(a, b, preferred_element_type=F32)


def _dot_nt(a, b):
    return lax.dot_general(a, b, (((1,), (1,)), ((), ())), preferred_element_type=F32)


def _row_sums(tri_bf16, x, terms=3):
    acc = None
    r = x
    for t in range(terms):
        h = r.astype(BF16)
        d = _dot(tri_bf16, h)
        acc = d if acc is None else acc + d
        if t + 1 < terms:
            r = r - h.astype(F32)
    return acc


def _silu(x):
    return x * jax.nn.sigmoid(x)


def _iota2(shape, dim):
    return lax.broadcasted_iota(jnp.int32, shape, dim)


def _const_spec(shape):
    nd = len(shape)
    return pl.BlockSpec(shape, lambda *_: (0,) * nd, pipeline_mode=pl.Buffered(1))


def _mod_kernel(c_ref, w_ref, b_ref, o_ref):
    cs = _silu(c_ref[...])
    o_ref[...] = _dot(cs, w_ref[...]) + b_ref[...]


def _adaln_mod(c_pad, w_ada, b_ada):
    rows, d = c_pad.shape
    n = w_ada.shape[1]
    bn = 3 * d
    return pl.pallas_call(
        _mod_kernel,
        grid=(n // bn,),
        in_specs=[pl.BlockSpec((rows, d), lambda j: (0, 0)),
                  pl.BlockSpec((d, bn), lambda j: (0, j)),
                  pl.BlockSpec((1, bn), lambda j: (0, j))],
        out_specs=pl.BlockSpec((rows, bn), lambda j: (0, j)),
        out_shape=jax.ShapeDtypeStruct((rows, n), F32),
        name="adaln_mod",
    )(c_pad, w_ada, b_ada.reshape(1, n))


def _rope_kernel(pos_ref, inv_ref, o_ref):
    half = ROPE_DIMS // 2
    s = pos_ref.shape[-1]
    p = pos_ref[0].astype(F32)
    ang = jnp.tile(inv_ref[...], (1, s // LANES)) * p
    o_ref[0, 0:half, :] = jnp.cos(ang)
    o_ref[0, half:, :] = jnp.sin(ang)


def _rope_table(positions):
    half = ROPE_DIMS // 2
    inv = ROPE_THETA ** (-jnp.arange(half, dtype=F32) * 2.0 / ROPE_DIMS)
    b, s = positions.shape
    return pl.pallas_call(
        _rope_kernel,
        grid=(b,),
        in_specs=[pl.BlockSpec((1, 1, s), lambda bi: (bi, 0, 0)), _const_spec((half, LANES))],
        out_specs=pl.BlockSpec((1, 2 * half, s), lambda bi: (bi, 0, 0)),
        out_shape=jax.ShapeDtypeStruct((b, 2 * half, s), F32),
        name="rope_table",
    )(positions.reshape(b, 1, s), jnp.broadcast_to(inv[:, None], (half, LANES)))


def _ada_norm(x, gain_scale, shift):
    ms = jnp.mean(x * x, axis=-1, keepdims=True)
    return x * lax.rsqrt(ms + EPS) * gain_scale + shift


def _swiglu(ub, wg_ref, wu_ref, wd_ref, side_work=(), drain=True):
    d_ff = wg_ref.shape[1]
    acc = None
    for c0 in range(0, d_ff, FFN_CHUNK):
        c1 = min(c0 + FFN_CHUNK, d_ff)
        g = _dot(ub, wg_ref[:, c0:c1])
        up = _dot(ub, wu_ref[:, c0:c1])
        a = (_silu(g) * up).astype(BF16)
        d = _dot(a, wd_ref[c0:c1, :])
        acc = d if acc is None else acc + d
        for gen in side_work:
            next(gen, None)
    for gen in side_work if drain else ():
        for _ in gen:
            pass
    return acc


_INPROJ_LAYOUT = (("z", SSD_WIDTH, F32), ("xbc", CONV_DIM, BF16), ("q", ATTN_HEADS * HEAD_DIM, F32),
                  ("kv", 2 * ATTN_KV_HEADS * HEAD_DIM, F32), ("dt", LANES, F32))


def _ssd_constants():
    r = np.arange(CHUNK)[:, None]
    c = np.arange(CHUNK)[None, :]
    return jnp.asarray(c <= r, dtype=BF16), jnp.asarray(c >= r, dtype=BF16)


def _dt_and_da(dt_raw, dtb_ref, alog_ref):
    lane = _iota2(dt_raw.shape, 1)
    dt = jnp.where(lane < 2 * SSD_HEADS, jax.nn.softplus(dt_raw + dtb_ref[...]), 0.0)
    a = -jnp.exp(alog_ref[...])
    return dt, dt * a


def _head_columns(x, lane0):
    return [jnp.broadcast_to(x[:, lane0 + h:lane0 + h + 1], x.shape) for h in range(SSD_HEADS)]


def _expand_heads(cols):
    first = _iota2(cols[0].shape, 1) < SSD_HEAD_DIM
    return jnp.concatenate([jnp.where(first, cols[2 * p], cols[2 * p + 1]) for p in range(SSD_HEADS // 2)],
                           axis=1)


def _ssd_chunk(direction, da, tri_ref, cm, bm, xdt_fn, h_ref):
    gw = SSD_WIDTH // SSD_GROUPS
    lane0 = 0 if direction == "fwd" else SSD_HEADS
    tot_row = CHUNK - 1 if direction == "fwd" else 0
    cgs = [cm[:, g * D_STATE:(g + 1) * D_STATE] for g in range(SSD_GROUPS)]
    bgs = [bm[:, g * D_STATE:(g + 1) * D_STATE] for g in range(SSD_GROUPS)]
    hgs = [h_ref[:, g * gw:(g + 1) * gw] for g in range(SSD_GROUPS)]

    cs = _row_sums(tri_ref[...], da)
    cbs = [_dot_nt(cgs[g], bgs[g]) for g in range(SSD_GROUPS)]
    chs = [_dot(cgs[g], hgs[g].astype(BF16)) for g in range(SSD_GROUPS)]
    yield

    xdt = xdt_fn()
    cs_cols = _head_columns(cs, lane0)
    csx = _expand_heads(cs_cols)
    tot = csx[tot_row:tot_row + 1, :]
    e_in = jnp.exp(csx)
    xdec = (xdt * jnp.exp(tot - csx)).astype(BF16)
    cdec = jnp.exp(tot)
    cst = cs.T
    row = _iota2((CHUNK, CHUNK), 0)
    col = _iota2((CHUNK, CHUNK), 1)
    keep = (col <= row) if direction == "fwd" else (col > row)
    half = _iota2((CHUNK, LANES), 1) // SSD_HEAD_DIM
    xdt_b = xdt.astype(BF16)
    ys = []
    for g in range(SSD_GROUPS):
        pairs = []
        for pp in range(gw // LANES):
            p = g * (gw // LANES) + pp
            xp = xdt_b[:, p * LANES:(p + 1) * LANES]
            yp = None
            for e in range(2):
                hcol = lane0 + 2 * p + e
                seg = cs_cols[2 * p + e] - cst[hcol:hcol + 1, :]
                m = (cbs[g] * jnp.exp(jnp.where(keep, seg, NEG_BIG))).astype(BF16)
                d = _dot(m, jnp.where(half == e, xp, jnp.zeros_like(xp)))
                yp = d if yp is None else yp + d
            pairs.append(yp)
        ys.append(jnp.concatenate(pairs, axis=1) + chs[g] * e_in[:, g * gw:(g + 1) * gw])
        bgt = bgs[g].astype(F32).T.astype(BF16)
        st = _dot(bgt, xdec[:, g * gw:(g + 1) * gw])
        h_ref[:, g * gw:(g + 1) * gw] = hgs[g] * cdec[:, g * gw:(g + 1) * gw] + st
    return jnp.concatenate(ys, axis=1)


def _ssd_fwd_tile(window_fn, dt_fn, cw_ref, cbias_ref, dtb_ref, alog_ref, dskip_ref,
                  tri_ref, bc_ref, xdtb_ref, yp_ref, h_ref):
    mid = CONV_K // 2
    for j in range(SEQ_ROWS // CHUNK):
        rows = pl.ds(j * CHUNK, CHUNK)
        win = window_fn(j)
        dt, da = _dt_and_da(dt_fn(j), dtb_ref, alog_ref)
        dtx_f = _expand_heads(_head_columns(dt, 0))
        dtx_b = _expand_heads(_head_columns(dt, SSD_HEADS))
        yield
        winf = win.astype(F32)
        acc = cbias_ref[...] + winf[CONV_HALO:CONV_HALO + CHUNK] * cw_ref[mid:mid + 1, :]
        for k in [k for k in range(CONV_K) if k != mid]:
            rolled = pltpu.roll(winf, (mid - k) % winf.shape[0], axis=0)
            acc = acc + rolled[CONV_HALO:CONV_HALO + CHUNK] * cw_ref[k:k + 1, :]
        act = _silu(acc)
        xs = act[:, :SSD_WIDTH]
        bc = act[:, SSD_WIDTH:].astype(BF16)
        xdtb_ref[0, rows, :] = (xs * dtx_b).astype(BF16)
        bc_ref[0, rows, :] = bc
        y = yield from _ssd_chunk("fwd", da, tri_ref, bc[:, SSD_GROUPS * D_STATE:],
                                  bc[:, :SSD_GROUPS * D_STATE], lambda: xs * dtx_f, h_ref)
        yp_ref[0, rows, :] = y + xs * dskip_ref[...]
        yield


def _ssd_bwd_tile(bc_ref, xdtb_ref, yp_ref, dt_ref, z_ref, dtb_ref, alog_ref, nw_ref, tri_ref,
                  h_ref, reset, store):
    h_ref[...] = jnp.where(reset, 0.0, h_ref[...])
    for j in reversed(range(bc_ref.shape[1] // CHUNK)):
        rows = pl.ds(j * CHUNK, CHUNK)
        bc = bc_ref[0, rows, :]
        _, da = _dt_and_da(dt_ref[0, rows, :], dtb_ref, alog_ref)
        y = yield from _ssd_chunk("bwd", da, tri_ref, bc[:, SSD_GROUPS * D_STATE:],
                                  bc[:, :SSD_GROUPS * D_STATE],
                                  lambda: xdtb_ref[0, rows, :].astype(F32), h_ref)
        y = (y + yp_ref[0, rows, :]) * _silu(z_ref[0, rows, :])
        ms = jnp.mean(y * y, axis=-1, keepdims=True)
        store(j * CHUNK, (y * lax.rsqrt(ms + EPS) * nw_ref[...]).astype(BF16))
        yield


RING = 3


def _fetch_as_bf16(src_hbm, dst, stage, sem):
    rows = stage.shape[1]
    n_chunks = src_hbm.shape[0] // rows

    def copy(c, slot):
        return pltpu.make_async_copy(src_hbm.at[pl.ds(c * rows, rows)], stage.at[slot], sem.at[slot])

    copy(0, 0).start()

    def body(c, carry):
        slot = c % 2

        @pl.when(c + 1 < n_chunks)
        def _():
            copy(c + 1, 1 - slot).start()

        copy(c, slot).wait()
        dst[pl.ds(pl.multiple_of(c * rows, rows), rows), :] = stage[slot].astype(BF16)
        return carry

    lax.fori_loop(0, n_chunks, body, 0)


N_HEAD_SCRATCH = 12


def _head_kernel(n_tiles, tiles_per_seq,
                 x_ref, mod_ref, n1_ref, n2_ref, wg_hbm, wu_hbm, wd_hbm, win_ref,
                 cw_ref, cbias_ref, dtb_ref, alog_ref, dskip_ref, tri_ref,
                 *rest):
    n_cast = (len(rest) - 8 - N_HEAD_SCRATCH) // 2
    cast_src, rest = rest[:n_cast], rest[n_cast:]
    h_ref, z_ref, q_ref, kv_ref, dt_ref, bc_ref, xdtb_ref, yp_ref = rest[:8]
    cast_dst = rest[8:8 + n_cast]
    (xstage, dtstage, xring, dtring, hstate_ref,
     wg_ref, wu_ref, wd_ref, stage_gu, stage_d, sem_gu, sem_d) = rest[8 + n_cast:]
    k = pl.program_id(0)

    for src, dst in zip(cast_src, cast_dst):
        dst[...] = src[...].astype(BF16)

    @pl.when(k == 0)
    def _():
        _fetch_as_bf16(wg_hbm, wg_ref, stage_gu, sem_gu)
        _fetch_as_bf16(wu_hbm, wu_ref, stage_gu, sem_gu)
        _fetch_as_bf16(wd_hbm, wd_ref, stage_d, sem_d)
        xstage[...] = jnp.zeros_like(xstage)
        dtstage[...] = jnp.zeros_like(dtstage)
        xring[...] = jnp.zeros_like(xring)
        dtring[...] = jnp.zeros_like(dtring)
        hstate_ref[...] = jnp.zeros_like(hstate_ref)

    t = jnp.maximum(k - 2, 0) % tiles_per_seq
    s_new = (k + RING - 1) % RING
    s_main = (k + RING - 2) % RING
    s_prev = k % RING
    has_prev = t > 0
    has_next = t < tiles_per_seq - 1

    def advance_ring():
        xring[s_new] = xstage[...]
        dtring[s_new] = dtstage[...]
        hstate_ref[...] = jnp.where(has_prev, hstate_ref[...], 0.0)

    def window(j):
        lo = j * CHUNK - CONV_HALO
        hi = (j + 1) * CHUNK + CONV_HALO
        parts = []
        if lo < 0:
            halo = xring[s_prev, SEQ_ROWS + lo:, :]
            parts.append(jnp.where(has_prev, halo, jnp.zeros_like(halo)))
        parts.append(xring[s_main, max(lo, 0):min(hi, SEQ_ROWS), :])
        if hi > SEQ_ROWS:
            halo = xring[s_new, 0:hi - SEQ_ROWS, :]
            parts.append(jnp.where(has_next, halo, jnp.zeros_like(halo)))
        return parts[0] if len(parts) == 1 else jnp.concatenate(parts, axis=0)

    advance_ring()
    scan = _ssd_fwd_tile(window, lambda j: dtring[s_main, j * CHUNK:(j + 1) * CHUNK, :],
                         cw_ref, cbias_ref, dtb_ref, alog_ref, dskip_ref, tri_ref,
                         bc_ref, xdtb_ref, yp_ref, hstate_ref)

    x = x_ref[0]
    mod = mod_ref[0]
    u = _ada_norm(x, n1_ref[...] * (1.0 + mod[1:2]), mod[0:1]).astype(BF16)
    ff = _swiglu(u, wg_ref, wu_ref, wd_ref, side_work=(scan,), drain=False)
    h = x + (0.5 * (1.0 + mod[2:3])) * ff
    h_ref[0] = h
    u2 = _ada_norm(h, n2_ref[...] * (1.0 + mod[4:5]), mod[3:4]).astype(BF16)
    dests = {"z": (z_ref,), "xbc": (xstage,), "q": (q_ref,), "kv": (kv_ref,), "dt": (dt_ref, dtstage)}
    c0 = 0
    for name, w, _ in _INPROJ_LAYOUT:
        piece = _dot(u2, win_ref[:, c0:c0 + w])
        for ref in dests[name]:
            if len(ref.shape) == 3:
                ref[0] = piece.astype(ref.dtype)
            else:
                ref[...] = piece.astype(ref.dtype)
        c0 += w
        next(scan, None)
    for _ in scan:
        pass


def _row_block(rows, max_blocks):
    tile = 2 * SUBLANES
    return next(r for r in range(tile, rows + 1, tile) if rows % r == 0 and rows // r <= max_blocks)


def _head(x, mod, norm1, norm2, wg, wu, wd, w_in_p, conv_w8, conv_b, dtb, alog, dskip, to_bf16):
    b, s, d = x.shape
    d_ff = wg.shape[1]
    tps = s // SEQ_ROWS
    n_tiles = b * tps
    tri_lo, _ = _ssd_constants()
    hbm = pl.BlockSpec(memory_space=pl.ANY)
    assert d % FETCH_ROWS_WIDE == 0 and d_ff % FETCH_ROWS_NARROW == 0
    ffn_tile = lambda k: jnp.minimum(k, n_tiles - 1)
    scan_tile = lambda k: jnp.maximum(k - 2, 0)

    def ffn_tok(w):
        return pl.BlockSpec((1, SEQ_ROWS, w), lambda k: (ffn_tile(k) // tps, ffn_tile(k) % tps, 0))

    def scan_tok(w):
        return pl.BlockSpec((1, SEQ_ROWS, w), lambda k: (scan_tile(k) // tps, scan_tile(k) % tps, 0))

    widths = {name: (w, dt) for name, w, dt in _INPROJ_LAYOUT}
    hbm_outs = ("z", "q", "kv", "dt")
    out_shape = [jax.ShapeDtypeStruct((b, s, d), F32)]
    out_shape += [jax.ShapeDtypeStruct((b, s, widths[n][0]), widths[n][1]) for n in hbm_outs]
    out_shape += [jax.ShapeDtypeStruct((b, s, BC_WIDTH), BF16), jax.ShapeDtypeStruct((b, s, SSD_WIDTH), BF16),
                  jax.ShapeDtypeStruct((b, s, SSD_WIDTH), F32)]
    out_specs = [ffn_tok(d)] + [ffn_tok(widths[n][0]) for n in hbm_outs]
    out_specs += [scan_tok(BC_WIDTH), scan_tok(SSD_WIDTH), scan_tok(SSD_WIDTH)]
    cast_specs = []
    for w in to_bf16:
        rb = _row_block(w.shape[0], n_tiles)
        cast_specs.append(pl.BlockSpec((rb, w.shape[1]),
                                       lambda k, last=w.shape[0] // rb - 1: (jnp.minimum(k, last), 0)))
    out_shape += [jax.ShapeDtypeStruct(w.shape, BF16) for w in to_bf16]
    out_specs += cast_specs
    return pl.pallas_call(
        functools.partial(_head_kernel, n_tiles, tps),
        grid=(n_tiles + 2,),
        in_specs=[ffn_tok(d),
                  pl.BlockSpec((1, N_MOD, d), lambda k: (ffn_tile(k) // tps, 0, 0)),
                  _const_spec((1, d)), _const_spec((1, d)),
                  hbm, hbm, hbm,
                  _const_spec(w_in_p.shape),
                  _const_spec(conv_w8.shape), _const_spec((1, CONV_DIM)),
                  _const_spec((1, LANES)), _const_spec((1, LANES)), _const_spec((1, SSD_WIDTH)),
                  _const_spec(tri_lo.shape)] + cast_specs,
        out_specs=out_specs,
        out_shape=out_shape,
        scratch_shapes=[pltpu.VMEM((SEQ_ROWS, CONV_DIM), BF16), pltpu.VMEM((SEQ_ROWS, LANES), F32),
                        pltpu.VMEM((RING, SEQ_ROWS, CONV_DIM), BF16), pltpu.VMEM((RING, SEQ_ROWS, LANES), F32),
                        pltpu.VMEM((D_STATE, SSD_WIDTH), F32),
                        pltpu.VMEM((d, d_ff), BF16), pltpu.VMEM((d, d_ff), BF16), pltpu.VMEM((d_ff, d), BF16),
                        pltpu.VMEM((2, FETCH_ROWS_WIDE, d_ff), F32), pltpu.VMEM((2, FETCH_ROWS_NARROW, d), F32),
                        pltpu.SemaphoreType.DMA((2,)), pltpu.SemaphoreType.DMA((2,))],
        compiler_params=pltpu.CompilerParams(dimension_semantics=("arbitrary",),
                                             vmem_limit_bytes=VMEM_LIMIT_BYTES),
        name="ffn1_inproj_scan",
    )(x, mod, norm1, norm2, wg, wu, wd, w_in_p, conv_w8, conv_b, dtb, alog, dskip, tri_lo, *to_bf16)


def _head_prep_t(t, w_b, cos, sin):
    ms = jnp.sum(t * t, axis=0, keepdims=True) * (1.0 / HEAD_DIM)
    tn = t * lax.rsqrt(ms + EPS) * w_b
    half = ROPE_DIMS // 2
    t1 = tn[0:half]
    t2 = tn[half:ROPE_DIMS]
    return jnp.concatenate([t1 * cos - t2 * sin, t2 * cos + t1 * sin, tn[ROPE_DIMS:]], axis=0)


def _attn_tile(q_ref, kvp_ref, kvc_ref, kvn_ref, csp_ref, csc_ref, csn_ref, qw_ref, kw_ref, sink_ref,
               has_prev, has_next, store):
    tq = q_ref.shape[1]
    nsub = tq // BLOCK
    nw = tq + 2 * BLOCK
    kvw = ATTN_KV_HEADS * HEAD_DIM
    half = ROPE_DIMS // 2
    q_per_kv = ATTN_HEADS // ATTN_KV_HEADS

    kv = jnp.concatenate([kvp_ref[0], kvc_ref[0], kvn_ref[0]], axis=0)
    cs = jnp.concatenate([csp_ref[0], csc_ref[0], csn_ref[0]], axis=1)
    k_t = kv[:, :kvw].T
    v_t = kv[:, kvw:].T.astype(BF16)
    kw_b = jnp.tile(kw_ref[...], (1, nw // LANES))
    k_prep = jnp.concatenate(
        [_head_prep_t(k_t[g * HEAD_DIM:(g + 1) * HEAD_DIM], kw_b, cs[0:half], cs[half:])
         for g in range(ATTN_KV_HEADS)], axis=0)
    keys = k_prep.T.astype(BF16)

    q_t = q_ref[0].T
    qw_b = jnp.tile(qw_ref[...] * (HEAD_DIM ** -0.5), (1, tq // LANES))
    cs_q = csc_ref[0]
    zeros = jnp.zeros((HEAD_DIM, tq), BF16)
    q_heads = []
    for h in range(ATTN_HEADS):
        qh = _head_prep_t(q_t[h * HEAD_DIM:(h + 1) * HEAD_DIM], qw_b, cs_q[0:half], cs_q[half:]).astype(BF16)
        q_heads.append(jnp.concatenate([qh, zeros] if h // q_per_kv == 0 else [zeros, qh], axis=0))

    yield
    kr = _iota2((BLOCK, BLOCK), 0)
    qc = _iota2((BLOCK, BLOCK), 1)
    bias_prev = jnp.where(kr >= qc, 0.0, NEG_BIG)
    bias_next = jnp.where(kr <= qc, 0.0, NEG_BIG)
    sink_row = jnp.concatenate([jnp.full((1, BLOCK), sink_ref[h], F32) for h in range(ATTN_HEADS)], axis=1)
    gq = q_per_kv * BLOCK
    for j in range(nsub):
        bp = bias_prev if j > 0 else jnp.where(has_prev, bias_prev, NEG_BIG)
        bn = bias_next if j < nsub - 1 else jnp.where(has_next, bias_next, NEG_BIG)
        lo, hi = j * BLOCK, (j + 3) * BLOCK
        q_all = jnp.concatenate([qh[:, lo:lo + BLOCK] for qh in q_heads], axis=1)
        s = _dot(keys[lo:hi], q_all)
        yield
        s0 = s[0:BLOCK] + jnp.tile(bp, (1, ATTN_HEADS))
        s1 = s[BLOCK:2 * BLOCK]
        s2 = s[2 * BLOCK:] + jnp.tile(bn, (1, ATTN_HEADS))
        m = jnp.maximum(jnp.max(jnp.maximum(jnp.maximum(s0, s1), s2), axis=0, keepdims=True), sink_row)
        p0 = jnp.exp(s0 - m)
        p1 = jnp.exp(s1 - m)
        p2 = jnp.exp(s2 - m)
        denom = jnp.sum(p0 + p1 + p2, axis=0, keepdims=True) + jnp.exp(sink_row - m)
        p = jnp.concatenate([p0, p1, p2], axis=0).astype(BF16)
        inv = 1.0 / denom
        outs = []
        for g in range(ATTN_KV_HEADS):
            o = _dot(v_t[g * HEAD_DIM:(g + 1) * HEAD_DIM, lo:hi], p[:, g * gq:(g + 1) * gq])
            o = o * inv[:, g * gq:(g + 1) * gq]
            outs += [o[:, k * BLOCK:(k + 1) * BLOCK] for k in range(q_per_kv)]
        store(lo, jnp.concatenate(outs, axis=0).T.astype(BF16))
        yield


def _tail_kernel(n_tiles, tiles_per_seq,
                 h_ref, mod_ref, n3_ref, wo_ref, wg_ref, wu_ref, wd_ref,
                 q_ref, kvp_ref, kvc_ref, kvn_ref, csp_ref, csc_ref, csn_ref, qw_ref, kw_ref, sink_ref,
                 bc_ref, xdtb_ref, yp_ref, dt_ref, z_ref, dtb_ref, alog_ref, nw_ref, tri_ref,
                 o_ref, y_scr, hstate_ref):
    k = pl.program_id(0)

    @pl.when(k == 0)
    def _():
        y_scr[...] = jnp.zeros_like(y_scr)
        hstate_ref[...] = jnp.zeros_like(hstate_ref)

    rd = k % 2
    wr = 1 - rd

    t = jnp.maximum(n_tiles - 1 - k, 0) % tiles_per_seq
    is_first = t == 0
    is_last = t == tiles_per_seq - 1

    def store_ssd(row0, y):
        y_scr[wr, row0:row0 + CHUNK, 0:SSD_WIDTH] = y

    def store_attn(row0, y):
        y_scr[wr, row0:row0 + BLOCK, SSD_WIDTH:] = y

    attn = _attn_tile(q_ref, kvp_ref, kvc_ref, kvn_ref, csp_ref, csc_ref, csn_ref, qw_ref, kw_ref, sink_ref,
                      jnp.logical_not(is_first), jnp.logical_not(is_last), store_attn)
    scan = _ssd_bwd_tile(bc_ref, xdtb_ref, yp_ref, dt_ref, z_ref, dtb_ref, alog_ref, nw_ref, tri_ref,
                         hstate_ref, is_last, store_ssd)

    h1 = h_ref[0]
    mod = mod_ref[0]
    mix = _dot(y_scr[rd], wo_ref[...])
    next(scan, None)
    h2 = h1 + (1.0 + mod[5:6]) * mix
    u = _ada_norm(h2, n3_ref[...] * (1.0 + mod[7:8]), mod[6:7]).astype(BF16)
    ff = _swiglu(u, wg_ref, wu_ref, wd_ref, side_work=(attn, scan))
    o_ref[0] = h2 + (0.5 * (1.0 + mod[8:9])) * ff


def _tail(h1, mod, norm3, w_out, wg, wu, wd, q, kv, cs_tab, qw_b, kw_b, sink,
          bc, xdtb, ypart, dt, z, dtb, alog, norm_w):
    b, s, d = h1.shape
    d_ff = wg.shape[1]
    tps = s // SEQ_ROWS
    n_tiles = b * tps
    nsub = SEQ_ROWS // BLOCK
    nb = s // BLOCK
    aw = q.shape[-1]
    kvw2 = kv.shape[-1]
    cs_rows = cs_tab.shape[1]
    _, tri_up = _ssd_constants()

    ffn_tile = lambda k: jnp.clip(n_tiles - k, 0, n_tiles - 1)
    mix_tile = lambda k: jnp.maximum(n_tiles - 1 - k, 0)
    prev_blk = lambda t: jnp.maximum(t * nsub - 1, 0)
    next_blk = lambda t: jnp.minimum((t + 1) * nsub, nb - 1)

    def ffn_tok(w):
        return pl.BlockSpec((1, SEQ_ROWS, w), lambda k: (ffn_tile(k) // tps, ffn_tile(k) % tps, 0))

    def mix_tok(w):
        return pl.BlockSpec((1, SEQ_ROWS, w), lambda k: (mix_tile(k) // tps, mix_tile(k) % tps, 0))

    in_specs = [
        ffn_tok(d),
        pl.BlockSpec((1, N_MOD, d), lambda k: (ffn_tile(k) // tps, 0, 0)),
        _const_spec((1, d)), _const_spec(w_out.shape),
        _const_spec((d, d_ff)), _const_spec((d, d_ff)), _const_spec((d_ff, d)),
        mix_tok(aw),
        pl.BlockSpec((1, BLOCK, kvw2), lambda k: (mix_tile(k) // tps, prev_blk(mix_tile(k) % tps), 0)),
        mix_tok(kvw2),
        pl.BlockSpec((1, BLOCK, kvw2), lambda k: (mix_tile(k) // tps, next_blk(mix_tile(k) % tps), 0)),
        pl.BlockSpec((1, cs_rows, BLOCK), lambda k: (mix_tile(k) // tps, 0, prev_blk(mix_tile(k) % tps))),
        pl.BlockSpec((1, cs_rows, SEQ_ROWS), lambda k: (mix_tile(k) // tps, 0, mix_tile(k) % tps)),
        pl.BlockSpec((1, cs_rows, BLOCK), lambda k: (mix_tile(k) // tps, 0, next_blk(mix_tile(k) % tps))),
        _const_spec((HEAD_DIM, LANES)), _const_spec((HEAD_DIM, LANES)),
        pl.BlockSpec(memory_space=pltpu.SMEM),
        mix_tok(BC_WIDTH), mix_tok(SSD_WIDTH), mix_tok(SSD_WIDTH), mix_tok(LANES), mix_tok(SSD_WIDTH),
        _const_spec((1, LANES)), _const_spec((1, LANES)), _const_spec((1, SSD_WIDTH)),
        _const_spec(tri_up.shape),
    ]
    return pl.pallas_call(
        functools.partial(_tail_kernel, n_tiles, tps),
        grid=(n_tiles + 1,),
        in_specs=in_specs,
        out_specs=ffn_tok(d),
        out_shape=jax.ShapeDtypeStruct((b, s, d), F32),
        scratch_shapes=[pltpu.VMEM((2, SEQ_ROWS, SSD_WIDTH + aw), BF16),
                        pltpu.VMEM((D_STATE, SSD_WIDTH), F32)],
        compiler_params=pltpu.CompilerParams(dimension_semantics=("arbitrary",),
                                             vmem_limit_bytes=VMEM_LIMIT_BYTES),
        name="mixers_outproj_ffn2",
    )(h1, mod, norm3, w_out, wg, wu, wd, q, kv, kv, kv, cs_tab, cs_tab, cs_tab, qw_b, kw_b, sink,
      bc, xdtb, ypart, dt, z, dtb, alog, norm_w, tri_up)


def _pad_inproj(w_in):
    n_dt = 2 * SSD_HEADS
    s_xbc = SSD_WIDTH + CONV_DIM
    s_dt = s_xbc + n_dt
    dt_cols = jnp.pad(w_in[:, s_xbc:s_dt], ((0, 0), (0, LANES - n_dt)))
    return jnp.concatenate([w_in[:, :s_xbc], w_in[:, s_dt:], dt_cols], axis=1).astype(BF16)


def _pad_lanes(v, width=LANES):
    v = v.reshape(1, -1)
    return jnp.pad(v, ((0, 0), (0, width - v.shape[1])))


def _lane_bcast(v):
    return jnp.broadcast_to(v[:, None], (v.shape[0], LANES))


def kernel(x, c, positions, w_ada, b_ada, norm_ffn1, ffn1_wg, ffn1_wu, ffn1_wd, norm_mix, w_in, conv_w,
           conv_b, dt_bias, a_log, d_skip, ssd_norm_w, q_norm_w, k_norm_w, sink_logit, w_out, norm_ffn2,
           ffn2_wg, ffn2_wu, ffn2_wd):
    depth = w_ada.shape[0]
    b, s, d = x.shape
    h = x.astype(F32)
    c_pad = jnp.pad(c.astype(F32), ((0, -b % SUBLANES), (0, 0)))
    cs_tab = _rope_table(positions)
    for l in range(depth):
        mod = _adaln_mod(c_pad, w_ada[l], b_ada[l])[:b].reshape(b, N_MOD, d)
        conv_w8 = jnp.pad(conv_w[l], ((0, SUBLANES - CONV_K), (0, 0)))
        dtb, alog = _pad_lanes(dt_bias[l]), _pad_lanes(a_log[l])
        h1, z, q, kv, dt, bc, xdtb, ypart, wo_b, wg2_b, wu2_b, wd2_b = _head(
            h, mod, norm_ffn1[l].reshape(1, d), norm_mix[l].reshape(1, d),
            ffn1_wg[l], ffn1_wu[l], ffn1_wd[l], _pad_inproj(w_in[l]),
            conv_w8, conv_b[l].reshape(1, -1), dtb, alog, jnp.repeat(d_skip[l], SSD_HEAD_DIM).reshape(1, -1),
            to_bf16=(w_out[l], ffn2_wg[l], ffn2_wu[l], ffn2_wd[l]))
        h = _tail(h1, mod, norm_ffn2[l].reshape(1, d), wo_b, wg2_b, wu2_b, wd2_b,
                  q, kv, cs_tab, _lane_bcast(q_norm_w[l]), _lane_bcast(k_norm_w[l]), sink_logit[l],
                  bc, xdtb, ypart, dt, z, dtb, alog, ssd_norm_w[l].reshape(1, -1))
    return h.astype(x.dtype)
```

```python
import functools

import jax
import jax.numpy as jnp
import numpy as np
from jax import lax
from jax.experimental import pallas as pl
from jax.experimental.pallas import tpu as pltpu

F32 = jnp.float32
BF16 = jnp.bfloat16

SSD_HEAD_DIM = 64
SSD_HEADS = 8
SSD_GROUPS = 2
D_STATE = 128
CONV_K = 5
CHUNK = 128
HEAD_DIM = 64
ATTN_HEADS = 8
ATTN_KV_HEADS = 2
WINDOW = 128
BLOCK = 128
ROPE_DIMS = 16
ROPE_THETA = 500000.0
N_MOD = 9
EPS = 1e-6

SSD_WIDTH = SSD_HEADS * SSD_HEAD_DIM
BC_WIDTH = 2 * SSD_GROUPS * D_STATE
CONV_DIM = SSD_WIDTH + BC_WIDTH

LANES = 128
SUBLANES = 8
VMEM_LIMIT_BYTES = 56 * 1024 * 1024

SEQ_ROWS = 512
FFN_CHUNK = 256
CONV_HALO = 16

NEG_BIG = -1e30


def _dot(a, b):
    return jnp.dot(a, b, preferred_element_type=F32)


def _dot_nt(a, b):
    return lax.dot_general(a, b, (((1,), (1,)), ((), ())), preferred_element_type=F32)


def _row_sums(tri_bf16, x, terms=3):
    acc = None
    r = x
    for t in range(terms):
        h = r.astype(BF16)
        d = _dot(tri_bf16, h)
        acc = d if acc is None else acc + d
        if t + 1 < terms:
            r = r - h.astype(F32)
    return acc


def _silu(x):
    return x * jax.nn.sigmoid(x)


def _iota2(shape, dim):
    return lax.broadcasted_iota(jnp.int32, shape, dim)


def _const_spec(shape):
    nd = len(shape)
    return pl.BlockSpec(shape, lambda *_: (0,) * nd, pipeline_mode=pl.Buffered(1))


def _mod_kernel(c_ref, w_ref, b_ref, o_ref):
    cs = _silu(c_ref[...])
    o_ref[...] = _dot(cs, w_ref[...]) + b_ref[...]


def _adaln_mod(c_pad, w_ada, b_ada):
    rows, d = c_pad.shape
    n = w_ada.shape[1]
    bn = 3 * d
    return pl.pallas_call(
        _mod_kernel,
        grid=(n // bn,),
        in_specs=[pl.BlockSpec((rows, d), lambda j: (0, 0)),
                  pl.BlockSpec((d, bn), lambda j: (0, j)),
                  pl.BlockSpec((1, bn), lambda j: (0, j))],
        out_specs=pl.BlockSpec((rows, bn), lambda j: (0, j)),
        out_shape=jax.ShapeDtypeStruct((rows, n), F32),
        name="adaln_mod",
    )(c_pad, w_ada, b_ada.reshape(1, n))


def _rope_tile(pos_ref, inv_ref, o_ref):
    half = ROPE_DIMS // 2
    s = pos_ref.shape[-1]
    p = pos_ref[0].astype(F32)
    ang = jnp.tile(inv_ref[...], (1, s // LANES)) * p
    o_ref[0, 0:half, :] = jnp.cos(ang)
    o_ref[0, half:, :] = jnp.sin(ang)


def _rope_inv_freq():
    half = ROPE_DIMS // 2
    inv = ROPE_THETA ** (-jnp.arange(half, dtype=F32) * 2.0 / ROPE_DIMS)
    return jnp.broadcast_to(inv[:, None], (half, LANES))


def _ada_norm(x, gain_scale, shift):
    ms = jnp.mean(x * x, axis=-1, keepdims=True)
    return x * lax.rsqrt(ms + EPS) * gain_scale + shift


def _swiglu(ub, wg_ref, wu_ref, wd_ref, side_work=(), drain=True):
    d_ff = wg_ref.shape[1]
    acc = None
    for c0 in range(0, d_ff, FFN_CHUNK):
        c1 = min(c0 + FFN_CHUNK, d_ff)
        g = _dot(ub, wg_ref[:, c0:c1])
        up = _dot(ub, wu_ref[:, c0:c1])
        a = (_silu(g) * up).astype(BF16)
        d = _dot(a, wd_ref[c0:c1, :])
        acc = d if acc is None else acc + d
        for gen in side_work:
            next(gen, None)
    for gen in side_work if drain else ():
        for _ in gen:
            pass
    return acc


_INPROJ_LAYOUT = (("z", SSD_WIDTH, F32), ("xbc", CONV_DIM, BF16), ("q", ATTN_HEADS * HEAD_DIM, F32),
                  ("kv", 2 * ATTN_KV_HEADS * HEAD_DIM, F32), ("dt", LANES, F32))


def _ssd_constants():
    r = np.arange(CHUNK)[:, None]
    c = np.arange(CHUNK)[None, :]
    return jnp.asarray(c <= r, dtype=BF16), jnp.asarray(c >= r, dtype=BF16)


def _dt_and_da(dt_raw, dtb_ref, alog_ref):
    lane = _iota2(dt_raw.shape, 1)
    dt = jnp.where(lane < 2 * SSD_HEADS, jax.nn.softplus(dt_raw + dtb_ref[...]), 0.0)
    a = -jnp.exp(alog_ref[...])
    return dt, dt * a


def _head_columns(x, lane0):
    return [jnp.broadcast_to(x[:, lane0 + h:lane0 + h + 1], x.shape) for h in range(SSD_HEADS)]


def _expand_heads(cols):
    first = _iota2(cols[0].shape, 1) < SSD_HEAD_DIM
    return jnp.concatenate([jnp.where(first, cols[2 * p], cols[2 * p + 1]) for p in range(SSD_HEADS // 2)],
                           axis=1)


def _ssd_chunk(direction, da, tri_ref, cm, bm, xdt_fn, h_ref):
    gw = SSD_WIDTH // SSD_GROUPS
    lane0 = 0 if direction == "fwd" else SSD_HEADS
    tot_row = CHUNK - 1 if direction == "fwd" else 0
    cgs = [cm[:, g * D_STATE:(g + 1) * D_STATE] for g in range(SSD_GROUPS)]
    bgs = [bm[:, g * D_STATE:(g + 1) * D_STATE] for g in range(SSD_GROUPS)]
    hgs = [h_ref[:, g * gw:(g + 1) * gw] for g in range(SSD_GROUPS)]

    cs = _row_sums(tri_ref[...], da)
    cbs = [_dot_nt(cgs[g], bgs[g]) for g in range(SSD_GROUPS)]
    chs = [_dot(cgs[g], hgs[g].astype(BF16)) for g in range(SSD_GROUPS)]
    yield

    xdt = xdt_fn()
    cs_cols = _head_columns(cs, lane0)
    csx = _expand_heads(cs_cols)
    tot = csx[tot_row:tot_row + 1, :]
    e_in = jnp.exp(csx)
    xdec = (xdt * jnp.exp(tot - csx)).astype(BF16)
    cdec = jnp.exp(tot)
    cst = cs.T
    row = _iota2((CHUNK, CHUNK), 0)
    col = _iota2((CHUNK, CHUNK), 1)
    keep = (col <= row) if direction == "fwd" else (col > row)
    half = _iota2((CHUNK, LANES), 1) // SSD_HEAD_DIM
    xdt_b = xdt.astype(BF16)
    ys = []
    for g in range(SSD_GROUPS):
        pairs = []
        for pp in range(gw // LANES):
            p = g * (gw // LANES) + pp
            xp = xdt_b[:, p * LANES:(p + 1) * LANES]
            yp = None
            for e in range(2):
                hcol = lane0 + 2 * p + e
                seg = cs_cols[2 * p + e] - cst[hcol:hcol + 1, :]
                m = (cbs[g] * jnp.exp(jnp.where(keep, seg, NEG_BIG))).astype(BF16)
                d = _dot(m, jnp.where(half == e, xp, jnp.zeros_like(xp)))
                yp = d if yp is None else yp + d
            pairs.append(yp)
        ys.append(jnp.concatenate(pairs, axis=1) + chs[g] * e_in[:, g * gw:(g + 1) * gw])
        bgt = bgs[g].astype(F32).T.astype(BF16)
        st = _dot(bgt, xdec[:, g * gw:(g + 1) * gw])
        h_ref[:, g * gw:(g + 1) * gw] = hgs[g] * cdec[:, g * gw:(g + 1) * gw] + st
    return jnp.concatenate(ys, axis=1)


def _ssd_fwd_tile(window_fn, dt_fn, cw_ref, cbias_ref, dtb_ref, alog_ref, dskip_ref,
                  tri_ref, bc_ref, xdtb_ref, yp_ref, h_ref):
    mid = CONV_K // 2
    for j in range(SEQ_ROWS // CHUNK):
        rows = pl.ds(j * CHUNK, CHUNK)
        win = window_fn(j)
        dt, da = _dt_and_da(dt_fn(j), dtb_ref, alog_ref)
        dtx_f = _expand_heads(_head_columns(dt, 0))
        dtx_b = _expand_heads(_head_columns(dt, SSD_HEADS))
        yield
        winf = win.astype(F32)
        acc = cbias_ref[...] + winf[CONV_HALO:CONV_HALO + CHUNK] * cw_ref[mid:mid + 1, :]
        for k in [k for k in range(CONV_K) if k != mid]:
            rolled = pltpu.roll(winf, (mid - k) % winf.shape[0], axis=0)
            acc = acc + rolled[CONV_HALO:CONV_HALO + CHUNK] * cw_ref[k:k + 1, :]
        act = _silu(acc)
        xs = act[:, :SSD_WIDTH]
        bc = act[:, SSD_WIDTH:].astype(BF16)
        xdtb_ref[0, rows, :] = (xs * dtx_b).astype(BF16)
        bc_ref[0, rows, :] = bc
        y = yield from _ssd_chunk("fwd", da, tri_ref, bc[:, SSD_GROUPS * D_STATE:],
                                  bc[:, :SSD_GROUPS * D_STATE], lambda: xs * dtx_f, h_ref)
        yp_ref[0, rows, :] = y + xs * dskip_ref[...]
        yield


def _ssd_bwd_tile(bc_ref, xdtb_ref, yp_ref, dt_ref, z_ref, dtb_ref, alog_ref, nw_ref, tri_ref,
                  h_ref, reset, store):
    h_ref[...] = jnp.where(reset, 0.0, h_ref[...])
    for j in reversed(range(bc_ref.shape[1] // CHUNK)):
        rows = pl.ds(j * CHUNK, CHUNK)
        bc = bc_ref[0, rows, :]
        _, da = _dt_and_da(dt_ref[0, rows, :], dtb_ref, alog_ref)
        y = yield from _ssd_chunk("bwd", da, tri_ref, bc[:, SSD_GROUPS * D_STATE:],
                                  bc[:, :SSD_GROUPS * D_STATE],
                                  lambda: xdtb_ref[0, rows, :].astype(F32), h_ref)
        y = (y + yp_ref[0, rows, :]) * _silu(z_ref[0, rows, :])
        ms = jnp.mean(y * y, axis=-1, keepdims=True)
        store(j * CHUNK, (y * lax.rsqrt(ms + EPS) * nw_ref[...]).astype(BF16))
        yield


RING = 3


def _head_kernel(n_tiles, tiles_per_seq,
                 x_ref, mod_ref, n1_ref, n2_ref, wg_ref, wu_ref, wd_ref, win_ref,
                 cw_ref, cbias_ref, dtb_ref, alog_ref, dskip_ref, tri_ref, pos_ref, inv_ref,
                 *rest):
    n_cast = (len(rest) - 14) // 2
    cast_src, rest = rest[:n_cast], rest[n_cast:]
    h_ref, z_ref, q_ref, kv_ref, dt_ref, bc_ref, xdtb_ref, yp_ref, cs_ref = rest[:9]
    cast_dst = rest[9:9 + n_cast]
    xstage, dtstage, xring, dtring, hstate_ref = rest[9 + n_cast:]
    k = pl.program_id(0)

    for src, dst in zip(cast_src, cast_dst):
        dst[...] = src[...].astype(BF16)
    _rope_tile(pos_ref, inv_ref, cs_ref)

    @pl.when(k == 0)
    def _():
        xstage[...] = jnp.zeros_like(xstage)
        dtstage[...] = jnp.zeros_like(dtstage)
        xring[...] = jnp.zeros_like(xring)
        dtring[...] = jnp.zeros_like(dtring)
        hstate_ref[...] = jnp.zeros_like(hstate_ref)

    t = jnp.maximum(k - 2, 0) % tiles_per_seq
    s_new = (k + RING - 1) % RING
    s_main = (k + RING - 2) % RING
    s_prev = k % RING
    has_prev = t > 0
    has_next = t < tiles_per_seq - 1

    def advance_ring():
        xring[s_new] = xstage[...]
        dtring[s_new] = dtstage[...]
        hstate_ref[...] = jnp.where(has_prev, hstate_ref[...], 0.0)

    def window(j):
        lo = j * CHUNK - CONV_HALO
        hi = (j + 1) * CHUNK + CONV_HALO
        parts = []
        if lo < 0:
            halo = xring[s_prev, SEQ_ROWS + lo:, :]
            parts.append(jnp.where(has_prev, halo, jnp.zeros_like(halo)))
        parts.append(xring[s_main, max(lo, 0):min(hi, SEQ_ROWS), :])
        if hi > SEQ_ROWS:
            halo = xring[s_new, 0:hi - SEQ_ROWS, :]
            parts.append(jnp.where(has_next, halo, jnp.zeros_like(halo)))
        return parts[0] if len(parts) == 1 else jnp.concatenate(parts, axis=0)

    advance_ring()
    scan = _ssd_fwd_tile(window, lambda j: dtring[s_main, j * CHUNK:(j + 1) * CHUNK, :],
                         cw_ref, cbias_ref, dtb_ref, alog_ref, dskip_ref, tri_ref,
                         bc_ref, xdtb_ref, yp_ref, hstate_ref)

    x = x_ref[0]
    mod = mod_ref[0]
    u = _ada_norm(x, n1_ref[...] * (1.0 + mod[1:2]), mod[0:1]).astype(BF16)
    ff = _swiglu(u, wg_ref, wu_ref, wd_ref, side_work=(scan,), drain=False)
    h = x + (0.5 * (1.0 + mod[2:3])) * ff
    h_ref[0] = h
    u2 = _ada_norm(h, n2_ref[...] * (1.0 + mod[4:5]), mod[3:4]).astype(BF16)
    dests = {"z": (z_ref,), "xbc": (xstage,), "q": (q_ref,), "kv": (kv_ref,), "dt": (dt_ref, dtstage)}
    c0 = 0
    for name, w, _ in _INPROJ_LAYOUT:
        piece = _dot(u2, win_ref[:, c0:c0 + w])
        for ref in dests[name]:
            if len(ref.shape) == 3:
                ref[0] = piece.astype(ref.dtype)
            else:
                ref[...] = piece.astype(ref.dtype)
        c0 += w
        next(scan, None)
    for _ in scan:
        pass


def _row_block(rows, max_blocks):
    tile = 2 * SUBLANES
    return next(r for r in range(tile, rows + 1, tile) if rows % r == 0 and rows // r <= max_blocks)


def _head(x, mod, norm1, norm2, wg, wu, wd, w_in_p, conv_w8, conv_b, dtb, alog, dskip, positions, to_bf16):
    b, s, d = x.shape
    d_ff = wg.shape[1]
    tps = s // SEQ_ROWS
    n_tiles = b * tps
    tri_lo, _ = _ssd_constants()
    ffn_tile = lambda k: jnp.minimum(k, n_tiles - 1)
    scan_tile = lambda k: jnp.maximum(k - 2, 0)

    def ffn_tok(w):
        return pl.BlockSpec((1, SEQ_ROWS, w), lambda k: (ffn_tile(k) // tps, ffn_tile(k) % tps, 0))

    def scan_tok(w):
        return pl.BlockSpec((1, SEQ_ROWS, w), lambda k: (scan_tile(k) // tps, scan_tile(k) % tps, 0))

    widths = {name: (w, dt) for name, w, dt in _INPROJ_LAYOUT}
    hbm_outs = ("z", "q", "kv", "dt")
    out_shape = [jax.ShapeDtypeStruct((b, s, d), F32)]
    out_shape += [jax.ShapeDtypeStruct((b, s, widths[n][0]), widths[n][1]) for n in hbm_outs]
    out_shape += [jax.ShapeDtypeStruct((b, s, BC_WIDTH), BF16), jax.ShapeDtypeStruct((b, s, SSD_WIDTH), BF16),
                  jax.ShapeDtypeStruct((b, s, SSD_WIDTH), F32), jax.ShapeDtypeStruct((b, ROPE_DIMS, s), F32)]
    out_specs = [ffn_tok(d)] + [ffn_tok(widths[n][0]) for n in hbm_outs]
    out_specs += [scan_tok(BC_WIDTH), scan_tok(SSD_WIDTH), scan_tok(SSD_WIDTH)]

    def ffn_lanes(rows):
        return pl.BlockSpec((1, rows, SEQ_ROWS), lambda k: (ffn_tile(k) // tps, 0, ffn_tile(k) % tps))

    out_specs.append(ffn_lanes(ROPE_DIMS))
    inv = _rope_inv_freq()
    cast_specs = []
    for w in to_bf16:
        rb = _row_block(w.shape[0], n_tiles)
        cast_specs.append(pl.BlockSpec((rb, w.shape[1]),
                                       lambda k, last=w.shape[0] // rb - 1: (jnp.minimum(k, last), 0)))
    out_shape += [jax.ShapeDtypeStruct(w.shape, BF16) for w in to_bf16]
    out_specs += cast_specs
    return pl.pallas_call(
        functools.partial(_head_kernel, n_tiles, tps),
        grid=(n_tiles + 2,),
        in_specs=[ffn_tok(d),
                  pl.BlockSpec((1, N_MOD, d), lambda k: (ffn_tile(k) // tps, 0, 0)),
                  _const_spec((1, d)), _const_spec((1, d)),
                  _const_spec((d, d_ff)), _const_spec((d, d_ff)), _const_spec((d_ff, d)),
                  _const_spec(w_in_p.shape),
                  _const_spec(conv_w8.shape), _const_spec((1, CONV_DIM)),
                  _const_spec((1, LANES)), _const_spec((1, LANES)), _const_spec((1, SSD_WIDTH)),
                  _const_spec(tri_lo.shape), ffn_lanes(1), _const_spec(inv.shape)] + cast_specs,
        out_specs=out_specs,
        out_shape=out_shape,
        scratch_shapes=[pltpu.VMEM((SEQ_ROWS, CONV_DIM), BF16), pltpu.VMEM((SEQ_ROWS, LANES), F32),
                        pltpu.VMEM((RING, SEQ_ROWS, CONV_DIM), BF16), pltpu.VMEM((RING, SEQ_ROWS, LANES), F32),
                        pltpu.VMEM((D_STATE, SSD_WIDTH), F32)],
        compiler_params=pltpu.CompilerParams(dimension_semantics=("arbitrary",),
                                             vmem_limit_bytes=VMEM_LIMIT_BYTES),
        name="ffn1_inproj_scan",
    )(x, mod, norm1, norm2, wg, wu, wd, w_in_p, conv_w8, conv_b, dtb, alog, dskip, tri_lo,
      positions.reshape(b, 1, s), inv, *to_bf16)


def _head_prep_t(t, w_b, cos, sin):
    ms = jnp.sum(t * t, axis=0, keepdims=True) * (1.0 / HEAD_DIM)
    tn = t * lax.rsqrt(ms + EPS) * w_b
    half = ROPE_DIMS // 2
    t1 = tn[0:half]
    t2 = tn[half:ROPE_DIMS]
    return jnp.concatenate([t1 * cos - t2 * sin, t2 * cos + t1 * sin, tn[ROPE_DIMS:]], axis=0)


def _attn_tile(q_ref, kvp_ref, kvc_ref, kvn_ref, csp_ref, csc_ref, csn_ref, qw_ref, kw_ref, sink_ref,
               has_prev, has_next, store):
    tq = q_ref.shape[1]
    nsub = tq // BLOCK
    nw = tq + 2 * BLOCK
    kvw = ATTN_KV_HEADS * HEAD_DIM
    half = ROPE_DIMS // 2
    q_per_kv = ATTN_HEADS // ATTN_KV_HEADS

    kv = jnp.concatenate([kvp_ref[0], kvc_ref[0], kvn_ref[0]], axis=0)
    cs = jnp.concatenate([csp_ref[0], csc_ref[0], csn_ref[0]], axis=1)
    k_t = kv[:, :kvw].T
    v_t = kv[:, kvw:].T.astype(BF16)
    kw_b = jnp.tile(kw_ref[...], (1, nw // LANES))
    k_prep = jnp.concatenate(
        [_head_prep_t(k_t[g * HEAD_DIM:(g + 1) * HEAD_DIM], kw_b, cs[0:half], cs[half:])
         for g in range(ATTN_KV_HEADS)], axis=0)
    keys = k_prep.T.astype(BF16)

    q_t = q_ref[0].T
    qw_b = jnp.tile(qw_ref[...] * (HEAD_DIM ** -0.5), (1, tq // LANES))
    cs_q = csc_ref[0]
    zeros = jnp.zeros((HEAD_DIM, tq), BF16)
    q_heads = []
    for h in range(ATTN_HEADS):
        qh = _head_prep_t(q_t[h * HEAD_DIM:(h + 1) * HEAD_DIM], qw_b, cs_q[0:half], cs_q[half:]).astype(BF16)
        q_heads.append(jnp.concatenate([qh, zeros] if h // q_per_kv == 0 else [zeros, qh], axis=0))

    yield
    kr = _iota2((BLOCK, BLOCK), 0)
    qc = _iota2((BLOCK, BLOCK), 1)
    bias_prev = jnp.where(kr >= qc, 0.0, NEG_BIG)
    bias_next = jnp.where(kr <= qc, 0.0, NEG_BIG)
    sink_row = jnp.concatenate([jnp.full((1, BLOCK), sink_ref[h], F32) for h in range(ATTN_HEADS)], axis=1)
    gq = q_per_kv * BLOCK
    for j in range(nsub):
        bp = bias_prev if j > 0 else jnp.where(has_prev, bias_prev, NEG_BIG)
        bn = bias_next if j < nsub - 1 else jnp.where(has_next, bias_next, NEG_BIG)
        lo, hi = j * BLOCK, (j + 3) * BLOCK
        q_all = jnp.concatenate([qh[:, lo:lo + BLOCK] for qh in q_heads], axis=1)
        s = _dot(keys[lo:hi], q_all)
        yield
        s0 = s[0:BLOCK] + jnp.tile(bp, (1, ATTN_HEADS))
        s1 = s[BLOCK:2 * BLOCK]
        s2 = s[2 * BLOCK:] + jnp.tile(bn, (1, ATTN_HEADS))
        m = jnp.maximum(jnp.max(jnp.maximum(jnp.maximum(s0, s1), s2), axis=0, keepdims=True), sink_row)
        p0 = jnp.exp(s0 - m)
        p1 = jnp.exp(s1 - m)
        p2 = jnp.exp(s2 - m)
        denom = jnp.sum(p0 + p1 + p2, axis=0, keepdims=True) + jnp.exp(sink_row - m)
        p = jnp.concatenate([p0, p1, p2], axis=0).astype(BF16)
        inv = 1.0 / denom
        outs = []
        for g in range(ATTN_KV_HEADS):
            o = _dot(v_t[g * HEAD_DIM:(g + 1) * HEAD_DIM, lo:hi], p[:, g * gq:(g + 1) * gq])
            o = o * inv[:, g * gq:(g + 1) * gq]
            outs += [o[:, k * BLOCK:(k + 1) * BLOCK] for k in range(q_per_kv)]
        store(lo, jnp.concatenate(outs, axis=0).T.astype(BF16))
        yield


def _tail_kernel(n_tiles, tiles_per_seq,
                 h_ref, mod_ref, n3_ref, wo_ref, wg_ref, wu_ref, wd_ref,
                 q_ref, kvp_ref, kvc_ref, kvn_ref, csp_ref, csc_ref, csn_ref, qw_ref, kw_ref, sink_ref,
                 bc_ref, xdtb_ref, yp_ref, dt_ref, z_ref, dtb_ref, alog_ref, nw_ref, tri_ref,
                 o_ref, y_scr, hstate_ref):
    k = pl.program_id(0)

    @pl.when(k == 0)
    def _():
        y_scr[...] = jnp.zeros_like(y_scr)
        hstate_ref[...] = jnp.zeros_like(hstate_ref)

    rd = k % 2
    wr = 1 - rd

    t = jnp.maximum(n_tiles - 1 - k, 0) % tiles_per_seq
    is_first = t == 0
    is_last = t == tiles_per_seq - 1

    def store_ssd(row0, y):
        y_scr[wr, row0:row0 + CHUNK, 0:SSD_WIDTH] = y

    def store_attn(row0, y):
        y_scr[wr, row0:row0 + BLOCK, SSD_WIDTH:] = y

    attn = _attn_tile(q_ref, kvp_ref, kvc_ref, kvn_ref, csp_ref, csc_ref, csn_ref, qw_ref, kw_ref, sink_ref,
                      jnp.logical_not(is_first), jnp.logical_not(is_last), store_attn)
    scan = _ssd_bwd_tile(bc_ref, xdtb_ref, yp_ref, dt_ref, z_ref, dtb_ref, alog_ref, nw_ref, tri_ref,
                         hstate_ref, is_last, store_ssd)

    h1 = h_ref[0]
    mod = mod_ref[0]
    mix = _dot(y_scr[rd], wo_ref[...])
    next(scan, None)
    h2 = h1 + (1.0 + mod[5:6]) * mix
    u = _ada_norm(h2, n3_ref[...] * (1.0 + mod[7:8]), mod[6:7]).astype(BF16)
    ff = _swiglu(u, wg_ref, wu_ref, wd_ref, side_work=(attn, scan))
    o_ref[0] = h2 + (0.5 * (1.0 + mod[8:9])) * ff


def _tail(h1, mod, norm3, w_out, wg, wu, wd, q, kv, cs_tab, qw_b, kw_b, sink,
          bc, xdtb, ypart, dt, z, dtb, alog, norm_w):
    b, s, d = h1.shape
    d_ff = wg.shape[1]
    tps = s // SEQ_ROWS
    n_tiles = b * tps
    nsub = SEQ_ROWS // BLOCK
    nb = s // BLOCK
    aw = q.shape[-1]
    kvw2 = kv.shape[-1]
    cs_rows = cs_tab.shape[1]
    _, tri_up = _ssd_constants()

    ffn_tile = lambda k: jnp.clip(n_tiles - k, 0, n_tiles - 1)
    mix_tile = lambda k: jnp.maximum(n_tiles - 1 - k, 0)
    prev_blk = lambda t: jnp.maximum(t * nsub - 1, 0)
    next_blk = lambda t: jnp.minimum((t + 1) * nsub, nb - 1)

    def ffn_tok(w):
        return pl.BlockSpec((1, SEQ_ROWS, w), lambda k: (ffn_tile(k) // tps, ffn_tile(k) % tps, 0))

    def mix_tok(w):
        return pl.BlockSpec((1, SEQ_ROWS, w), lambda k: (mix_tile(k) // tps, mix_tile(k) % tps, 0))

    in_specs = [
        ffn_tok(d),
        pl.BlockSpec((1, N_MOD, d), lambda k: (ffn_tile(k) // tps, 0, 0)),
        _const_spec((1, d)), _const_spec(w_out.shape),
        _const_spec((d, d_ff)), _const_spec((d, d_ff)), _const_spec((d_ff, d)),
        mix_tok(aw),
        pl.BlockSpec((1, BLOCK, kvw2), lambda k: (mix_tile(k) // tps, prev_blk(mix_tile(k) % tps), 0)),
        mix_tok(kvw2),
        pl.BlockSpec((1, BLOCK, kvw2), lambda k: (mix_tile(k) // tps, next_blk(mix_tile(k) % tps), 0)),
        pl.BlockSpec((1, cs_rows, BLOCK), lambda k: (mix_tile(k) // tps, 0, prev_blk(mix_tile(k) % tps))),
        pl.BlockSpec((1, cs_rows, SEQ_ROWS), lambda k: (mix_tile(k) // tps, 0, mix_tile(k) % tps)),
        pl.BlockSpec((1, cs_rows, BLOCK), lambda k: (mix_tile(k) // tps, 0, next_blk(mix_tile(k) % tps))),
        _const_spec((HEAD_DIM, LANES)), _const_spec((HEAD_DIM, LANES)),
        pl.BlockSpec(memory_space=pltpu.SMEM),
        mix_tok(BC_WIDTH), mix_tok(SSD_WIDTH), mix_tok(SSD_WIDTH), mix_tok(LANES), mix_tok(SSD_WIDTH),
        _const_spec((1, LANES)), _const_spec((1, LANES)), _const_spec((1, SSD_WIDTH)),
        _const_spec(tri_up.shape),
    ]
    return pl.pallas_call(
        functools.partial(_tail_kernel, n_tiles, tps),
        grid=(n_tiles + 1,),
        in_specs=in_specs,
        out_specs=ffn_tok(d),
        out_shape=jax.ShapeDtypeStruct((b, s, d), F32),
        scratch_shapes=[pltpu.VMEM((2, SEQ_ROWS, SSD_WIDTH + aw), BF16),
                        pltpu.VMEM((D_STATE, SSD_WIDTH), F32)],
        compiler_params=pltpu.CompilerParams(dimension_semantics=("arbitrary",),
                                             vmem_limit_bytes=VMEM_LIMIT_BYTES),
        name="mixers_outproj_ffn2",
    )(h1, mod, norm3, w_out, wg, wu, wd, q, kv, kv, kv, cs_tab, cs_tab, cs_tab, qw_b, kw_b, sink,
      bc, xdtb, ypart, dt, z, dtb, alog, norm_w, tri_up)


def _pad_inproj(w_in):
    n_dt = 2 * SSD_HEADS
    s_xbc = SSD_WIDTH + CONV_DIM
    s_dt = s_xbc + n_dt
    dt_cols = jnp.pad(w_in[:, s_xbc:s_dt], ((0, 0), (0, LANES - n_dt)))
    return jnp.concatenate([w_in[:, :s_xbc], w_in[:, s_dt:], dt_cols], axis=1).astype(BF16)


def _pad_lanes(v, width=LANES):
    v = v.reshape(1, -1)
    return jnp.pad(v, ((0, 0), (0, width - v.shape[1])))


def _lane_bcast(v):
    return jnp.broadcast_to(v[:, None], (v.shape[0], LANES))


def kernel(x, c, positions, w_ada, b_ada, norm_ffn1, ffn1_wg, ffn1_wu, ffn1_wd, norm_mix, w_in, conv_w,
           conv_b, dt_bias, a_log, d_skip, ssd_norm_w, q_norm_w, k_norm_w, sink_logit, w_out, norm_ffn2,
           ffn2_wg, ffn2_wu, ffn2_wd):
    depth = w_ada.shape[0]
    b, s, d = x.shape
    h = x.astype(F32)
    c_pad = jnp.pad(c.astype(F32), ((0, -b % SUBLANES), (0, 0)))
    for l in range(depth):
        mod = _adaln_mod(c_pad, w_ada[l], b_ada[l])[:b].reshape(b, N_MOD, d)
        conv_w8 = jnp.pad(conv_w[l], ((0, SUBLANES - CONV_K), (0, 0)))
        dtb, alog = _pad_lanes(dt_bias[l]), _pad_lanes(a_log[l])
        h1, z, q, kv, dt, bc, xdtb, ypart, cs_tab, wo_b, wg2_b, wu2_b, wd2_b = _head(
            h, mod, norm_ffn1[l].reshape(1, d), norm_mix[l].reshape(1, d),
            ffn1_wg[l].astype(BF16), ffn1_wu[l].astype(BF16), ffn1_wd[l].astype(BF16), _pad_inproj(w_in[l]),
            conv_w8, conv_b[l].reshape(1, -1), dtb, alog, jnp.repeat(d_skip[l], SSD_HEAD_DIM).reshape(1, -1),
            positions, to_bf16=(w_out[l], ffn2_wg[l], ffn2_wu[l], ffn2_wd[l]))
        h = _tail(h1, mod, norm_ffn2[l].reshape(1, d), wo_b, wg2_b, wu2_b, wd2_b,
                  q, kv, cs_tab, _lane_bcast(q_norm_w[l]), _lane_bcast(k_norm_w[l]), sink_logit[l],
                  bc, xdtb, ypart, dt, z, dtb, alog, ssd_norm_w[l].reshape(1, -1))
    return h.astype(x.dtype)
```

```python
import functools

import jax
import jax.numpy as jnp
import numpy as np
from jax import lax
from jax.experimental import pallas as pl
from jax.experimental.pallas import tpu as pltpu

F32 = jnp.float32
BF16 = jnp.bfloat16

SSD_HEAD_DIM = 64
SSD_HEADS = 8
SSD_GROUPS = 2
D_STATE = 128
CONV_K = 5
CHUNK = 128
HEAD_DIM = 64
ATTN_HEADS = 8
ATTN_KV_HEADS = 2
WINDOW = 128
BLOCK = 128
ROPE_DIMS = 16
ROPE_THETA = 500000.0
N_MOD = 9
EPS = 1e-6

SSD_WIDTH = SSD_HEADS * SSD_HEAD_DIM
BC_WIDTH = 2 * SSD_GROUPS * D_STATE
CONV_DIM = SSD_WIDTH + BC_WIDTH

LANES = 128
SUBLANES = 8
VMEM_LIMIT_BYTES = 56 * 1024 * 1024

SEQ_ROWS = 512
FFN_CHUNK = 256
CONV_HALO = 16

NEG_BIG = -1e30


def _dot(a, b):
    return jnp.dot(a, b, preferred_element_type=F32)


def _dot_nt(a, b):
    return lax.dot_general(a, b, (((1,), (1,)), ((), ())), preferred_element_type=F32)


def _row_sums(tri_bf16, x, terms=3):
    acc = None
    r = x
    for t in range(terms):
        h = r.astype(BF16)
        d = _dot(tri_bf16, h)
        acc = d if acc is None else acc + d
        if t + 1 < terms:
            r = r - h.astype(F32)
    return acc


def _silu(x):
    return x * jax.nn.sigmoid(x)


def _iota2(shape, dim):
    return lax.broadcasted_iota(jnp.int32, shape, dim)


def _const_spec(shape):
    nd = len(shape)
    return pl.BlockSpec(shape, lambda *_: (0,) * nd, pipeline_mode=pl.Buffered(1))


def _mod_kernel(c_ref, w_ref, b_ref, o_ref):
    cs = _silu(c_ref[...])
    o_ref[...] = _dot(cs, w_ref[...]) + b_ref[...]


def _adaln_mod(c_pad, w_ada, b_ada):
    rows, d = c_pad.shape
    n = w_ada.shape[1]
    bn = 3 * d
    return pl.pallas_call(
        _mod_kernel,
        grid=(n // bn,),
        in_specs=[pl.BlockSpec((rows, d), lambda j: (0, 0)),
                  pl.BlockSpec((d, bn), lambda j: (0, j)),
                  pl.BlockSpec((1, bn), lambda j: (0, j))],
        out_specs=pl.BlockSpec((rows, bn), lambda j: (0, j)),
        out_shape=jax.ShapeDtypeStruct((rows, n), F32),
        name="adaln_mod",
    )(c_pad, w_ada, b_ada.reshape(1, n))


def _rope_tile(pos_ref, inv_ref, o_ref):
    half = ROPE_DIMS // 2
    s = pos_ref.shape[-1]
    p = pos_ref[0].astype(F32)
    ang = jnp.tile(inv_ref[...], (1, s // LANES)) * p
    o_ref[0, 0:half, :] = jnp.cos(ang)
    o_ref[0, half:, :] = jnp.sin(ang)


def _rope_inv_freq():
    half = ROPE_DIMS // 2
    inv = ROPE_THETA ** (-jnp.arange(half, dtype=F32) * 2.0 / ROPE_DIMS)
    return jnp.broadcast_to(inv[:, None], (half, LANES))


def _ada_norm(x, gain_scale, shift):
    ms = jnp.mean(x * x, axis=-1, keepdims=True)
    return x * lax.rsqrt(ms + EPS) * gain_scale + shift


def _swiglu(ub, wg_ref, wu_ref, wd_ref, side_work=(), drain=True):
    d_ff = wg_ref.shape[1]
    acc = None
    for c0 in range(0, d_ff, FFN_CHUNK):
        c1 = min(c0 + FFN_CHUNK, d_ff)
        g = _dot(ub, wg_ref[:, c0:c1])
        up = _dot(ub, wu_ref[:, c0:c1])
        a = (_silu(g) * up).astype(BF16)
        d = _dot(a, wd_ref[c0:c1, :])
        acc = d if acc is None else acc + d
        for gen in side_work:
            next(gen, None)
    for gen in side_work if drain else ():
        for _ in gen:
            pass
    return acc


_INPROJ_LAYOUT = (("z", SSD_WIDTH, BF16), ("xbc", CONV_DIM, BF16), ("q", ATTN_HEADS * HEAD_DIM, BF16),
                  ("kv", 2 * ATTN_KV_HEADS * HEAD_DIM, BF16), ("dt", LANES, F32))


def _ssd_constants():
    r = np.arange(CHUNK)[:, None]
    c = np.arange(CHUNK)[None, :]
    return jnp.asarray(c <= r, dtype=BF16), jnp.asarray(c >= r, dtype=BF16)


def _dt_and_da(dt_raw, dtb_ref, alog_ref):
    lane = _iota2(dt_raw.shape, 1)
    dt = jnp.where(lane < 2 * SSD_HEADS, jax.nn.softplus(dt_raw + dtb_ref[...]), 0.0)
    a = -jnp.exp(alog_ref[...])
    return dt, dt * a


def _head_columns(x, lane0):
    return [jnp.broadcast_to(x[:, lane0 + h:lane0 + h + 1], x.shape) for h in range(SSD_HEADS)]


def _expand_heads(cols):
    first = _iota2(cols[0].shape, 1) < SSD_HEAD_DIM
    return jnp.concatenate([jnp.where(first, cols[2 * p], cols[2 * p + 1]) for p in range(SSD_HEADS // 2)],
                           axis=1)


def _ssd_chunk(direction, da, tri_ref, cm, bm, xdt_fn, h_ref):
    gw = SSD_WIDTH // SSD_GROUPS
    lane0 = 0 if direction == "fwd" else SSD_HEADS
    tot_row = CHUNK - 1 if direction == "fwd" else 0
    cgs = [cm[:, g * D_STATE:(g + 1) * D_STATE] for g in range(SSD_GROUPS)]
    bgs = [bm[:, g * D_STATE:(g + 1) * D_STATE] for g in range(SSD_GROUPS)]
    hgs = [h_ref[:, g * gw:(g + 1) * gw] for g in range(SSD_GROUPS)]

    cs = _row_sums(tri_ref[...], da)
    cbs = [_dot_nt(cgs[g], bgs[g]) for g in range(SSD_GROUPS)]
    chs = [_dot(cgs[g], hgs[g].astype(BF16)) for g in range(SSD_GROUPS)]
    yield

    xdt = xdt_fn()
    cs_cols = _head_columns(cs, lane0)
    csx = _expand_heads(cs_cols)
    tot = csx[tot_row:tot_row + 1, :]
    e_in = jnp.exp(csx)
    xdec = (xdt * jnp.exp(tot - csx)).astype(BF16)
    cdec = jnp.exp(tot)
    cst = cs.T
    row = _iota2((CHUNK, CHUNK), 0)
    col = _iota2((CHUNK, CHUNK), 1)
    keep = (col <= row) if direction == "fwd" else (col > row)
    half = _iota2((CHUNK, LANES), 1) // SSD_HEAD_DIM
    xdt_b = xdt.astype(BF16)
    ys = []
    for g in range(SSD_GROUPS):
        pairs = []
        for pp in range(gw // LANES):
            p = g * (gw // LANES) + pp
            xp = xdt_b[:, p * LANES:(p + 1) * LANES]
            yp = None
            for e in range(2):
                hcol = lane0 + 2 * p + e
                seg = cs_cols[2 * p + e] - cst[hcol:hcol + 1, :]
                m = (cbs[g] * jnp.exp(jnp.where(keep, seg, NEG_BIG))).astype(BF16)
                d = _dot(m, jnp.where(half == e, xp, jnp.zeros_like(xp)))
                yp = d if yp is None else yp + d
            pairs.append(yp)
        ys.append(jnp.concatenate(pairs, axis=1) + chs[g] * e_in[:, g * gw:(g + 1) * gw])
        bgt = bgs[g].astype(F32).T.astype(BF16)
        st = _dot(bgt, xdec[:, g * gw:(g + 1) * gw])
        h_ref[:, g * gw:(g + 1) * gw] = hgs[g] * cdec[:, g * gw:(g + 1) * gw] + st
    return jnp.concatenate(ys, axis=1)


def _ssd_fwd_tile(window_fn, dt_fn, cw_ref, cbias_ref, dtb_ref, alog_ref, dskip_ref,
                  tri_ref, bc_ref, xdtb_ref, yp_ref, h_ref):
    mid = CONV_K // 2
    for j in range(SEQ_ROWS // CHUNK):
        rows = pl.ds(j * CHUNK, CHUNK)
        win = window_fn(j)
        dt, da = _dt_and_da(dt_fn(j), dtb_ref, alog_ref)
        dtx_f = _expand_heads(_head_columns(dt, 0))
        dtx_b = _expand_heads(_head_columns(dt, SSD_HEADS))
        yield
        winf = win.astype(F32)
        acc = cbias_ref[...] + winf[CONV_HALO:CONV_HALO + CHUNK] * cw_ref[mid:mid + 1, :]
        for k in [k for k in range(CONV_K) if k != mid]:
            rolled = pltpu.roll(winf, (mid - k) % winf.shape[0], axis=0)
            acc = acc + rolled[CONV_HALO:CONV_HALO + CHUNK] * cw_ref[k:k + 1, :]
        act = _silu(acc)
        xs = act[:, :SSD_WIDTH]
        bc = act[:, SSD_WIDTH:].astype(BF16)
        xdtb_ref[0, rows, :] = (xs * dtx_b).astype(BF16)
        bc_ref[0, rows, :] = bc
        y = yield from _ssd_chunk("fwd", da, tri_ref, bc[:, SSD_GROUPS * D_STATE:],
                                  bc[:, :SSD_GROUPS * D_STATE], lambda: xs * dtx_f, h_ref)
        yp_ref[0, rows, :] = y + xs * dskip_ref[...]
        yield


def _ssd_bwd_tile(bc_ref, xdtb_ref, yp_ref, dt_ref, z_ref, dtb_ref, alog_ref, nw_ref, tri_ref,
                  h_ref, reset, store):
    h_ref[...] = jnp.where(reset, 0.0, h_ref[...])
    for j in reversed(range(bc_ref.shape[1] // CHUNK)):
        rows = pl.ds(j * CHUNK, CHUNK)
        bc = bc_ref[0, rows, :]
        _, da = _dt_and_da(dt_ref[0, rows, :], dtb_ref, alog_ref)
        y = yield from _ssd_chunk("bwd", da, tri_ref, bc[:, SSD_GROUPS * D_STATE:],
                                  bc[:, :SSD_GROUPS * D_STATE],
                                  lambda: xdtb_ref[0, rows, :].astype(F32), h_ref)
        y = (y + yp_ref[0, rows, :]) * _silu(z_ref[0, rows, :].astype(F32))
        ms = jnp.mean(y * y, axis=-1, keepdims=True)
        store(j * CHUNK, (y * lax.rsqrt(ms + EPS) * nw_ref[...]).astype(BF16))
        yield


RING = 3


def _head_kernel(n_tiles, tiles_per_seq,
                 x_ref, mod_ref, n1_ref, n2_ref, wg_ref, wu_ref, wd_ref, win_ref,
                 cw_ref, cbias_ref, dtb_ref, alog_ref, dskip_ref, tri_ref, pos_ref, inv_ref,
                 *rest):
    n_cast = (len(rest) - 14) // 2
    cast_src, rest = rest[:n_cast], rest[n_cast:]
    h_ref, z_ref, q_ref, kv_ref, dt_ref, bc_ref, xdtb_ref, yp_ref, cs_ref = rest[:9]
    cast_dst = rest[9:9 + n_cast]
    xstage, dtstage, xring, dtring, hstate_ref = rest[9 + n_cast:]
    k = pl.program_id(0)

    for src, dst in zip(cast_src, cast_dst):
        dst[...] = src[...].astype(BF16)
    _rope_tile(pos_ref, inv_ref, cs_ref)

    @pl.when(k == 0)
    def _():
        xstage[...] = jnp.zeros_like(xstage)
        dtstage[...] = jnp.zeros_like(dtstage)
        xring[...] = jnp.zeros_like(xring)
        dtring[...] = jnp.zeros_like(dtring)
        hstate_ref[...] = jnp.zeros_like(hstate_ref)

    t = jnp.maximum(k - 2, 0) % tiles_per_seq
    s_new = (k + RING - 1) % RING
    s_main = (k + RING - 2) % RING
    s_prev = k % RING
    has_prev = t > 0
    has_next = t < tiles_per_seq - 1

    def advance_ring():
        xring[s_new] = xstage[...]
        dtring[s_new] = dtstage[...]
        hstate_ref[...] = jnp.where(has_prev, hstate_ref[...], 0.0)

    def window(j):
        lo = j * CHUNK - CONV_HALO
        hi = (j + 1) * CHUNK + CONV_HALO
        parts = []
        if lo < 0:
            halo = xring[s_prev, SEQ_ROWS + lo:, :]
            parts.append(jnp.where(has_prev, halo, jnp.zeros_like(halo)))
        parts.append(xring[s_main, max(lo, 0):min(hi, SEQ_ROWS), :])
        if hi > SEQ_ROWS:
            halo = xring[s_new, 0:hi - SEQ_ROWS, :]
            parts.append(jnp.where(has_next, halo, jnp.zeros_like(halo)))
        return parts[0] if len(parts) == 1 else jnp.concatenate(parts, axis=0)

    advance_ring()
    scan = _ssd_fwd_tile(window, lambda j: dtring[s_main, j * CHUNK:(j + 1) * CHUNK, :],
                         cw_ref, cbias_ref, dtb_ref, alog_ref, dskip_ref, tri_ref,
                         bc_ref, xdtb_ref, yp_ref, hstate_ref)

    x = x_ref[0]
    mod = mod_ref[0]
    u = _ada_norm(x, n1_ref[...] * (1.0 + mod[1:2]), mod[0:1]).astype(BF16)
    ff = _swiglu(u, wg_ref, wu_ref, wd_ref, side_work=(scan,), drain=False)
    h = x + (0.5 * (1.0 + mod[2:3])) * ff
    h_ref[0] = h
    u2 = _ada_norm(h, n2_ref[...] * (1.0 + mod[4:5]), mod[3:4]).astype(BF16)
    dests = {"z": (z_ref,), "xbc": (xstage,), "q": (q_ref,), "kv": (kv_ref,), "dt": (dt_ref, dtstage)}
    c0 = 0
    for name, w, _ in _INPROJ_LAYOUT:
        piece = _dot(u2, win_ref[:, c0:c0 + w])
        for ref in dests[name]:
            if len(ref.shape) == 3:
                ref[0] = piece.astype(ref.dtype)
            else:
                ref[...] = piece.astype(ref.dtype)
        c0 += w
        next(scan, None)
    for _ in scan:
        pass


def _row_block(rows, max_blocks):
    tile = 2 * SUBLANES
    return next(r for r in range(tile, rows + 1, tile) if rows % r == 0 and rows // r <= max_blocks)


def _head(x, mod, norm1, norm2, wg, wu, wd, w_in_p, conv_w8, conv_b, dtb, alog, dskip, positions, to_bf16):
    b, s, d = x.shape
    d_ff = wg.shape[1]
    tps = s // SEQ_ROWS
    n_tiles = b * tps
    tri_lo, _ = _ssd_constants()
    ffn_tile = lambda k: jnp.minimum(k, n_tiles - 1)
    scan_tile = lambda k: jnp.maximum(k - 2, 0)

    def ffn_tok(w):
        return pl.BlockSpec((1, SEQ_ROWS, w), lambda k: (ffn_tile(k) // tps, ffn_tile(k) % tps, 0))

    def scan_tok(w):
        return pl.BlockSpec((1, SEQ_ROWS, w), lambda k: (scan_tile(k) // tps, scan_tile(k) % tps, 0))

    widths = {name: (w, dt) for name, w, dt in _INPROJ_LAYOUT}
    hbm_outs = ("z", "q", "kv", "dt")
    out_shape = [jax.ShapeDtypeStruct((b, s, d), F32)]
    out_shape += [jax.ShapeDtypeStruct((b, s, widths[n][0]), widths[n][1]) for n in hbm_outs]
    out_shape += [jax.ShapeDtypeStruct((b, s, BC_WIDTH), BF16), jax.ShapeDtypeStruct((b, s, SSD_WIDTH), BF16),
                  jax.ShapeDtypeStruct((b, s, SSD_WIDTH), F32), jax.ShapeDtypeStruct((b, ROPE_DIMS, s), F32)]
    out_specs = [ffn_tok(d)] + [ffn_tok(widths[n][0]) for n in hbm_outs]
    out_specs += [scan_tok(BC_WIDTH), scan_tok(SSD_WIDTH), scan_tok(SSD_WIDTH)]

    def ffn_lanes(rows):
        return pl.BlockSpec((1, rows, SEQ_ROWS), lambda k: (ffn_tile(k) // tps, 0, ffn_tile(k) % tps))

    out_specs.append(ffn_lanes(ROPE_DIMS))
    inv = _rope_inv_freq()
    cast_specs = []
    for w in to_bf16:
        rb = _row_block(w.shape[0], n_tiles)
        cast_specs.append(pl.BlockSpec((rb, w.shape[1]),
                                       lambda k, last=w.shape[0] // rb - 1: (jnp.minimum(k, last), 0)))
    out_shape += [jax.ShapeDtypeStruct(w.shape, BF16) for w in to_bf16]
    out_specs += cast_specs
    return pl.pallas_call(
        functools.partial(_head_kernel, n_tiles, tps),
        grid=(n_tiles + 2,),
        in_specs=[ffn_tok(d),
                  pl.BlockSpec((1, N_MOD, d), lambda k: (ffn_tile(k) // tps, 0, 0)),
                  _const_spec((1, d)), _const_spec((1, d)),
                  _const_spec((d, d_ff)), _const_spec((d, d_ff)), _const_spec((d_ff, d)),
                  _const_spec(w_in_p.shape),
                  _const_spec(conv_w8.shape), _const_spec((1, CONV_DIM)),
                  _const_spec((1, LANES)), _const_spec((1, LANES)), _const_spec((1, SSD_WIDTH)),
                  _const_spec(tri_lo.shape), ffn_lanes(1), _const_spec(inv.shape)] + cast_specs,
        out_specs=out_specs,
        out_shape=out_shape,
        scratch_shapes=[pltpu.VMEM((SEQ_ROWS, CONV_DIM), BF16), pltpu.VMEM((SEQ_ROWS, LANES), F32),
                        pltpu.VMEM((RING, SEQ_ROWS, CONV_DIM), BF16), pltpu.VMEM((RING, SEQ_ROWS, LANES), F32),
                        pltpu.VMEM((D_STATE, SSD_WIDTH), F32)],
        compiler_params=pltpu.CompilerParams(dimension_semantics=("arbitrary",),
                                             vmem_limit_bytes=VMEM_LIMIT_BYTES),
        name="ffn1_inproj_scan",
    )(x, mod, norm1, norm2, wg, wu, wd, w_in_p, conv_w8, conv_b, dtb, alog, dskip, tri_lo,
      positions.reshape(b, 1, s), inv, *to_bf16)


def _head_prep_t(t, w_b, cos, sin):
    ms = jnp.sum(t * t, axis=0, keepdims=True) * (1.0 / HEAD_DIM)
    tn = t * lax.rsqrt(ms + EPS) * w_b
    half = ROPE_DIMS // 2
    t1 = tn[0:half]
    t2 = tn[half:ROPE_DIMS]
    return jnp.concatenate([t1 * cos - t2 * sin, t2 * cos + t1 * sin, tn[ROPE_DIMS:]], axis=0)


def _attn_tile(q_ref, kvp_ref, kvc_ref, kvn_ref, csp_ref, csc_ref, csn_ref, qw_ref, kw_ref, sink_ref,
               has_prev, has_next, store):
    tq = q_ref.shape[1]
    nsub = tq // BLOCK
    nw = tq + 2 * BLOCK
    kvw = ATTN_KV_HEADS * HEAD_DIM
    half = ROPE_DIMS // 2
    q_per_kv = ATTN_HEADS // ATTN_KV_HEADS

    kv = jnp.concatenate([kvp_ref[0], kvc_ref[0], kvn_ref[0]], axis=0).astype(F32)
    cs = jnp.concatenate([csp_ref[0], csc_ref[0], csn_ref[0]], axis=1)
    k_t = kv[:, :kvw].T
    v_t = kv[:, kvw:].T.astype(BF16)
    kw_b = jnp.tile(kw_ref[...], (1, nw // LANES))
    k_prep = jnp.concatenate(
        [_head_prep_t(k_t[g * HEAD_DIM:(g + 1) * HEAD_DIM], kw_b, cs[0:half], cs[half:])
         for g in range(ATTN_KV_HEADS)], axis=0)
    keys = k_prep.T.astype(BF16)

    q_t = q_ref[0].astype(F32).T
    qw_b = jnp.tile(qw_ref[...] * (HEAD_DIM ** -0.5), (1, tq // LANES))
    cs_q = csc_ref[0]
    zeros = jnp.zeros((HEAD_DIM, tq), BF16)
    q_heads = []
    for h in range(ATTN_HEADS):
        qh = _head_prep_t(q_t[h * HEAD_DIM:(h + 1) * HEAD_DIM], qw_b, cs_q[0:half], cs_q[half:]).astype(BF16)
        q_heads.append(jnp.concatenate([qh, zeros] if h // q_per_kv == 0 else [zeros, qh], axis=0))

    yield
    kr = _iota2((BLOCK, BLOCK), 0)
    qc = _iota2((BLOCK, BLOCK), 1)
    bias_prev = jnp.where(kr >= qc, 0.0, NEG_BIG)
    bias_next = jnp.where(kr <= qc, 0.0, NEG_BIG)
    sink_row = jnp.concatenate([jnp.full((1, BLOCK), sink_ref[h], F32) for h in range(ATTN_HEADS)], axis=1)
    gq = q_per_kv * BLOCK
    for j in range(nsub):
        bp = bias_prev if j > 0 else jnp.where(has_prev, bias_prev, NEG_BIG)
        bn = bias_next if j < nsub - 1 else jnp.where(has_next, bias_next, NEG_BIG)
        lo, hi = j * BLOCK, (j + 3) * BLOCK
        q_all = jnp.concatenate([qh[:, lo:lo + BLOCK] for qh in q_heads], axis=1)
        s = _dot(keys[lo:hi], q_all)
        yield
        s0 = s[0:BLOCK] + jnp.tile(bp, (1, ATTN_HEADS))
        s1 = s[BLOCK:2 * BLOCK]
        s2 = s[2 * BLOCK:] + jnp.tile(bn, (1, ATTN_HEADS))
        m = jnp.maximum(jnp.max(jnp.maximum(jnp.maximum(s0, s1), s2), axis=0, keepdims=True), sink_row)
        p0 = jnp.exp(s0 - m)
        p1 = jnp.exp(s1 - m)
        p2 = jnp.exp(s2 - m)
        denom = jnp.sum(p0 + p1 + p2, axis=0, keepdims=True) + jnp.exp(sink_row - m)
        p = jnp.concatenate([p0, p1, p2], axis=0).astype(BF16)
        inv = 1.0 / denom
        outs = []
        for g in range(ATTN_KV_HEADS):
            o = _dot(v_t[g * HEAD_DIM:(g + 1) * HEAD_DIM, lo:hi], p[:, g * gq:(g + 1) * gq])
            o = o * inv[:, g * gq:(g + 1) * gq]
            outs += [o[:, k * BLOCK:(k + 1) * BLOCK] for k in range(q_per_kv)]
        store(lo, jnp.concatenate(outs, axis=0).T.astype(BF16))
        yield


def _tail_kernel(n_tiles, tiles_per_seq,
                 h_ref, mod_ref, n3_ref, wo_ref, wg_ref, wu_ref, wd_ref,
                 q_ref, kvp_ref, kvc_ref, kvn_ref, csp_ref, csc_ref, csn_ref, qw_ref, kw_ref, sink_ref,
                 bc_ref, xdtb_ref, yp_ref, dt_ref, z_ref, dtb_ref, alog_ref, nw_ref, tri_ref,
                 o_ref, y_scr, hstate_ref):
    k = pl.program_id(0)

    @pl.when(k == 0)
    def _():
        y_scr[...] = jnp.zeros_like(y_scr)
        hstate_ref[...] = jnp.zeros_like(hstate_ref)

    rd = k % 2
    wr = 1 - rd

    t = jnp.maximum(n_tiles - 1 - k, 0) % tiles_per_seq
    is_first = t == 0
    is_last = t == tiles_per_seq - 1

    def store_ssd(row0, y):
        y_scr[wr, row0:row0 + CHUNK, 0:SSD_WIDTH] = y

    def store_attn(row0, y):
        y_scr[wr, row0:row0 + BLOCK, SSD_WIDTH:] = y

    attn = _attn_tile(q_ref, kvp_ref, kvc_ref, kvn_ref, csp_ref, csc_ref, csn_ref, qw_ref, kw_ref, sink_ref,
                      jnp.logical_not(is_first), jnp.logical_not(is_last), store_attn)
    scan = _ssd_bwd_tile(bc_ref, xdtb_ref, yp_ref, dt_ref, z_ref, dtb_ref, alog_ref, nw_ref, tri_ref,
                         hstate_ref, is_last, store_ssd)

    h1 = h_ref[0]
    mod = mod_ref[0]
    mix = _dot(y_scr[rd], wo_ref[...])
    next(scan, None)
    h2 = h1 + (1.0 + mod[5:6]) * mix
    u = _ada_norm(h2, n3_ref[...] * (1.0 + mod[7:8]), mod[6:7]).astype(BF16)
    ff = _swiglu(u, wg_ref, wu_ref, wd_ref, side_work=(attn, scan))
    o_ref[0] = h2 + (0.5 * (1.0 + mod[8:9])) * ff


def _tail(h1, mod, norm3, w_out, wg, wu, wd, q, kv, cs_tab, qw_b, kw_b, sink,
          bc, xdtb, ypart, dt, z, dtb, alog, norm_w):
    b, s, d = h1.shape
    d_ff = wg.shape[1]
    tps = s // SEQ_ROWS
    n_tiles = b * tps
    nsub = SEQ_ROWS // BLOCK
    nb = s // BLOCK
    aw = q.shape[-1]
    kvw2 = kv.shape[-1]
    cs_rows = cs_tab.shape[1]
    _, tri_up = _ssd_constants()

    ffn_tile = lambda k: jnp.clip(n_tiles - k, 0, n_tiles - 1)
    mix_tile = lambda k: jnp.maximum(n_tiles - 1 - k, 0)
    prev_blk = lambda t: jnp.maximum(t * nsub - 1, 0)
    next_blk = lambda t: jnp.minimum((t + 1) * nsub, nb - 1)

    def ffn_tok(w):
        return pl.BlockSpec((1, SEQ_ROWS, w), lambda k: (ffn_tile(k) // tps, ffn_tile(k) % tps, 0))

    def mix_tok(w):
        return pl.BlockSpec((1, SEQ_ROWS, w), lambda k: (mix_tile(k) // tps, mix_tile(k) % tps, 0))

    in_specs = [
        ffn_tok(d),
        pl.BlockSpec((1, N_MOD, d), lambda k: (ffn_tile(k) // tps, 0, 0)),
        _const_spec((1, d)), _const_spec(w_out.shape),
        _const_spec((d, d_ff)), _const_spec((d, d_ff)), _const_spec((d_ff, d)),
        mix_tok(aw),
        pl.BlockSpec((1, BLOCK, kvw2), lambda k: (mix_tile(k) // tps, prev_blk(mix_tile(k) % tps), 0)),
        mix_tok(kvw2),
        pl.BlockSpec((1, BLOCK, kvw2), lambda k: (mix_tile(k) // tps, next_blk(mix_tile(k) % tps), 0)),
        pl.BlockSpec((1, cs_rows, BLOCK), lambda k: (mix_tile(k) // tps, 0, prev_blk(mix_tile(k) % tps))),
        pl.BlockSpec((1, cs_rows, SEQ_ROWS), lambda k: (mix_tile(k) // tps, 0, mix_tile(k) % tps)),
        pl.BlockSpec((1, cs_rows, BLOCK), lambda k: (mix_tile(k) // tps, 0, next_blk(mix_tile(k) % tps))),
        _const_spec((HEAD_DIM, LANES)), _const_spec((HEAD_DIM, LANES)),
        pl.BlockSpec(memory_space=pltpu.SMEM),
        mix_tok(BC_WIDTH), mix_tok(SSD_WIDTH), mix_tok(SSD_WIDTH), mix_tok(LANES), mix_tok(SSD_WIDTH),
        _const_spec((1, LANES)), _const_spec((1, LANES)), _const_spec((1, SSD_WIDTH)),
        _const_spec(tri_up.shape),
    ]
    return pl.pallas_call(
        functools.partial(_tail_kernel, n_tiles, tps),
        grid=(n_tiles + 1,),
        in_specs=in_specs,
        out_specs=ffn_tok(d),
        out_shape=jax.ShapeDtypeStruct((b, s, d), F32),
        scratch_shapes=[pltpu.VMEM((2, SEQ_ROWS, SSD_WIDTH + aw), BF16),
                        pltpu.VMEM((D_STATE, SSD_WIDTH), F32)],
        compiler_params=pltpu.CompilerParams(dimension_semantics=("arbitrary",),
                                             vmem_limit_bytes=VMEM_LIMIT_BYTES),
        name="mixers_outproj_ffn2",
    )(h1, mod, norm3, w_out, wg, wu, wd, q, kv, kv, kv, cs_tab, cs_tab, cs_tab, qw_b, kw_b, sink,
      bc, xdtb, ypart, dt, z, dtb, alog, norm_w, tri_up)


def _pad_inproj(w_in):
    n_dt = 2 * SSD_HEADS
    s_xbc = SSD_WIDTH + CONV_DIM
    s_dt = s_xbc + n_dt
    dt_cols = jnp.pad(w_in[:, s_xbc:s_dt], ((0, 0), (0, LANES - n_dt)))
    return jnp.concatenate([w_in[:, :s_xbc], w_in[:, s_dt:], dt_cols], axis=1).astype(BF16)


def _pad_lanes(v, width=LANES):
    v = v.reshape(1, -1)
    return jnp.pad(v, ((0, 0), (0, width - v.shape[1])))


def _lane_bcast(v):
    return jnp.broadcast_to(v[:, None], (v.shape[0], LANES))


def kernel(x, c, positions, w_ada, b_ada, norm_ffn1, ffn1_wg, ffn1_wu, ffn1_wd, norm_mix, w_in, conv_w,
           conv_b, dt_bias, a_log, d_skip, ssd_norm_w, q_norm_w, k_norm_w, sink_logit, w_out, norm_ffn2,
           ffn2_wg, ffn2_wu, ffn2_wd):
    depth = w_ada.shape[0]
    b, s, d = x.shape
    h = x.astype(F32)
    c_pad = jnp.pad(c.astype(F32), ((0, -b % SUBLANES), (0, 0)))
    for l in range(depth):
        mod = _adaln_mod(c_pad, w_ada[l], b_ada[l])[:b].reshape(b, N_MOD, d)
        conv_w8 = jnp.pad(conv_w[l], ((0, SUBLANES - CONV_K), (0, 0)))
        dtb, alog = _pad_lanes(dt_bias[l]), _pad_lanes(a_log[l])
        h1, z, q, kv, dt, bc, xdtb, ypart, cs_tab, wo_b, wg2_b, wu2_b, wd2_b = _head(
            h, mod, norm_ffn1[l].reshape(1, d), norm_mix[l].reshape(1, d),
            ffn1_wg[l].astype(BF16), ffn1_wu[l].astype(BF16), ffn1_wd[l].astype(BF16), _pad_inproj(w_in[l]),
            conv_w8, conv_b[l].reshape(1, -1), dtb, alog, jnp.repeat(d_skip[l], SSD_HEAD_DIM).reshape(1, -1),
            positions, to_bf16=(w_out[l], ffn2_wg[l], ffn2_wu[l], ffn2_wd[l]))
        h = _tail(h1, mod, norm_ffn2[l].reshape(1, d), wo_b, wg2_b, wu2_b, wd2_b,
                  q, kv, cs_tab, _lane_bcast(q_norm_w[l]), _lane_bcast(k_norm_w[l]), sink_logit[l],
                  bc, xdtb, ypart, dt, z, dtb, alog, ssd_norm_w[l].reshape(1, -1))
    return h.astype(x.dtype)
```

```python
import functools

import jax
import jax.numpy as jnp
import numpy as np
from jax import lax
from jax.experimental import pallas as pl
from jax.experimental.pallas import tpu as pltpu

F32 = jnp.float32
BF16 = jnp.bfloat16

SSD_HEAD_DIM = 64
SSD_HEADS = 8
SSD_GROUPS = 2
D_STATE = 128
CONV_K = 5
CHUNK = 128
HEAD_DIM = 64
ATTN_HEADS = 8
ATTN_KV_HEADS = 2
WINDOW = 128
BLOCK = 128
ROPE_DIMS = 16
ROPE_THETA = 500000.0
N_MOD = 9
EPS = 1e-6

SSD_WIDTH = SSD_HEADS * SSD_HEAD_DIM
BC_WIDTH = 2 * SSD_GROUPS * D_STATE
CONV_DIM = SSD_WIDTH + BC_WIDTH

LANES = 128
SUBLANES = 8
VMEM_LIMIT_BYTES = 56 * 1024 * 1024

SEQ_ROWS = 512
FFN_CHUNK = 256
CONV_HALO = 16

NEG_BIG = -1e30


def _dot(a, b):
    return jnp.dot(a, b, preferred_element_type=F32)


def _dot_nt(a, b):
    return lax.dot_general(a, b, (((1,), (1,)), ((), ())), preferred_element_type=F32)


def _row_sums(tri_bf16, x, terms=3):
    acc = None
    r = x
    for t in range(terms):
        h = r.astype(BF16)
        d = _dot(tri_bf16, h)
        acc = d if acc is None else acc + d
        if t + 1 < terms:
            r = r - h.astype(F32)
    return acc


def _silu(x):
    return x * jax.nn.sigmoid(x)


def _iota2(shape, dim):
    return lax.broadcasted_iota(jnp.int32, shape, dim)


def _const_spec(shape):
    nd = len(shape)
    return pl.BlockSpec(shape, lambda *_: (0,) * nd, pipeline_mode=pl.Buffered(1))


def _mod_kernel(c_ref, w_ref, b_ref, o_ref):
    cs = _silu(c_ref[...])
    o_ref[...] = _dot(cs, w_ref[...]) + b_ref[...]


def _adaln_mod(c_pad, w_ada, b_ada):
    rows, d = c_pad.shape
    n = w_ada.shape[1]
    bn = 3 * d
    return pl.pallas_call(
        _mod_kernel,
        grid=(n // bn,),
        in_specs=[pl.BlockSpec((rows, d), lambda j: (0, 0)),
                  pl.BlockSpec((d, bn), lambda j: (0, j)),
                  pl.BlockSpec((1, bn), lambda j: (0, j))],
        out_specs=pl.BlockSpec((rows, bn), lambda j: (0, j)),
        out_shape=jax.ShapeDtypeStruct((rows, n), F32),
        name="adaln_mod",
    )(c_pad, w_ada, b_ada.reshape(1, n))


def _rope_tile(pos_ref, inv_ref, o_ref):
    half = ROPE_DIMS // 2
    s = pos_ref.shape[-1]
    p = pos_ref[0].astype(F32)
    ang = jnp.tile(inv_ref[...], (1, s // LANES)) * p
    o_ref[0, 0:half, :] = jnp.cos(ang)
    o_ref[0, half:, :] = jnp.sin(ang)


def _rope_inv_freq():
    half = ROPE_DIMS // 2
    inv = ROPE_THETA ** (-jnp.arange(half, dtype=F32) * 2.0 / ROPE_DIMS)
    return jnp.broadcast_to(inv[:, None], (half, LANES))


def _ada_norm(x, gain_scale, shift):
    ms = jnp.mean(x * x, axis=-1, keepdims=True)
    return x * lax.rsqrt(ms + EPS) * gain_scale + shift


def _swiglu(ub, wg_ref, wu_ref, wd_ref, side_work=(), drain=True):
    d_ff = wg_ref.shape[1]
    acc = None
    for c0 in range(0, d_ff, FFN_CHUNK):
        c1 = min(c0 + FFN_CHUNK, d_ff)
        g = _dot(ub, wg_ref[:, c0:c1])
        up = _dot(ub, wu_ref[:, c0:c1])
        a = (_silu(g) * up).astype(BF16)
        d = _dot(a, wd_ref[c0:c1, :])
        acc = d if acc is None else acc + d
        for gen in side_work:
            next(gen, None)
    for gen in side_work if drain else ():
        for _ in gen:
            pass
    return acc


_INPROJ_LAYOUT = (("z", SSD_WIDTH, F32), ("xbc", CONV_DIM, BF16), ("q", ATTN_HEADS * HEAD_DIM, F32),
                  ("kv", 2 * ATTN_KV_HEADS * HEAD_DIM, F32), ("dt", LANES, F32))


def _ssd_constants():
    r = np.arange(CHUNK)[:, None]
    c = np.arange(CHUNK)[None, :]
    return jnp.asarray(c <= r, dtype=BF16), jnp.asarray(c >= r, dtype=BF16)


def _dt_and_da(dt_raw, dtb_ref, alog_ref):
    lane = _iota2(dt_raw.shape, 1)
    dt = jnp.where(lane < 2 * SSD_HEADS, jax.nn.softplus(dt_raw + dtb_ref[...]), 0.0)
    a = -jnp.exp(alog_ref[...])
    return dt, dt * a


def _head_columns(x, lane0):
    return [jnp.broadcast_to(x[:, lane0 + h:lane0 + h + 1], x.shape) for h in range(SSD_HEADS)]


def _expand_heads(cols):
    first = _iota2(cols[0].shape, 1) < SSD_HEAD_DIM
    return jnp.concatenate([jnp.where(first, cols[2 * p], cols[2 * p + 1]) for p in range(SSD_HEADS // 2)],
                           axis=1)


def _ssd_chunk(direction, da, tri_ref, cm, bm, xdt_fn, h_ref):
    gw = SSD_WIDTH // SSD_GROUPS
    lane0 = 0 if direction == "fwd" else SSD_HEADS
    tot_row = CHUNK - 1 if direction == "fwd" else 0
    cgs = [cm[:, g * D_STATE:(g + 1) * D_STATE] for g in range(SSD_GROUPS)]
    bgs = [bm[:, g * D_STATE:(g + 1) * D_STATE] for g in range(SSD_GROUPS)]
    hgs = [h_ref[:, g * gw:(g + 1) * gw] for g in range(SSD_GROUPS)]

    cs = _row_sums(tri_ref[...], da)
    cbs = [_dot_nt(cgs[g], bgs[g]) for g in range(SSD_GROUPS)]
    chs = [_dot(cgs[g], hgs[g].astype(BF16)) for g in range(SSD_GROUPS)]
    yield

    xdt = xdt_fn()
    cs_cols = _head_columns(cs, lane0)
    csx = _expand_heads(cs_cols)
    tot = csx[tot_row:tot_row + 1, :]
    e_in = jnp.exp(csx)
    xdec = (xdt * jnp.exp(tot - csx)).astype(BF16)
    cdec = jnp.exp(tot)
    cst = cs.T
    row = _iota2((CHUNK, CHUNK), 0)
    col = _iota2((CHUNK, CHUNK), 1)
    keep = (col <= row) if direction == "fwd" else (col > row)
    half = _iota2((CHUNK, LANES), 1) // SSD_HEAD_DIM
    xdt_b = xdt.astype(BF16)
    ys = []
    for g in range(SSD_GROUPS):
        pairs = []
        for pp in range(gw // LANES):
            p = g * (gw // LANES) + pp
            xp = xdt_b[:, p * LANES:(p + 1) * LANES]
            yp = None
            for e in range(2):
                hcol = lane0 + 2 * p + e
                seg = cs_cols[2 * p + e] - cst[hcol:hcol + 1, :]
                m = (cbs[g] * jnp.exp(jnp.where(keep, seg, NEG_BIG))).astype(BF16)
                d = _dot(m, jnp.where(half == e, xp, jnp.zeros_like(xp)))
                yp = d if yp is None else yp + d
            pairs.append(yp)
        ys.append(jnp.concatenate(pairs, axis=1) + chs[g] * e_in[:, g * gw:(g + 1) * gw])
        bgt = bgs[g].astype(F32).T.astype(BF16)
        st = _dot(bgt, xdec[:, g * gw:(g + 1) * gw])
        h_ref[:, g * gw:(g + 1) * gw] = hgs[g] * cdec[:, g * gw:(g + 1) * gw] + st
    return jnp.concatenate(ys, axis=1)


def _ssd_fwd_tile(window_fn, dt_fn, cw_ref, cbias_ref, dtb_ref, alog_ref, dskip_ref,
                  tri_ref, bc_ref, xdtb_ref, yp_ref, h_ref):
    mid = CONV_K // 2
    for j in range(SEQ_ROWS // CHUNK):
        rows = pl.ds(j * CHUNK, CHUNK)
        win = window_fn(j)
        dt, da = _dt_and_da(dt_fn(j), dtb_ref, alog_ref)
        dtx_f = _expand_heads(_head_columns(dt, 0))
        dtx_b = _expand_heads(_head_columns(dt, SSD_HEADS))
        yield
        winf = win.astype(F32)
        acc = cbias_ref[...] + winf[CONV_HALO:CONV_HALO + CHUNK] * cw_ref[mid:mid + 1, :]
        for k in [k for k in range(CONV_K) if k != mid]:
            rolled = pltpu.roll(winf, (mid - k) % winf.shape[0], axis=0)
            acc = acc + rolled[CONV_HALO:CONV_HALO + CHUNK] * cw_ref[k:k + 1, :]
        act = _silu(acc)
        xs = act[:, :SSD_WIDTH]
        bc = act[:, SSD_WIDTH:].astype(BF16)
        xdtb_ref[0, rows, :] = (xs * dtx_b).astype(BF16)
        bc_ref[0, rows, :] = bc
        y = yield from _ssd_chunk("fwd", da, tri_ref, bc[:, SSD_GROUPS * D_STATE:],
                                  bc[:, :SSD_GROUPS * D_STATE], lambda: xs * dtx_f, h_ref)
        yp_ref[0, rows, :] = y + xs * dskip_ref[...]
        yield


def _ssd_bwd_tile(bc_ref, xdtb_ref, yp_ref, dt_ref, z_ref, dtb_ref, alog_ref, nw_ref, tri_ref,
                  h_ref, reset, store):
    h_ref[...] = jnp.where(reset, 0.0, h_ref[...])
    for j in reversed(range(bc_ref.shape[1] // CHUNK)):
        rows = pl.ds(j * CHUNK, CHUNK)
        bc = bc_ref[0, rows, :]
        _, da = _dt_and_da(dt_ref[0, rows, :], dtb_ref, alog_ref)
        y = yield from _ssd_chunk("bwd", da, tri_ref, bc[:, SSD_GROUPS * D_STATE:],
                                  bc[:, :SSD_GROUPS * D_STATE],
                                  lambda: xdtb_ref[0, rows, :].astype(F32), h_ref)
        y = (y + yp_ref[0, rows, :]) * _silu(z_ref[0, rows, :])
        ms = jnp.mean(y * y, axis=-1, keepdims=True)
        store(j * CHUNK, (y * lax.rsqrt(ms + EPS) * nw_ref[...]).astype(BF16))
        yield


RING = 3


def _head_kernel(n_tiles, tiles_per_seq,
                 x_ref, mod_ref, n1_ref, n2_ref, wg_ref, wu_ref, wd_ref, win_ref,
                 cw_ref, cbias_ref, dtb_ref, alog_ref, dskip_ref, tri_ref, pos_ref, inv_ref,
                 *rest):
    n_cast = (len(rest) - 14) // 2
    cast_src, rest = rest[:n_cast], rest[n_cast:]
    h_ref, z_ref, q_ref, kv_ref, dt_ref, bc_ref, xdtb_ref, yp_ref, cs_ref = rest[:9]
    cast_dst = rest[9:9 + n_cast]
    xstage, dtstage, xring, dtring, hstate_ref = rest[9 + n_cast:]
    k = pl.program_id(0)

    @pl.when(k == 0)
    def _():
        xstage[...] = jnp.zeros_like(xstage)
        dtstage[...] = jnp.zeros_like(dtstage)
        xring[...] = jnp.zeros_like(xring)
        dtring[...] = jnp.zeros_like(dtring)
        hstate_ref[...] = jnp.zeros_like(hstate_ref)

    t = jnp.maximum(k - 2, 0) % tiles_per_seq
    s_new = (k + RING - 1) % RING
    s_main = (k + RING - 2) % RING
    s_prev = k % RING
    has_prev = t > 0
    has_next = t < tiles_per_seq - 1

    def advance_ring():
        xring[s_new] = xstage[...]
        dtring[s_new] = dtstage[...]
        hstate_ref[...] = jnp.where(has_prev, hstate_ref[...], 0.0)

    def window(j):
        lo = j * CHUNK - CONV_HALO
        hi = (j + 1) * CHUNK + CONV_HALO
        parts = []
        if lo < 0:
            halo = xring[s_prev, SEQ_ROWS + lo:, :]
            parts.append(jnp.where(has_prev, halo, jnp.zeros_like(halo)))
        parts.append(xring[s_main, max(lo, 0):min(hi, SEQ_ROWS), :])
        if hi > SEQ_ROWS:
            halo = xring[s_new, 0:hi - SEQ_ROWS, :]
            parts.append(jnp.where(has_next, halo, jnp.zeros_like(halo)))
        return parts[0] if len(parts) == 1 else jnp.concatenate(parts, axis=0)

    advance_ring()
    scan = _ssd_fwd_tile(window, lambda j: dtring[s_main, j * CHUNK:(j + 1) * CHUNK, :],
                         cw_ref, cbias_ref, dtb_ref, alog_ref, dskip_ref, tri_ref,
                         bc_ref, xdtb_ref, yp_ref, hstate_ref)

    x = x_ref[0]
    mod = mod_ref[0]
    u = _ada_norm(x, n1_ref[...] * (1.0 + mod[1:2]), mod[0:1]).astype(BF16)
    ff = _swiglu(u, wg_ref, wu_ref, wd_ref, side_work=(scan,), drain=False)
    h = x + (0.5 * (1.0 + mod[2:3])) * ff
    h_ref[0] = h

    for src, dst in zip(cast_src, cast_dst):
        dst[...] = src[...].astype(BF16)
    _rope_tile(pos_ref, inv_ref, cs_ref)

    u2 = _ada_norm(h, n2_ref[...] * (1.0 + mod[4:5]), mod[3:4]).astype(BF16)
    dests = {"z": (z_ref,), "xbc": (xstage,), "q": (q_ref,), "kv": (kv_ref,), "dt": (dt_ref, dtstage)}
    c0 = 0
    for name, w, _ in _INPROJ_LAYOUT:
        piece = _dot(u2, win_ref[:, c0:c0 + w])
        for ref in dests[name]:
            if len(ref.shape) == 3:
                ref[0] = piece.astype(ref.dtype)
            else:
                ref[...] = piece.astype(ref.dtype)
        c0 += w
        next(scan, None)
    for _ in scan:
        pass


def _row_block(rows, max_blocks):
    tile = 2 * SUBLANES
    return next(r for r in range(tile, rows + 1, tile) if rows % r == 0 and rows // r <= max_blocks)


def _head(x, mod, norm1, norm2, wg, wu, wd, w_in_p, conv_w8, conv_b, dtb, alog, dskip, positions, to_bf16):
    b, s, d = x.shape
    d_ff = wg.shape[1]
    tps = s // SEQ_ROWS
    n_tiles = b * tps
    tri_lo, _ = _ssd_constants()
    ffn_tile = lambda k: jnp.minimum(k, n_tiles - 1)
    scan_tile = lambda k: jnp.maximum(k - 2, 0)

    def ffn_tok(w):
        return pl.BlockSpec((1, SEQ_ROWS, w), lambda k: (ffn_tile(k) // tps, ffn_tile(k) % tps, 0))

    def scan_tok(w):
        return pl.BlockSpec((1, SEQ_ROWS, w), lambda k: (scan_tile(k) // tps, scan_tile(k) % tps, 0))

    widths = {name: (w, dt) for name, w, dt in _INPROJ_LAYOUT}
    hbm_outs = ("z", "q", "kv", "dt")
    out_shape = [jax.ShapeDtypeStruct((b, s, d), F32)]
    out_shape += [jax.ShapeDtypeStruct((b, s, widths[n][0]), widths[n][1]) for n in hbm_outs]
    out_shape += [jax.ShapeDtypeStruct((b, s, BC_WIDTH), BF16), jax.ShapeDtypeStruct((b, s, SSD_WIDTH), BF16),
                  jax.ShapeDtypeStruct((b, s, SSD_WIDTH), F32), jax.ShapeDtypeStruct((b, ROPE_DIMS, s), F32)]
    out_specs = [ffn_tok(d)] + [ffn_tok(widths[n][0]) for n in hbm_outs]
    out_specs += [scan_tok(BC_WIDTH), scan_tok(SSD_WIDTH), scan_tok(SSD_WIDTH)]

    def ffn_lanes(rows):
        return pl.BlockSpec((1, rows, SEQ_ROWS), lambda k: (ffn_tile(k) // tps, 0, ffn_tile(k) % tps))

    out_specs.append(ffn_lanes(ROPE_DIMS))
    inv = _rope_inv_freq()
    cast_specs = []
    for w in to_bf16:
        rb = _row_block(w.shape[0], n_tiles)
        cast_specs.append(pl.BlockSpec((rb, w.shape[1]),
                                       lambda k, last=w.shape[0] // rb - 1: (jnp.minimum(k, last), 0)))
    out_shape += [jax.ShapeDtypeStruct(w.shape, BF16) for w in to_bf16]
    out_specs += cast_specs
    return pl.pallas_call(
        functools.partial(_head_kernel, n_tiles, tps),
        grid=(n_tiles + 2,),
        in_specs=[ffn_tok(d),
                  pl.BlockSpec((1, N_MOD, d), lambda k: (ffn_tile(k) // tps, 0, 0)),
                  _const_spec((1, d)), _const_spec((1, d)),
                  _const_spec((d, d_ff)), _const_spec((d, d_ff)), _const_spec((d_ff, d)),
                  _const_spec(w_in_p.shape),
                  _const_spec(conv_w8.shape), _const_spec((1, CONV_DIM)),
                  _const_spec((1, LANES)), _const_spec((1, LANES)), _const_spec((1, SSD_WIDTH)),
                  _const_spec(tri_lo.shape), ffn_lanes(1), _const_spec(inv.shape)] + cast_specs,
        out_specs=out_specs,
        out_shape=out_shape,
        scratch_shapes=[pltpu.VMEM((SEQ_ROWS, CONV_DIM), BF16), pltpu.VMEM((SEQ_ROWS, LANES), F32),
                        pltpu.VMEM((RING, SEQ_ROWS, CONV_DIM), BF16), pltpu.VMEM((RING, SEQ_ROWS, LANES), F32),
                        pltpu.VMEM((D_STATE, SSD_WIDTH), F32)],
        compiler_params=pltpu.CompilerParams(dimension_semantics=("arbitrary",),
                                             vmem_limit_bytes=VMEM_LIMIT_BYTES),
        name="ffn1_inproj_scan",
    )(x, mod, norm1, norm2, wg, wu, wd, w_in_p, conv_w8, conv_b, dtb, alog, dskip, tri_lo,
      positions.reshape(b, 1, s), inv, *to_bf16)


def _head_prep_t(t, w_b, cos, sin):
    ms = jnp.sum(t * t, axis=0, keepdims=True) * (1.0 / HEAD_DIM)
    tn = t * lax.rsqrt(ms + EPS) * w_b
    half = ROPE_DIMS // 2
    t1 = tn[0:half]
    t2 = tn[half:ROPE_DIMS]
    return jnp.concatenate([t1 * cos - t2 * sin, t2 * cos + t1 * sin, tn[ROPE_DIMS:]], axis=0)


def _attn_tile(q_ref, kvp_ref, kvc_ref, kvn_ref, csp_ref, csc_ref, csn_ref, qw_ref, kw_ref, sink_ref,
               has_prev, has_next, store):
    tq = q_ref.shape[1]
    nsub = tq // BLOCK
    nw = tq + 2 * BLOCK
    kvw = ATTN_KV_HEADS * HEAD_DIM
    half = ROPE_DIMS // 2
    q_per_kv = ATTN_HEADS // ATTN_KV_HEADS

    kv = jnp.concatenate([kvp_ref[0], kvc_ref[0], kvn_ref[0]], axis=0)
    cs = jnp.concatenate([csp_ref[0], csc_ref[0], csn_ref[0]], axis=1)
    k_t = kv[:, :kvw].T
    v_t = kv[:, kvw:].T.astype(BF16)
    kw_b = jnp.tile(kw_ref[...], (1, nw // LANES))
    k_prep = jnp.concatenate(
        [_head_prep_t(k_t[g * HEAD_DIM:(g + 1) * HEAD_DIM], kw_b, cs[0:half], cs[half:])
         for g in range(ATTN_KV_HEADS)], axis=0)
    keys = k_prep.T.astype(BF16)

    q_t = q_ref[0].T
    qw_b = jnp.tile(qw_ref[...] * (HEAD_DIM ** -0.5), (1, tq // LANES))
    cs_q = csc_ref[0]
    zeros = jnp.zeros((HEAD_DIM, tq), BF16)
    q_heads = []
    for h in range(ATTN_HEADS):
        qh = _head_prep_t(q_t[h * HEAD_DIM:(h + 1) * HEAD_DIM], qw_b, cs_q[0:half], cs_q[half:]).astype(BF16)
        q_heads.append(jnp.concatenate([qh, zeros] if h // q_per_kv == 0 else [zeros, qh], axis=0))

    yield
    kr = _iota2((BLOCK, BLOCK), 0)
    qc = _iota2((BLOCK, BLOCK), 1)
    bias_prev = jnp.where(kr >= qc, 0.0, NEG_BIG)
    bias_next = jnp.where(kr <= qc, 0.0, NEG_BIG)
    sink_row = jnp.concatenate([jnp.full((1, BLOCK), sink_ref[h], F32) for h in range(ATTN_HEADS)], axis=1)
    gq = q_per_kv * BLOCK
    for j in range(nsub):
        bp = bias_prev if j > 0 else jnp.where(has_prev, bias_prev, NEG_BIG)
        bn = bias_next if j < nsub - 1 else jnp.where(has_next, bias_next, NEG_BIG)
        lo, hi = j * BLOCK, (j + 3) * BLOCK
        q_all = jnp.concatenate([qh[:, lo:lo + BLOCK] for qh in q_heads], axis=1)
        s = _dot(keys[lo:hi], q_all)
        yield
        s0 = s[0:BLOCK] + jnp.tile(bp, (1, ATTN_HEADS))
        s1 = s[BLOCK:2 * BLOCK]
        s2 = s[2 * BLOCK:] + jnp.tile(bn, (1, ATTN_HEADS))
        m = jnp.maximum(jnp.max(jnp.maximum(jnp.maximum(s0, s1), s2), axis=0, keepdims=True), sink_row)
        p0 = jnp.exp(s0 - m)
        p1 = jnp.exp(s1 - m)
        p2 = jnp.exp(s2 - m)
        denom = jnp.sum(p0 + p1 + p2, axis=0, keepdims=True) + jnp.exp(sink_row - m)
        p = jnp.concatenate([p0, p1, p2], axis=0).astype(BF16)
        inv = 1.0 / denom
        outs = []
        for g in range(ATTN_KV_HEADS):
            o = _dot(v_t[g * HEAD_DIM:(g + 1) * HEAD_DIM, lo:hi], p[:, g * gq:(g + 1) * gq])
            o = o * inv[:, g * gq:(g + 1) * gq]
            outs += [o[:, k * BLOCK:(k + 1) * BLOCK] for k in range(q_per_kv)]
        store(lo, jnp.concatenate(outs, axis=0).T.astype(BF16))
        yield


def _tail_kernel(n_tiles, tiles_per_seq,
                 h_ref, mod_ref, n3_ref, wo_ref, wg_ref, wu_ref, wd_ref,
                 q_ref, kvp_ref, kvc_ref, kvn_ref, csp_ref, csc_ref, csn_ref, qw_ref, kw_ref, sink_ref,
                 bc_ref, xdtb_ref, yp_ref, dt_ref, z_ref, dtb_ref, alog_ref, nw_ref, tri_ref,
                 o_ref, y_scr, hstate_ref):
    k = pl.program_id(0)

    @pl.when(k == 0)
    def _():
        y_scr[...] = jnp.zeros_like(y_scr)
        hstate_ref[...] = jnp.zeros_like(hstate_ref)

    rd = k % 2
    wr = 1 - rd

    t = jnp.maximum(n_tiles - 1 - k, 0) % tiles_per_seq
    is_first = t == 0
    is_last = t == tiles_per_seq - 1

    def store_ssd(row0, y):
        y_scr[wr, row0:row0 + CHUNK, 0:SSD_WIDTH] = y

    def store_attn(row0, y):
        y_scr[wr, row0:row0 + BLOCK, SSD_WIDTH:] = y

    attn = _attn_tile(q_ref, kvp_ref, kvc_ref, kvn_ref, csp_ref, csc_ref, csn_ref, qw_ref, kw_ref, sink_ref,
                      jnp.logical_not(is_first), jnp.logical_not(is_last), store_attn)
    scan = _ssd_bwd_tile(bc_ref, xdtb_ref, yp_ref, dt_ref, z_ref, dtb_ref, alog_ref, nw_ref, tri_ref,
                         hstate_ref, is_last, store_ssd)

    h1 = h_ref[0]
    mod = mod_ref[0]
    mix = _dot(y_scr[rd], wo_ref[...])
    next(scan, None)
    h2 = h1 + (1.0 + mod[5:6]) * mix
    u = _ada_norm(h2, n3_ref[...] * (1.0 + mod[7:8]), mod[6:7]).astype(BF16)
    ff = _swiglu(u, wg_ref, wu_ref, wd_ref, side_work=(attn, scan))
    o_ref[0] = h2 + (0.5 * (1.0 + mod[8:9])) * ff


def _tail(h1, mod, norm3, w_out, wg, wu, wd, q, kv, cs_tab, qw_b, kw_b, sink,
          bc, xdtb, ypart, dt, z, dtb, alog, norm_w):
    b, s, d = h1.shape
    d_ff = wg.shape[1]
    tps = s // SEQ_ROWS
    n_tiles = b * tps
    nsub = SEQ_ROWS // BLOCK
    nb = s // BLOCK
    aw = q.shape[-1]
    kvw2 = kv.shape[-1]
    cs_rows = cs_tab.shape[1]
    _, tri_up = _ssd_constants()

    ffn_tile = lambda k: jnp.clip(n_tiles - k, 0, n_tiles - 1)
    mix_tile = lambda k: jnp.maximum(n_tiles - 1 - k, 0)
    prev_blk = lambda t: jnp.maximum(t * nsub - 1, 0)
    next_blk = lambda t: jnp.minimum((t + 1) * nsub, nb - 1)

    def ffn_tok(w):
        return pl.BlockSpec((1, SEQ_ROWS, w), lambda k: (ffn_tile(k) // tps, ffn_tile(k) % tps, 0))

    def mix_tok(w):
        return pl.BlockSpec((1, SEQ_ROWS, w), lambda k: (mix_tile(k) // tps, mix_tile(k) % tps, 0))

    in_specs = [
        ffn_tok(d),
        pl.BlockSpec((1, N_MOD, d), lambda k: (ffn_tile(k) // tps, 0, 0)),
        _const_spec((1, d)), _const_spec(w_out.shape),
        _const_spec((d, d_ff)), _const_spec((d, d_ff)), _const_spec((d_ff, d)),
        mix_tok(aw),
        pl.BlockSpec((1, BLOCK, kvw2), lambda k: (mix_tile(k) // tps, prev_blk(mix_tile(k) % tps), 0)),
        mix_tok(kvw2),
        pl.BlockSpec((1, BLOCK, kvw2), lambda k: (mix_tile(k) // tps, next_blk(mix_tile(k) % tps), 0)),
        pl.BlockSpec((1, cs_rows, BLOCK), lambda k: (mix_tile(k) // tps, 0, prev_blk(mix_tile(k) % tps))),
        pl.BlockSpec((1, cs_rows, SEQ_ROWS), lambda k: (mix_tile(k) // tps, 0, mix_tile(k) % tps)),
        pl.BlockSpec((1, cs_rows, BLOCK), lambda k: (mix_tile(k) // tps, 0, next_blk(mix_tile(k) % tps))),
        _const_spec((HEAD_DIM, LANES)), _const_spec((HEAD_DIM, LANES)),
        pl.BlockSpec(memory_space=pltpu.SMEM),
        mix_tok(BC_WIDTH), mix_tok(SSD_WIDTH), mix_tok(SSD_WIDTH), mix_tok(LANES), mix_tok(SSD_WIDTH),
        _const_spec((1, LANES)), _const_spec((1, LANES)), _const_spec((1, SSD_WIDTH)),
        _const_spec(tri_up.shape),
    ]
    return pl.pallas_call(
        functools.partial(_tail_kernel, n_tiles, tps),
        grid=(n_tiles + 1,),
        in_specs=in_specs,
        out_specs=ffn_tok(d),
        out_shape=jax.ShapeDtypeStruct((b, s, d), F32),
        scratch_shapes=[pltpu.VMEM((2, SEQ_ROWS, SSD_WIDTH + aw), BF16),
                        pltpu.VMEM((D_STATE, SSD_WIDTH), F32)],
        compiler_params=pltpu.CompilerParams(dimension_semantics=("arbitrary",),
                                             vmem_limit_bytes=VMEM_LIMIT_BYTES),
        name="mixers_outproj_ffn2",
    )(h1, mod, norm3, w_out, wg, wu, wd, q, kv, kv, kv, cs_tab, cs_tab, cs_tab, qw_b, kw_b, sink,
      bc, xdtb, ypart, dt, z, dtb, alog, norm_w, tri_up)


def _pad_inproj(w_in):
    n_dt = 2 * SSD_HEADS
    s_xbc = SSD_WIDTH + CONV_DIM
    s_dt = s_xbc + n_dt
    dt_cols = jnp.pad(w_in[:, s_xbc:s_dt], ((0, 0), (0, LANES - n_dt)))
    return jnp.concatenate([w_in[:, :s_xbc], w_in[:, s_dt:], dt_cols], axis=1).astype(BF16)


def _pad_lanes(v, width=LANES):
    v = v.reshape(1, -1)
    return jnp.pad(v, ((0, 0), (0, width - v.shape[1])))


def _lane_bcast(v):
    return jnp.broadcast_to(v[:, None], (v.shape[0], LANES))


def kernel(x, c, positions, w_ada, b_ada, norm_ffn1, ffn1_wg, ffn1_wu, ffn1_wd, norm_mix, w_in, conv_w,
           conv_b, dt_bias, a_log, d_skip, ssd_norm_w, q_norm_w, k_norm_w, sink_logit, w_out, norm_ffn2,
           ffn2_wg, ffn2_wu, ffn2_wd):
    depth = w_ada.shape[0]
    b, s, d = x.shape
    h = x.astype(F32)
    c_pad = jnp.pad(c.astype(F32), ((0, -b % SUBLANES), (0, 0)))
    for l in range(depth):
        mod = _adaln_mod(c_pad, w_ada[l], b_ada[l])[:b].reshape(b, N_MOD, d)
        conv_w8 = jnp.pad(conv_w[l], ((0, SUBLANES - CONV_K), (0, 0)))
        dtb, alog = _pad_lanes(dt_bias[l]), _pad_lanes(a_log[l])
        h1, z, q, kv, dt, bc, xdtb, ypart, cs_tab, wo_b, wg2_b, wu2_b, wd2_b = _head(
            h, mod, norm_ffn1[l].reshape(1, d), norm_mix[l].reshape(1, d),
            ffn1_wg[l].astype(BF16), ffn1_wu[l].astype(BF16), ffn1_wd[l].astype(BF16), _pad_inproj(w_in[l]),
            conv_w8, conv_b[l].reshape(1, -1), dtb, alog, jnp.repeat(d_skip[l], SSD_HEAD_DIM).reshape(1, -1),
            positions, to_bf16=(w_out[l], ffn2_wg[l], ffn2_wu[l], ffn2_wd[l]))
        h = _tail(h1, mod, norm_ffn2[l].reshape(1, d), wo_b, wg2_b, wu2_b, wd2_b,
                  q, kv, cs_tab, _lane_bcast(q_norm_w[l]), _lane_bcast(k_norm_w[l]), sink_logit[l],
                  bc, xdtb, ypart, dt, z, dtb, alog, ssd_norm_w[l].reshape(1, -1))
    return h.astype(x.dtype)
```

```python
import functools

import jax
import jax.numpy as jnp
import numpy as np
from jax import lax
from jax.experimental import pallas as pl
from jax.experimental.pallas import tpu as pltpu

F32 = jnp.float32
BF16 = jnp.bfloat16

SSD_HEAD_DIM = 64
SSD_HEADS = 8
SSD_GROUPS = 2
D_STATE = 128
CONV_K = 5
CHUNK = 128
HEAD_DIM = 64
ATTN_HEADS = 8
ATTN_KV_HEADS = 2
WINDOW = 128
BLOCK = 128
ROPE_DIMS = 16
ROPE_THETA = 500000.0
N_MOD = 9
EPS = 1e-6

SSD_WIDTH = SSD_HEADS * SSD_HEAD_DIM
BC_WIDTH = 2 * SSD_GROUPS * D_STATE
CONV_DIM = SSD_WIDTH + BC_WIDTH

LANES = 128
SUBLANES = 8
VMEM_LIMIT_BYTES = 56 * 1024 * 1024

SEQ_ROWS = 512
FFN_CHUNK = 256
CONV_HALO = 16

NEG_BIG = -1e30


def _dot(a, b):
    return jnp.dot(a, b, preferred_element_type=F32)


def _dot_nt(a, b):
    return lax.dot_general(a, b, (((1,), (1,)), ((), ())), preferred_element_type=F32)


def _row_sums(tri_bf16, x, terms=3):
    acc = None
    r = x
    for t in range(terms):
        h = r.astype(BF16)
        d = _dot(tri_bf16, h)
        acc = d if acc is None else acc + d
        if t + 1 < terms:
            r = r - h.astype(F32)
    return acc


def _silu(x):
    return x * jax.nn.sigmoid(x)


def _iota2(shape, dim):
    return lax.broadcasted_iota(jnp.int32, shape, dim)


def _const_spec(shape):
    nd = len(shape)
    return pl.BlockSpec(shape, lambda *_: (0,) * nd, pipeline_mode=pl.Buffered(1))


def _mod_kernel(c_ref, w_ref, b_ref, o_ref):
    cs = _silu(c_ref[...])
    o_ref[...] = _dot(cs, w_ref[...]) + b_ref[...]


def _adaln_mod(c_pad, w_ada, b_ada):
    rows, d = c_pad.shape
    n = w_ada.shape[1]
    bn = 3 * d
    return pl.pallas_call(
        _mod_kernel,
        grid=(n // bn,),
        in_specs=[pl.BlockSpec((rows, d), lambda j: (0, 0)),
                  pl.BlockSpec((d, bn), lambda j: (0, j)),
                  pl.BlockSpec((1, bn), lambda j: (0, j))],
        out_specs=pl.BlockSpec((rows, bn), lambda j: (0, j)),
        out_shape=jax.ShapeDtypeStruct((rows, n), F32),
        name="adaln_mod",
    )(c_pad, w_ada, b_ada.reshape(1, n))


def _rope_tile(pos_ref, inv_ref, o_ref):
    half = ROPE_DIMS // 2
    s = pos_ref.shape[-1]
    p = pos_ref[0].astype(F32)
    ang = jnp.tile(inv_ref[...], (1, s // LANES)) * p
    o_ref[0, 0:half, :] = jnp.cos(ang)
    o_ref[0, half:, :] = jnp.sin(ang)


def _rope_inv_freq():
    half = ROPE_DIMS // 2
    inv = ROPE_THETA ** (-jnp.arange(half, dtype=F32) * 2.0 / ROPE_DIMS)
    return jnp.broadcast_to(inv[:, None], (half, LANES))


def _ada_norm(x, gain_scale, shift):
    ms = jnp.mean(x * x, axis=-1, keepdims=True)
    return x * lax.rsqrt(ms + EPS) * gain_scale + shift


def _swiglu(ub, wg_ref, wu_ref, wd_ref, side_work=(), drain=True):
    d_ff = wg_ref.shape[1]
    acc = None
    for c0 in range(0, d_ff, FFN_CHUNK):
        c1 = min(c0 + FFN_CHUNK, d_ff)
        g = _dot(ub, wg_ref[:, c0:c1])
        up = _dot(ub, wu_ref[:, c0:c1])
        a = (_silu(g) * up).astype(BF16)
        d = _dot(a, wd_ref[c0:c1, :])
        acc = d if acc is None else acc + d
        for gen in side_work:
            next(gen, None)
    for gen in side_work if drain else ():
        for _ in gen:
            pass
    return acc


_INPROJ_LAYOUT = (("z", SSD_WIDTH, F32), ("xbc", CONV_DIM, BF16), ("q", ATTN_HEADS * HEAD_DIM, F32),
                  ("kv", 2 * ATTN_KV_HEADS * HEAD_DIM, F32), ("dt", LANES, F32))


def _ssd_constants():
    r = np.arange(CHUNK)[:, None]
    c = np.arange(CHUNK)[None, :]
    return jnp.asarray(c <= r, dtype=BF16), jnp.asarray(c >= r, dtype=BF16)


def _dt_and_da(dt_raw, dtb_ref, alog_ref):
    lane = _iota2(dt_raw.shape, 1)
    dt = jnp.where(lane < 2 * SSD_HEADS, jax.nn.softplus(dt_raw + dtb_ref[...]), 0.0)
    a = -jnp.exp(alog_ref[...])
    return dt, dt * a


def _head_columns(x, lane0):
    return [jnp.broadcast_to(x[:, lane0 + h:lane0 + h + 1], x.shape) for h in range(SSD_HEADS)]


def _expand_heads(cols):
    first = _iota2(cols[0].shape, 1) < SSD_HEAD_DIM
    return jnp.concatenate([jnp.where(first, cols[2 * p], cols[2 * p + 1]) for p in range(SSD_HEADS // 2)],
                           axis=1)


def _ssd_chunk(direction, da, tri_ref, cm, bm, xdt_fn, h_ref):
    gw = SSD_WIDTH // SSD_GROUPS
    lane0 = 0 if direction == "fwd" else SSD_HEADS
    tot_row = CHUNK - 1 if direction == "fwd" else 0
    cgs = [cm[:, g * D_STATE:(g + 1) * D_STATE] for g in range(SSD_GROUPS)]
    bgs = [bm[:, g * D_STATE:(g + 1) * D_STATE] for g in range(SSD_GROUPS)]
    hgs = [h_ref[:, g * gw:(g + 1) * gw] for g in range(SSD_GROUPS)]

    cs = _row_sums(tri_ref[...], da)
    cbs = [_dot_nt(cgs[g], bgs[g]) for g in range(SSD_GROUPS)]
    chs = [_dot(cgs[g], hgs[g].astype(BF16)) for g in range(SSD_GROUPS)]
    yield

    xdt = xdt_fn()
    cs_cols = _head_columns(cs, lane0)
    csx = _expand_heads(cs_cols)
    tot = csx[tot_row:tot_row + 1, :]
    e_in = jnp.exp(csx)
    xdec = (xdt * jnp.exp(tot - csx)).astype(BF16)
    cdec = jnp.exp(tot)
    cst = cs.T
    row = _iota2((CHUNK, CHUNK), 0)
    col = _iota2((CHUNK, CHUNK), 1)
    keep = (col <= row) if direction == "fwd" else (col > row)
    half = _iota2((CHUNK, LANES), 1) // SSD_HEAD_DIM
    xdt_b = xdt.astype(BF16)
    ys = []
    for g in range(SSD_GROUPS):
        pairs = []
        for pp in range(gw // LANES):
            p = g * (gw // LANES) + pp
            xp = xdt_b[:, p * LANES:(p + 1) * LANES]
            yp = None
            for e in range(2):
                hcol = lane0 + 2 * p + e
                seg = cs_cols[2 * p + e] - cst[hcol:hcol + 1, :]
                m = (cbs[g] * jnp.exp(jnp.where(keep, seg, NEG_BIG))).astype(BF16)
                d = _dot(m, jnp.where(half == e, xp, jnp.zeros_like(xp)))
                yp = d if yp is None else yp + d
            pairs.append(yp)
        ys.append(jnp.concatenate(pairs, axis=1) + chs[g] * e_in[:, g * gw:(g + 1) * gw])
        bgt = bgs[g].astype(F32).T.astype(BF16)
        st = _dot(bgt, xdec[:, g * gw:(g + 1) * gw])
        h_ref[:, g * gw:(g + 1) * gw] = hgs[g] * cdec[:, g * gw:(g + 1) * gw] + st
    return jnp.concatenate(ys, axis=1)


def _ssd_fwd_tile(window_fn, dt_fn, cw_ref, cbias_ref, dtb_ref, alog_ref, dskip_ref,
                  tri_ref, bc_ref, xdtb_ref, yp_ref, h_ref):
    mid = CONV_K // 2
    for j in range(SEQ_ROWS // CHUNK):
        rows = pl.ds(j * CHUNK, CHUNK)
        win = window_fn(j)
        dt, da = _dt_and_da(dt_fn(j), dtb_ref, alog_ref)
        dtx_f = _expand_heads(_head_columns(dt, 0))
        dtx_b = _expand_heads(_head_columns(dt, SSD_HEADS))
        yield
        winf = win.astype(F32)
        acc = cbias_ref[...] + winf[CONV_HALO:CONV_HALO + CHUNK] * cw_ref[mid:mid + 1, :]
        for k in [k for k in range(CONV_K) if k != mid]:
            rolled = pltpu.roll(winf, (mid - k) % winf.shape[0], axis=0)
            acc = acc + rolled[CONV_HALO:CONV_HALO + CHUNK] * cw_ref[k:k + 1, :]
        act = _silu(acc)
        xs = act[:, :SSD_WIDTH]
        bc = act[:, SSD_WIDTH:].astype(BF16)
        xdtb_ref[0, rows, :] = (xs * dtx_b).astype(BF16)
        bc_ref[0, rows, :] = bc
        y = yield from _ssd_chunk("fwd", da, tri_ref, bc[:, SSD_GROUPS * D_STATE:],
                                  bc[:, :SSD_GROUPS * D_STATE], lambda: xs * dtx_f, h_ref)
        yp_ref[0, rows, :] = y + xs * dskip_ref[...]
        yield


def _ssd_bwd_tile(bc_ref, xdtb_ref, yp_ref, dt_ref, z_ref, dtb_ref, alog_ref, nw_ref, tri_ref,
                  h_ref, reset, store):
    h_ref[...] = jnp.where(reset, 0.0, h_ref[...])
    for j in reversed(range(bc_ref.shape[1] // CHUNK)):
        rows = pl.ds(j * CHUNK, CHUNK)
        bc = bc_ref[0, rows, :]
        _, da = _dt_and_da(dt_ref[0, rows, :], dtb_ref, alog_ref)
        y = yield from _ssd_chunk("bwd", da, tri_ref, bc[:, SSD_GROUPS * D_STATE:],
                                  bc[:, :SSD_GROUPS * D_STATE],
                                  lambda: xdtb_ref[0, rows, :].astype(F32), h_ref)
        y = (y + yp_ref[0, rows, :]) * _silu(z_ref[0, rows, :])
        ms = jnp.mean(y * y, axis=-1, keepdims=True)
        store(j * CHUNK, (y * lax.rsqrt(ms + EPS) * nw_ref[...]).astype(BF16))
        yield


RING = 3


def _head_kernel(n_tiles, tiles_per_seq,
                 x_ref, mod_ref, n1_ref, n2_ref, wg_ref, wu_ref, wd_ref, win_ref,
                 cw_ref, cbias_ref, dtb_ref, alog_ref, dskip_ref, tri_ref, pos_ref, inv_ref,
                 *rest):
    n_cast = (len(rest) - 14) // 2
    cast_src, rest = rest[:n_cast], rest[n_cast:]
    h_ref, z_ref, q_ref, kv_ref, dt_ref, bc_ref, xdtb_ref, yp_ref, cs_ref = rest[:9]
    cast_dst = rest[9:9 + n_cast]
    xstage, dtstage, xring, dtring, hstate_ref = rest[9 + n_cast:]
    k = pl.program_id(0)

    @pl.when(k == 0)
    def _():
        xstage[...] = jnp.zeros_like(xstage)
        dtstage[...] = jnp.zeros_like(dtstage)
        xring[...] = jnp.zeros_like(xring)
        dtring[...] = jnp.zeros_like(dtring)
        hstate_ref[...] = jnp.zeros_like(hstate_ref)

    t = jnp.maximum(k - 2, 0) % tiles_per_seq
    s_new = (k + RING - 1) % RING
    s_main = (k + RING - 2) % RING
    s_prev = k % RING
    has_prev = t > 0
    has_next = t < tiles_per_seq - 1

    def advance_ring():
        xring[s_new] = xstage[...]
        dtring[s_new] = dtstage[...]
        hstate_ref[...] = jnp.where(has_prev, hstate_ref[...], 0.0)

    def window(j):
        lo = j * CHUNK - CONV_HALO
        hi = (j + 1) * CHUNK + CONV_HALO
        parts = []
        if lo < 0:
            halo = xring[s_prev, SEQ_ROWS + lo:, :]
            parts.append(jnp.where(has_prev, halo, jnp.zeros_like(halo)))
        parts.append(xring[s_main, max(lo, 0):min(hi, SEQ_ROWS), :])
        if hi > SEQ_ROWS:
            halo = xring[s_new, 0:hi - SEQ_ROWS, :]
            parts.append(jnp.where(has_next, halo, jnp.zeros_like(halo)))
        return parts[0] if len(parts) == 1 else jnp.concatenate(parts, axis=0)

    advance_ring()
    scan = _ssd_fwd_tile(window, lambda j: dtring[s_main, j * CHUNK:(j + 1) * CHUNK, :],
                         cw_ref, cbias_ref, dtb_ref, alog_ref, dskip_ref, tri_ref,
                         bc_ref, xdtb_ref, yp_ref, hstate_ref)

    x = x_ref[0]
    mod = mod_ref[0]
    u = _ada_norm(x, n1_ref[...] * (1.0 + mod[1:2]), mod[0:1]).astype(BF16)
    ff = _swiglu(u, wg_ref, wu_ref, wd_ref, side_work=(scan,), drain=False)
    h = x + (0.5 * (1.0 + mod[2:3])) * ff
    h_ref[0] = h

    for src, dst in zip(cast_src, cast_dst):
        dst[...] = src[...].astype(BF16)
    _rope_tile(pos_ref, inv_ref, cs_ref)

    u2 = _ada_norm(h, n2_ref[...] * (1.0 + mod[4:5]), mod[3:4]).astype(BF16)
    dests = {"z": (z_ref,), "xbc": (xstage,), "q": (q_ref,), "kv": (kv_ref,), "dt": (dt_ref, dtstage)}
    c0 = 0
    for name, w, _ in _INPROJ_LAYOUT:
        piece = _dot(u2, win_ref[:, c0:c0 + w])
        for ref in dests[name]:
            if len(ref.shape) == 3:
                ref[0] = piece.astype(ref.dtype)
            else:
                ref[...] = piece.astype(ref.dtype)
        c0 += w
        next(scan, None)
    for _ in scan:
        pass


def _row_block(rows, max_blocks):
    tile = 2 * SUBLANES
    return next(r for r in range(tile, rows + 1, tile) if rows % r == 0 and rows // r <= max_blocks)


def _head(x, mod, norm1, norm2, wg, wu, wd, w_in_p, conv_w8, conv_b, dtb, alog, dskip, positions, to_bf16):
    b, s, d = x.shape
    d_ff = wg.shape[1]
    tps = s // SEQ_ROWS
    n_tiles = b * tps
    tri_lo, _ = _ssd_constants()
    ffn_tile = lambda k: jnp.minimum(k, n_tiles - 1)
    scan_tile = lambda k: jnp.maximum(k - 2, 0)

    def ffn_tok(w):
        return pl.BlockSpec((1, SEQ_ROWS, w), lambda k: (ffn_tile(k) // tps, ffn_tile(k) % tps, 0))

    def scan_tok(w):
        return pl.BlockSpec((1, SEQ_ROWS, w), lambda k: (scan_tile(k) // tps, scan_tile(k) % tps, 0))

    widths = {name: (w, dt) for name, w, dt in _INPROJ_LAYOUT}
    hbm_outs = ("z", "q", "kv", "dt")
    out_shape = [jax.ShapeDtypeStruct((b, s, d), F32)]
    out_shape += [jax.ShapeDtypeStruct((b, s, widths[n][0]), widths[n][1]) for n in hbm_outs]
    out_shape += [jax.ShapeDtypeStruct((b, s, BC_WIDTH), BF16), jax.ShapeDtypeStruct((b, s, SSD_WIDTH), BF16),
                  jax.ShapeDtypeStruct((b, s, SSD_WIDTH), F32), jax.ShapeDtypeStruct((b, ROPE_DIMS, s), F32)]
    out_specs = [ffn_tok(d)] + [ffn_tok(widths[n][0]) for n in hbm_outs]
    out_specs += [scan_tok(BC_WIDTH), scan_tok(SSD_WIDTH), scan_tok(SSD_WIDTH)]

    def ffn_lanes(rows):
        return pl.BlockSpec((1, rows, SEQ_ROWS), lambda k: (ffn_tile(k) // tps, 0, ffn_tile(k) % tps))

    out_specs.append(ffn_lanes(ROPE_DIMS))
    inv = _rope_inv_freq()
    cast_specs = []
    for w in to_bf16:
        rb = _row_block(w.shape[0], n_tiles)
        cast_specs.append(pl.BlockSpec((rb, w.shape[1]),
                                       lambda k, last=w.shape[0] // rb - 1: (jnp.minimum(k, last), 0)))
    out_shape += [jax.ShapeDtypeStruct(w.shape, BF16) for w in to_bf16]
    out_specs += cast_specs
    return pl.pallas_call(
        functools.partial(_head_kernel, n_tiles, tps),
        grid=(n_tiles + 2,),
        in_specs=[ffn_tok(d),
                  pl.BlockSpec((1, N_MOD, d), lambda k: (ffn_tile(k) // tps, 0, 0)),
                  _const_spec((1, d)), _const_spec((1, d)),
                  _const_spec((d, d_ff)), _const_spec((d, d_ff)), _const_spec((d_ff, d)),
                  _const_spec(w_in_p.shape),
                  _const_spec(conv_w8.shape), _const_spec((1, CONV_DIM)),
                  _const_spec((1, LANES)), _const_spec((1, LANES)), _const_spec((1, SSD_WIDTH)),
                  _const_spec(tri_lo.shape), ffn_lanes(1), _const_spec(inv.shape)] + cast_specs,
        out_specs=out_specs,
        out_shape=out_shape,
        scratch_shapes=[pltpu.VMEM((SEQ_ROWS, CONV_DIM), BF16), pltpu.VMEM((SEQ_ROWS, LANES), F32),
                        pltpu.VMEM((RING, SEQ_ROWS, CONV_DIM), BF16), pltpu.VMEM((RING, SEQ_ROWS, LANES), F32),
                        pltpu.VMEM((D_STATE, SSD_WIDTH), F32)],
        compiler_params=pltpu.CompilerParams(dimension_semantics=("arbitrary",),
                                             vmem_limit_bytes=VMEM_LIMIT_BYTES),
        name="ffn1_inproj_scan",
    )(x, mod, norm1, norm2, wg, wu, wd, w_in_p, conv_w8, conv_b, dtb, alog, dskip, tri_lo,
      positions.reshape(b, 1, s), inv, *to_bf16)


def _head_prep_t(t, w_b, cos, sin):
    ms = jnp.sum(t * t, axis=0, keepdims=True) * (1.0 / HEAD_DIM)
    tn = t * lax.rsqrt(ms + EPS) * w_b
    half = ROPE_DIMS // 2
    t1 = tn[0:half]
    t2 = tn[half:ROPE_DIMS]
    return jnp.concatenate([t1 * cos - t2 * sin, t2 * cos + t1 * sin, tn[ROPE_DIMS:]], axis=0)


def _attn_tile(q_ref, kvp_ref, kvc_ref, kvn_ref, csp_ref, csc_ref, csn_ref, qw_ref, kw_ref, sink_ref,
               has_prev, has_next, store):
    tq = q_ref.shape[1]
    nsub = tq // BLOCK
    nw = tq + 2 * BLOCK
    kvw = ATTN_KV_HEADS * HEAD_DIM
    half = ROPE_DIMS // 2
    q_per_kv = ATTN_HEADS // ATTN_KV_HEADS

    kv = jnp.concatenate([kvp_ref[0], kvc_ref[0], kvn_ref[0]], axis=0)
    cs = jnp.concatenate([csp_ref[0], csc_ref[0], csn_ref[0]], axis=1)
    k_t = kv[:, :kvw].T
    v_t = kv[:, kvw:].T.astype(BF16)
    kw_b = jnp.tile(kw_ref[...], (1, nw // LANES))
    k_prep = jnp.concatenate(
        [_head_prep_t(k_t[g * HEAD_DIM:(g + 1) * HEAD_DIM], kw_b, cs[0:half], cs[half:])
         for g in range(ATTN_KV_HEADS)], axis=0)
    keys = k_prep.T.astype(BF16)

    q_t = q_ref[0].T
    qw_b = jnp.tile(qw_ref[...] * (HEAD_DIM ** -0.5), (1, tq // LANES))
    cs_q = csc_ref[0]
    zeros = jnp.zeros((HEAD_DIM, tq), BF16)
    q_heads = []
    for h in range(ATTN_HEADS):
        qh = _head_prep_t(q_t[h * HEAD_DIM:(h + 1) * HEAD_DIM], qw_b, cs_q[0:half], cs_q[half:]).astype(BF16)
        q_heads.append(jnp.concatenate([qh, zeros] if h // q_per_kv == 0 else [zeros, qh], axis=0))

    yield
    kr = _iota2((BLOCK, BLOCK), 0)
    qc = _iota2((BLOCK, BLOCK), 1)
    bias_prev = jnp.where(kr >= qc, 0.0, NEG_BIG)
    bias_next = jnp.where(kr <= qc, 0.0, NEG_BIG)
    sink_row = jnp.concatenate([jnp.full((1, BLOCK), sink_ref[h], F32) for h in range(ATTN_HEADS)], axis=1)
    gq = q_per_kv * BLOCK
    for j in range(nsub):
        bp = bias_prev if j > 0 else jnp.where(has_prev, bias_prev, NEG_BIG)
        bn = bias_next if j < nsub - 1 else jnp.where(has_next, bias_next, NEG_BIG)
        lo, hi = j * BLOCK, (j + 3) * BLOCK
        q_all = jnp.concatenate([qh[:, lo:lo + BLOCK] for qh in q_heads], axis=1)
        s = _dot(keys[lo:hi], q_all)
        yield
        s0 = s[0:BLOCK] + jnp.tile(bp, (1, ATTN_HEADS))
        s1 = s[BLOCK:2 * BLOCK]
        s2 = s[2 * BLOCK:] + jnp.tile(bn, (1, ATTN_HEADS))
        m = jnp.maximum(jnp.max(jnp.maximum(jnp.maximum(s0, s1), s2), axis=0, keepdims=True), sink_row)
        p0 = jnp.exp(s0 - m)
        p1 = jnp.exp(s1 - m)
        p2 = jnp.exp(s2 - m)
        denom = jnp.sum(p0 + p1 + p2, axis=0, keepdims=True) + jnp.exp(sink_row - m)
        p = jnp.concatenate([p0, p1, p2], axis=0).astype(BF16)
        inv = 1.0 / denom
        outs = []
        for g in range(ATTN_KV_HEADS):
            o = _dot(v_t[g * HEAD_DIM:(g + 1) * HEAD_DIM, lo:hi], p[:, g * gq:(g + 1) * gq])
            o = o * inv[:, g * gq:(g + 1) * gq]
            outs += [o[:, k * BLOCK:(k + 1) * BLOCK] for k in range(q_per_kv)]
        store(lo, jnp.concatenate(outs, axis=0).T.astype(BF16))
        yield


def _tail_kernel(n_tiles, tiles_per_seq,
                 h_ref, mod_ref, n3_ref, wo_ref, wg_ref, wu_ref, wd_ref,
                 q_ref, kvp_ref, kvc_ref, kvn_ref, csp_ref, csc_ref, csn_ref, qw_ref, kw_ref, sink_ref,
                 bc_ref, xdtb_ref, yp_ref, dt_ref, z_ref, dtb_ref, alog_ref, nw_ref, tri_ref,
                 o_ref, y_scr, hstate_ref):
    k = pl.program_id(0)

    @pl.when(k == 0)
    def _():
        y_scr[...] = jnp.zeros_like(y_scr)
        hstate_ref[...] = jnp.zeros_like(hstate_ref)

    rd = k % 2
    wr = 1 - rd

    t = jnp.maximum(n_tiles - 1 - k, 0) % tiles_per_seq
    is_first = t == 0
    is_last = t == tiles_per_seq - 1

    def store_ssd(row0, y):
        y_scr[wr, row0:row0 + CHUNK, 0:SSD_WIDTH] = y

    def store_attn(row0, y):
        y_scr[wr, row0:row0 + BLOCK, SSD_WIDTH:] = y

    attn = _attn_tile(q_ref, kvp_ref, kvc_ref, kvn_ref, csp_ref, csc_ref, csn_ref, qw_ref, kw_ref, sink_ref,
                      jnp.logical_not(is_first), jnp.logical_not(is_last), store_attn)
    scan = _ssd_bwd_tile(bc_ref, xdtb_ref, yp_ref, dt_ref, z_ref, dtb_ref, alog_ref, nw_ref, tri_ref,
                         hstate_ref, is_last, store_ssd)

    h1 = h_ref[0]
    mod = mod_ref[0]
    mix = _dot(y_scr[rd], wo_ref[...])
    h2 = h1 + (1.0 + mod[5:6]) * mix
    u = _ada_norm(h2, n3_ref[...] * (1.0 + mod[7:8]), mod[6:7]).astype(BF16)
    next(scan, None)
    ff = _swiglu(u, wg_ref, wu_ref, wd_ref, side_work=(attn, scan))
    o_ref[0] = h2 + (0.5 * (1.0 + mod[8:9])) * ff


def _tail(h1, mod, norm3, w_out, wg, wu, wd, q, kv, cs_tab, qw_b, kw_b, sink,
          bc, xdtb, ypart, dt, z, dtb, alog, norm_w):
    b, s, d = h1.shape
    d_ff = wg.shape[1]
    tps = s // SEQ_ROWS
    n_tiles = b * tps
    nsub = SEQ_ROWS // BLOCK
    nb = s // BLOCK
    aw = q.shape[-1]
    kvw2 = kv.shape[-1]
    cs_rows = cs_tab.shape[1]
    _, tri_up = _ssd_constants()

    ffn_tile = lambda k: jnp.clip(n_tiles - k, 0, n_tiles - 1)
    mix_tile = lambda k: jnp.maximum(n_tiles - 1 - k, 0)
    prev_blk = lambda t: jnp.maximum(t * nsub - 1, 0)
    next_blk = lambda t: jnp.minimum((t + 1) * nsub, nb - 1)

    def ffn_tok(w):
        return pl.BlockSpec((1, SEQ_ROWS, w), lambda k: (ffn_tile(k) // tps, ffn_tile(k) % tps, 0))

    def mix_tok(w):
        return pl.BlockSpec((1, SEQ_ROWS, w), lambda k: (mix_tile(k) // tps, mix_tile(k) % tps, 0))

    in_specs = [
        ffn_tok(d),
        pl.BlockSpec((1, N_MOD, d), lambda k: (ffn_tile(k) // tps, 0, 0)),
        _const_spec((1, d)), _const_spec(w_out.shape),
        _const_spec((d, d_ff)), _const_spec((d, d_ff)), _const_spec((d_ff, d)),
        mix_tok(aw),
        pl.BlockSpec((1, BLOCK, kvw2), lambda k: (mix_tile(k) // tps, prev_blk(mix_tile(k) % tps), 0)),
        mix_tok(kvw2),
        pl.BlockSpec((1, BLOCK, kvw2), lambda k: (mix_tile(k) // tps, next_blk(mix_tile(k) % tps), 0)),
        pl.BlockSpec((1, cs_rows, BLOCK), lambda k: (mix_tile(k) // tps, 0, prev_blk(mix_tile(k) % tps))),
        pl.BlockSpec((1, cs_rows, SEQ_ROWS), lambda k: (mix_tile(k) // tps, 0, mix_tile(k) % tps)),
        pl.BlockSpec((1, cs_rows, BLOCK), lambda k: (mix_tile(k) // tps, 0, next_blk(mix_tile(k) % tps))),
        _const_spec((HEAD_DIM, LANES)), _const_spec((HEAD_DIM, LANES)),
        pl.BlockSpec(memory_space=pltpu.SMEM),
        mix_tok(BC_WIDTH), mix_tok(SSD_WIDTH), mix_tok(SSD_WIDTH), mix_tok(LANES), mix_tok(SSD_WIDTH),
        _const_spec((1, LANES)), _const_spec((1, LANES)), _const_spec((1, SSD_WIDTH)),
        _const_spec(tri_up.shape),
    ]
    return pl.pallas_call(
        functools.partial(_tail_kernel, n_tiles, tps),
        grid=(n_tiles + 1,),
        in_specs=in_specs,
        out_specs=ffn_tok(d),
        out_shape=jax.ShapeDtypeStruct((b, s, d), F32),
        scratch_shapes=[pltpu.VMEM((2, SEQ_ROWS, SSD_WIDTH + aw), BF16),
                        pltpu.VMEM((D_STATE, SSD_WIDTH), F32)],
        compiler_params=pltpu.CompilerParams(dimension_semantics=("arbitrary",),
                                             vmem_limit_bytes=VMEM_LIMIT_BYTES),
        name="mixers_outproj_ffn2",
    )(h1, mod, norm3, w_out, wg, wu, wd, q, kv, kv, kv, cs_tab, cs_tab, cs_tab, qw_b, kw_b, sink,
      bc, xdtb, ypart, dt, z, dtb, alog, norm_w, tri_up)


def _pad_inproj(w_in):
    n_dt = 2 * SSD_HEADS
    s_xbc = SSD_WIDTH + CONV_DIM
    s_dt = s_xbc + n_dt
    dt_cols = jnp.pad(w_in[:, s_xbc:s_dt], ((0, 0), (0, LANES - n_dt)))
    return jnp.concatenate([w_in[:, :s_xbc], w_in[:, s_dt:], dt_cols], axis=1).astype(BF16)


def _pad_lanes(v, width=LANES):
    v = v.reshape(1, -1)
    return jnp.pad(v, ((0, 0), (0, width - v.shape[1])))


def _lane_bcast(v):
    return jnp.broadcast_to(v[:, None], (v.shape[0], LANES))


def kernel(x, c, positions, w_ada, b_ada, norm_ffn1, ffn1_wg, ffn1_wu, ffn1_wd, norm_mix, w_in, conv_w,
           conv_b, dt_bias, a_log, d_skip, ssd_norm_w, q_norm_w, k_norm_w, sink_logit, w_out, norm_ffn2,
           ffn2_wg, ffn2_wu, ffn2_wd):
    depth = w_ada.shape[0]
    b, s, d = x.shape
    h = x.astype(F32)
    c_pad = jnp.pad(c.astype(F32), ((0, -b % SUBLANES), (0, 0)))
    for l in range(depth):
        mod = _adaln_mod(c_pad, w_ada[l], b_ada[l])[:b].reshape(b, N_MOD, d)
        conv_w8 = jnp.pad(conv_w[l], ((0, SUBLANES - CONV_K), (0, 0)))
        dtb, alog = _pad_lanes(dt_bias[l]), _pad_lanes(a_log[l])
        h1, z, q, kv, dt, bc, xdtb, ypart, cs_tab, wo_b, wg2_b, wu2_b, wd2_b = _head(
            h, mod, norm_ffn1[l].reshape(1, d), norm_mix[l].reshape(1, d),
            ffn1_wg[l].astype(BF16), ffn1_wu[l].astype(BF16), ffn1_wd[l].astype(BF16), _pad_inproj(w_in[l]),
            conv_w8, conv_b[l].reshape(1, -1), dtb, alog, jnp.repeat(d_skip[l], SSD_HEAD_DIM).reshape(1, -1),
            positions, to_bf16=(w_out[l], ffn2_wg[l], ffn2_wu[l], ffn2_wd[l]))
        h = _tail(h1, mod, norm_ffn2[l].reshape(1, d), wo_b, wg2_b, wu2_b, wd2_b,
                  q, kv, cs_tab, _lane_bcast(q_norm_w[l]), _lane_bcast(k_norm_w[l]), sink_logit[l],
                  bc, xdtb, ypart, dt, z, dtb, alog, ssd_norm_w[l].reshape(1, -1))
    return h.astype(x.dtype)
```
